```python
import math
import jax, jax.numpy as jnp
from jax import lax
import numpy as np

D_MODEL = 1024
BATCH = 4
SEQ = 8192
DEPTH = 1
DEC_BATCH = 1
DEC_SEQ = 16384
PAST_LEN = 128

PLE_DIM = 256
ATTN_WIDTH = D_MODEL // 2
SSM_WIDTH = D_MODEL - ATTN_WIDTH
DIFF_HEAD_DIM = 64
N_DIFF_HEADS = ATTN_WIDTH // (2 * DIFF_HEAD_DIM)
SSM_GROUP = 16
N_SSM_GROUPS = SSM_WIDTH // SSM_GROUP
SSM_STATE = 64
IN_WIDTH = 3 * ATTN_WIDTH + SSM_WIDTH
N_EXPERTS = 32
TOP_K = 4
D_FF = D_MODEL
SWIGLU_LIMIT = 7.0
SWIGLU_ALPHA = 1.702
Q_BLOCK = 128
EXPERT_BLOCK = 128
EPS = 1e-6

kernel_name = 'hybrid_diffattn_s5_moe_encoder'


def _rmsnorm(x, g):
    xf = x.astype(jnp.float32)
    y = xf * lax.rsqrt(jnp.mean(xf * xf, axis=-1, keepdims=True) + EPS)
    return (y * g.astype(jnp.float32)).astype(x.dtype)


def _diff_attention(q1, q2, k1, k2, v, lam):
    b, l, h, d = q1.shape
    nb = l // Q_BLOCK
    scale = 1.0 / math.sqrt(d)
    slopes = jnp.exp2(-8.0 * jnp.arange(1, h + 1, dtype=jnp.float32) / h)
    kpos = jnp.arange(l, dtype=jnp.int32)

    def block(args):
        q1b, q2b, start = args
        qpos = start + jnp.arange(Q_BLOCK, dtype=jnp.int32)
        dist = jnp.abs(qpos[:, None] - kpos[None, :]).astype(jnp.float32)
        bias = -slopes[:, None, None] * dist[None]
        s1 = jnp.einsum('bqhd,bkhd->bhqk', q1b, k1).astype(jnp.float32) * scale + bias
        s2 = jnp.einsum('bqhd,bkhd->bhqk', q2b, k2).astype(jnp.float32) * scale + bias
        attn = jax.nn.softmax(s1, axis=-1) - lam * jax.nn.softmax(s2, axis=-1)
        return jnp.einsum('bhqk,bkhe->bqhe', attn.astype(v.dtype), v)

    q1b = q1.reshape(b, nb, Q_BLOCK, h, d).swapaxes(0, 1)
    q2b = q2.reshape(b, nb, Q_BLOCK, h, d).swapaxes(0, 1)
    starts = jnp.arange(nb, dtype=jnp.int32) * Q_BLOCK
    out = lax.map(block, (q1b, q2b, starts))
    return out.swapaxes(0, 1).reshape(b, l, h, v.shape[-1])


def _cplx_scan_op(c1, c2):
    a1r, a1i, b1r, b1i = c1
    a2r, a2i, b2r, b2i = c2
    ar = a2r * a1r - a2i * a1i
    ai = a2r * a1i + a2i * a1r
    br = a2r * b1r - a2i * b1i + b2r
    bi = a2r * b1i + a2i * b1r + b2i
    return (ar, ai, br, bi)


def _s5(u, lam_re, lam_im, log_dt, b_re, b_im, c_re, c_im, d_skip, w_glu, b_glu):
    b, l, _ = u.shape
    ug = u.reshape(b, l, N_SSM_GROUPS, SSM_GROUP)
    y = d_skip * u
    for dr in range(2):
        dt = jnp.exp(log_dt[dr])[:, None]
        lr, li = lam_re[dr], lam_im[dr]
        mag = jnp.exp(lr * dt)
        ab_re = mag * jnp.cos(li * dt)
        ab_im = mag * jnp.sin(li * dt)
        den = lr * lr + li * li
        nr, ni = ab_re - 1.0, ab_im
        co_re = (nr * lr + ni * li) / den
        co_im = (ni * lr - nr * li) / den
        bb_re = co_re[..., None] * b_re[dr] - co_im[..., None] * b_im[dr]
        bb_im = co_re[..., None] * b_im[dr] + co_im[..., None] * b_re[dr]
        bu_re = jnp.einsum('blgh,gph->blgp', ug, bb_re)
        bu_im = jnp.einsum('blgh,gph->blgp', ug, bb_im)
        a_re = jnp.broadcast_to(ab_re, bu_re.shape)
        a_im = jnp.broadcast_to(ab_im, bu_im.shape)
        _, _, x_re, x_im = lax.associative_scan(
            _cplx_scan_op, (a_re, a_im, bu_re, bu_im), reverse=(dr == 1), axis=1)
        y_dir = (jnp.einsum('blgp,ghp->blgh', x_re, c_re[dr])
                 - jnp.einsum('blgp,ghp->blgh', x_im, c_im[dr]))
        y = y + y_dir.reshape(b, l, SSM_WIDTH)
    y = jax.nn.gelu(y)
    return y * jax.nn.sigmoid(y @ w_glu + b_glu)


def _moe(x, w_router, b_router, w_gate, b_gate, w_up, b_up, w_down, b_down):
    b, l, d = x.shape
    t = b * l
    xt = x.reshape(t, d)
    logits = (xt @ w_router).astype(jnp.float32) + b_router.astype(jnp.float32)
    top_v, top_i = lax.top_k(logits, TOP_K)
    gates = jax.nn.softmax(top_v, axis=-1)
    n_assign = t * TOP_K
    e_flat = top_i.reshape(n_assign).astype(jnp.int32)
    tok_flat = jnp.arange(n_assign, dtype=jnp.int32) // TOP_K
    g_flat = gates.reshape(n_assign)
    order = jnp.argsort(e_flat)
    e_s, tok_s, g_s = e_flat[order], tok_flat[order], g_flat[order]
    counts = jnp.zeros((N_EXPERTS,), jnp.int32).at[e_flat].add(1)
    starts = jnp.cumsum(counts) - counts
    padded = ((counts + EXPERT_BLOCK - 1) // EXPERT_BLOCK) * EXPERT_BLOCK
    pends = jnp.cumsum(padded)
    pstarts = pends - padded
    dest = pstarts[e_s] + (jnp.arange(n_assign, dtype=jnp.int32) - starts[e_s])
    n_blocks = -(-(n_assign + N_EXPERTS * (EXPERT_BLOCK - 1)) // EXPERT_BLOCK)
    n_pad = n_blocks * EXPERT_BLOCK
    buf_tok = jnp.full((n_pad,), t, jnp.int32).at[dest].set(tok_s)
    buf_gate = jnp.zeros((n_pad,), jnp.float32).at[dest].set(g_s)
    blk_start = jnp.arange(n_blocks, dtype=jnp.int32) * EXPERT_BLOCK
    blk_expert = jnp.minimum(jnp.searchsorted(pends, blk_start, side='right'), N_EXPERTS - 1)
    x_pad = jnp.concatenate([xt, jnp.zeros((1, d), xt.dtype)], axis=0)

    def run_block(args):
        tok_b, g_b, e = args
        xb = x_pad[tok_b]
        gt = jnp.minimum(xb @ w_gate[e] + b_gate[e], SWIGLU_LIMIT)
        up = jnp.clip(xb @ w_up[e] + b_up[e], -SWIGLU_LIMIT, SWIGLU_LIMIT)
        hh = (up + 1.0) * (gt * jax.nn.sigmoid(SWIGLU_ALPHA * gt))
        yb = hh @ w_down[e] + b_down[e]
        return yb * g_b[:, None].astype(yb.dtype)

    y_blocks = lax.map(run_block, (buf_tok.reshape(n_blocks, EXPERT_BLOCK),
                                   buf_gate.reshape(n_blocks, EXPERT_BLOCK), blk_expert))
    y = jnp.zeros((t + 1, d), y_blocks.dtype).at[buf_tok].add(y_blocks.reshape(n_pad, d))
    return y[:t].reshape(b, l, d)


def _layer(h, p_i, lambda_init, g_mix, w_in, lambda_q1, lambda_k1, lambda_q2, lambda_k2,
           g_subln, ssm_lambda_re, ssm_lambda_im, ssm_log_dt, ssm_b_re, ssm_b_im,
           ssm_c_re, ssm_c_im, ssm_d, w_glu, b_glu, w_out, g_ffn, w_router, b_router,
           w_gate, b_gate, w_up, b_up, w_down, b_down, g_ple, w_ple_gate, w_ple_proj):
    b, l, _ = h.shape
    hn = _rmsnorm(h, g_mix)
    proj = hn @ w_in
    q, k, v, u = jnp.split(proj, [ATTN_WIDTH, 2 * ATTN_WIDTH, 3 * ATTN_WIDTH], axis=-1)
    q = q.reshape(b, l, N_DIFF_HEADS, 2, DIFF_HEAD_DIM)
    k = k.reshape(b, l, N_DIFF_HEADS, 2, DIFF_HEAD_DIM)
    v = v.reshape(b, l, N_DIFF_HEADS, 2 * DIFF_HEAD_DIM)
    f32 = jnp.float32
    lam = (jnp.exp(jnp.sum(lambda_q1.astype(f32) * lambda_k1.astype(f32)))
           - jnp.exp(jnp.sum(lambda_q2.astype(f32) * lambda_k2.astype(f32))) + lambda_init)
    a = _diff_attention(q[..., 0, :], q[..., 1, :], k[..., 0, :], k[..., 1, :], v, lam)
    a = (_rmsnorm(a, g_subln) * (1.0 - lambda_init)).reshape(b, l, ATTN_WIDTH)
    s = _s5(u, ssm_lambda_re, ssm_lambda_im, ssm_log_dt, ssm_b_re, ssm_b_im,
            ssm_c_re, ssm_c_im, ssm_d, w_glu, b_glu)
    h = h + jnp.concatenate([a, s.astype(a.dtype)], axis=-1) @ w_out
    h = h + _moe(_rmsnorm(h, g_ffn), w_router, b_router, w_gate, b_gate,
                 w_up, b_up, w_down, b_down)
    gate = jax.nn.sigmoid(_rmsnorm(h, g_ple) @ w_ple_gate)
    return h + gate * (p_i @ w_ple_proj)


def _trunk(x, p, params, g_final):
    h = x
    for i in range(DEPTH):
        lambda_init = 0.8 - 0.6 * math.exp(-0.3 * i)
        h = _layer(h, p[i], lambda_init, *[w[i] for w in params])
    return _rmsnorm(h, g_final)


def setup_inputs(seed: int = 0) -> dict:
    key = jax.random.key(seed)
    ks = jax.random.split(key, 40)
    f32 = jnp.float32

    def nrm(k, shape, scale):
        return jax.random.normal(k, shape, f32) * scale

    G, P, H = N_SSM_GROUPS, SSM_STATE, SSM_GROUP
    lam_re = -0.5 + nrm(ks[10], (DEPTH, 2, G, P), 0.01)
    lam_im = (math.pi * jnp.arange(P, dtype=f32))[None, None, None, :] + nrm(ks[11], (DEPTH, 2, G, P), 0.01)
    log_dt = jax.random.uniform(ks[12], (DEPTH, 2, G), f32, math.log(1e-3), math.log(1e-1))
    return {
        'x_prompt': nrm(ks[0], (BATCH, SEQ, D_MODEL), 1.0),
        'x_sample': nrm(ks[1], (DEC_BATCH, DEC_SEQ, D_MODEL), 1.0),
        'p_prompt': nrm(ks[2], (DEPTH, BATCH, SEQ, PLE_DIM), 1.0),
        'p_sample': nrm(ks[3], (DEPTH, DEC_BATCH, DEC_SEQ, PLE_DIM), 1.0),
        'g_mix': 1.0 + nrm(ks[4], (DEPTH, D_MODEL), 0.02),
        'w_in': nrm(ks[5], (DEPTH, D_MODEL, IN_WIDTH), D_MODEL ** -0.5),
        'lambda_q1': nrm(ks[6], (DEPTH, DIFF_HEAD_DIM), 0.1),
        'lambda_k1': nrm(ks[7], (DEPTH, DIFF_HEAD_DIM), 0.1),
        'lambda_q2': nrm(ks[8], (DEPTH, DIFF_HEAD_DIM), 0.1),
        'lambda_k2': nrm(ks[9], (DEPTH, DIFF_HEAD_DIM), 0.1),
        'g_subln': 1.0 + nrm(ks[13], (DEPTH, 2 * DIFF_HEAD_DIM), 0.02),
        'ssm_lambda_re': lam_re,
        'ssm_lambda_im': lam_im,
        'ssm_log_dt': log_dt,
        'ssm_b_re': nrm(ks[14], (DEPTH, 2, G, P, H), (2 * H) ** -0.5),
        'ssm_b_im': nrm(ks[15], (DEPTH, 2, G, P, H), (2 * H) ** -0.5),
        'ssm_c_re': nrm(ks[16], (DEPTH, 2, G, H, P), (2 * P) ** -0.5),
        'ssm_c_im': nrm(ks[17], (DEPTH, 2, G, H, P), (2 * P) ** -0.5),
        'ssm_d': nrm(ks[18], (DEPTH, SSM_WIDTH), 1.0),
        'w_glu': nrm(ks[19], (DEPTH, SSM_WIDTH, SSM_WIDTH), SSM_WIDTH ** -0.5),
        'b_glu': nrm(ks[20], (DEPTH, SSM_WIDTH), 0.02),
        'w_out': nrm(ks[21], (DEPTH, D_MODEL, D_MODEL), D_MODEL ** -0.5),
        'g_ffn': 1.0 + nrm(ks[22], (DEPTH, D_MODEL), 0.02),
        'w_router': nrm(ks[23], (DEPTH, D_MODEL, N_EXPERTS), D_MODEL ** -0.5),
        'b_router': nrm(ks[24], (DEPTH, N_EXPERTS), 0.01),
        'w_gate': nrm(ks[25], (DEPTH, N_EXPERTS, D_MODEL, D_FF), D_MODEL ** -0.5),
        'b_gate': nrm(ks[26], (DEPTH, N_EXPERTS, D_FF), 0.02),
        'w_up': nrm(ks[27], (DEPTH, N_EXPERTS, D_MODEL, D_FF), D_MODEL ** -0.5),
        'b_up': nrm(ks[28], (DEPTH, N_EXPERTS, D_FF), 0.02),
        'w_down': nrm(ks[29], (DEPTH, N_EXPERTS, D_FF, D_MODEL), D_FF ** -0.5),
        'b_down': nrm(ks[30], (DEPTH, N_EXPERTS, D_MODEL), 0.02),
        'g_ple': 1.0 + nrm(ks[31], (DEPTH, D_MODEL), 0.02),
        'w_ple_gate': nrm(ks[32], (DEPTH, D_MODEL, D_MODEL), D_MODEL ** -0.5),
        'w_ple_proj': nrm(ks[33], (DEPTH, PLE_DIM, D_MODEL), PLE_DIM ** -0.5),
        'g_final': 1.0 + nrm(ks[34], (D_MODEL,), 0.02),
    }


def reference(x_prompt, x_sample, p_prompt, p_sample, g_mix, w_in, lambda_q1, lambda_k1,
              lambda_q2, lambda_k2, g_subln, ssm_lambda_re, ssm_lambda_im, ssm_log_dt,
              ssm_b_re, ssm_b_im, ssm_c_re, ssm_c_im, ssm_d, w_glu, b_glu, w_out, g_ffn,
              w_router, b_router, w_gate, b_gate, w_up, b_up, w_down, b_down, g_ple,
              w_ple_gate, w_ple_proj, g_final):
    params = (g_mix, w_in, lambda_q1, lambda_k1, lambda_q2, lambda_k2, g_subln,
              ssm_lambda_re, ssm_lambda_im, ssm_log_dt, ssm_b_re, ssm_b_im, ssm_c_re,
              ssm_c_im, ssm_d, w_glu, b_glu, w_out, g_ffn, w_router, b_router, w_gate,
              b_gate, w_up, b_up, w_down, b_down, g_ple, w_ple_gate, w_ple_proj)
    y_prompt = _trunk(x_prompt, p_prompt, params, g_final)
    y_sample = _trunk(x_sample, p_sample, params, g_final)
    return (y_prompt, y_sample)
```

```python
import functools
import math

import jax
import jax.numpy as jnp
import numpy as np
from jax import lax
from jax.experimental import pallas as pl
from jax.experimental.pallas import tpu as pltpu

F32 = jnp.float32
BF16 = jnp.bfloat16

D_MODEL = 1024
PLE_DIM = 256
ATTN_WIDTH = 512
SSM_WIDTH = 512
HEAD_W = 128
DIFF_HEAD_DIM = 64
N_HEADS = 4
SSM_GROUP = 16
N_GROUPS = 32
SSM_STATE = 64
N_STATE = N_GROUPS * SSM_STATE
N_EXPERTS = 32
TOP_K = 4
SWIGLU_LIMIT = 7.0
SWIGLU_ALPHA = 1.702
EPS = 1e-6
LANES = 128

VMEM_LIMIT = 56 * 1024 * 1024


def _cparams(sem):
    return pltpu.CompilerParams(dimension_semantics=sem, vmem_limit_bytes=VMEM_LIMIT)


def _rms(x, g):
    return x * lax.rsqrt(jnp.mean(x * x, axis=-1, keepdims=True) + EPS) * g


def _inproj_kernel(x_ref, g_ref, w_ref, q_ref, k_ref, v_ref, u_ref):
    xn = _rms(x_ref[...], g_ref[...]).astype(BF16)
    proj = jnp.dot(xn, w_ref[...], preferred_element_type=F32)
    scale = 1.0 / math.sqrt(DIFF_HEAD_DIM)
    q_ref[...] = (proj[:, :ATTN_WIDTH] * scale).astype(BF16)
    k_ref[...] = proj[:, ATTN_WIDTH:2 * ATTN_WIDTH].astype(BF16)
    v_ref[...] = proj[:, 2 * ATTN_WIDTH:3 * ATTN_WIDTH].astype(BF16)
    u_ref[...] = proj[:, 3 * ATTN_WIDTH:]


def _inproj(x, g_mix, w_in_bf, tm):
    t = x.shape[0]
    row = lambda i: (i, 0)
    const = lambda i: (0, 0)
    return pl.pallas_call(
        _inproj_kernel,
        grid=(t // tm,),
        in_specs=[pl.BlockSpec((tm, D_MODEL), row),
                  pl.BlockSpec((1, D_MODEL), const),
                  pl.BlockSpec(w_in_bf.shape, const)],
        out_specs=[pl.BlockSpec((tm, ATTN_WIDTH), row)] * 3 + [pl.BlockSpec((tm, SSM_WIDTH), row)],
        out_shape=[jax.ShapeDtypeStruct((t, ATTN_WIDTH), BF16)] * 3
        + [jax.ShapeDtypeStruct((t, SSM_WIDTH), F32)],
        compiler_params=_cparams(("parallel",)),
        name="inproj",
    )(x, g_mix.reshape(1, D_MODEL), w_in_bf)


def _attn_kernel(sc_ref, q_ref, k_ref, v_ref, g_ref, o_ref,
                 m1_ref, l1_ref, a1_ref, m2_ref, l2_ref, a2_ref, *, tq, tk, out_scale):
    h = pl.program_id(1)
    qi = pl.program_id(2)
    ki = pl.program_id(3)

    @pl.when(ki == 0)
    def _():
        m1_ref[...] = jnp.full_like(m1_ref, -jnp.inf)
        m2_ref[...] = jnp.full_like(m2_ref, -jnp.inf)
        l1_ref[...] = jnp.zeros_like(l1_ref)
        l2_ref[...] = jnp.zeros_like(l2_ref)
        a1_ref[...] = jnp.zeros_like(a1_ref)
        a2_ref[...] = jnp.zeros_like(a2_ref)

    slope = sc_ref[1 + h]
    q = q_ref[0]
    k = k_ref[0]
    v = v_ref[0]
    lane = lax.broadcasted_iota(jnp.int32, q.shape, 1)
    zero = jnp.zeros_like(q)
    q1 = jnp.where(lane < DIFF_HEAD_DIM, q, zero)
    q2 = jnp.where(lane >= DIFF_HEAD_DIM, q, zero)
    qpos = qi * tq + lax.broadcasted_iota(jnp.int32, (tq, tk), 0)
    kpos = ki * tk + lax.broadcasted_iota(jnp.int32, (tq, tk), 1)
    bias = jnp.abs(qpos - kpos).astype(F32) * (-slope)
    nt = (((1,), (1,)), ((), ()))

    def stream(qm, m_ref, l_ref, a_ref):
        s = lax.dot_general(qm, k, nt, preferred_element_type=F32) + bias
        m_old = m_ref[...]
        m_new = jnp.maximum(m_old, jnp.max(s, axis=-1, keepdims=True))
        alpha = jnp.exp(m_old - m_new)
        p = jnp.exp(s - m_new)
        l_ref[...] = alpha * l_ref[...] + jnp.sum(p, axis=-1, keepdims=True)
        a_ref[...] = alpha * a_ref[...] + jnp.dot(p.astype(BF16), v, preferred_element_type=F32)
        m_ref[...] = m_new

    stream(q1, m1_ref, l1_ref, a1_ref)
    stream(q2, m2_ref, l2_ref, a2_ref)

    @pl.when(ki == pl.num_programs(3) - 1)
    def _():
        lam = sc_ref[0]
        out = a1_ref[...] / l1_ref[...] - lam * (a2_ref[...] / l2_ref[...])
        o_ref[0] = (_rms(out, g_ref[...]) * out_scale).astype(o_ref.dtype)


def _attention(q, k, v, scal, g_subln, lambda_init, tq, tk):
    b, l, _ = q.shape
    kernel = functools.partial(_attn_kernel, tq=tq, tk=tk, out_scale=1.0 - lambda_init)
    qmap = lambda bi, h, qi, ki: (bi, qi, h)
    kmap = lambda bi, h, qi, ki: (bi, ki, h)
    return pl.pallas_call(
        kernel,
        grid=(b, N_HEADS, l // tq, l // tk),
        in_specs=[pl.BlockSpec(memory_space=pltpu.SMEM),
                  pl.BlockSpec((1, tq, HEAD_W), qmap),
                  pl.BlockSpec((1, tk, HEAD_W), kmap),
                  pl.BlockSpec((1, tk, HEAD_W), kmap),
                  pl.BlockSpec((1, HEAD_W), lambda bi, h, qi, ki: (0, 0))],
        out_specs=pl.BlockSpec((1, tq, HEAD_W), qmap),
        out_shape=jax.ShapeDtypeStruct((b, l, ATTN_WIDTH), BF16),
        scratch_shapes=[pltpu.VMEM((tq, 1), F32), pltpu.VMEM((tq, 1), F32), pltpu.VMEM((tq, HEAD_W), F32),
                        pltpu.VMEM((tq, 1), F32), pltpu.VMEM((tq, 1), F32), pltpu.VMEM((tq, HEAD_W), F32)],
        compiler_params=_cparams(("parallel", "parallel", "parallel", "arbitrary")),
        name="attn",
    )(scal, q, k, v, g_subln.reshape(1, HEAD_W))


SCAN_LANES = 512


def _s5_kernel(u_ref, pm_ref, pmt_ref, bb_ref, cc_ref, a_ref, at_ref, apow_ref, y_ref,
               xs_ref, xb_ref, carry_ref, *, tc, reverse):
    tc8 = tc // 8
    c = pl.program_id(1)

    @pl.when(c == 0)
    def _():
        carry_ref[...] = jnp.zeros_like(carry_ref)

    up = jnp.dot(pm_ref[...], u_ref[0].astype(BF16), preferred_element_type=F32).astype(BF16)
    xs_ref[...] = jnp.dot(up, bb_ref[...], preferred_element_type=F32)

    for cb in range(N_STATE // SCAN_LANES):
        lo = cb * SCAN_LANES
        re_sl = pl.ds(lo, SCAN_LANES)
        im_sl = pl.ds(N_STATE + lo, SCAN_LANES)
        ar = jnp.broadcast_to(a_ref[0:1, lo:lo + SCAN_LANES], (8, SCAN_LANES))
        ai = jnp.broadcast_to(a_ref[1:2, lo:lo + SCAN_LANES], (8, SCAN_LANES))

        def step(i, carry):
            xr, xi = carry
            r = (tc8 - 1 - i) if reverse else i
            rows = pl.ds(pl.multiple_of(r * 8, 8), 8)
            nr = ar * xr - ai * xi + xs_ref[rows, re_sl]
            ni = ar * xi + ai * xr + xs_ref[rows, im_sl]
            xs_ref[rows, re_sl] = nr
            xs_ref[rows, im_sl] = ni
            return nr, ni

        z = jnp.zeros((8, SCAN_LANES), F32)
        lax.fori_loop(0, tc8, step, (z, z), unroll=4)

    last = 0 if reverse else (tc8 - 1) * 8
    e_re = xs_ref[last:last + 8, 0:N_STATE]
    e_im = xs_ref[last:last + 8, N_STATE:2 * N_STATE]
    atr = at_ref[0:1, :]
    ati = at_ref[1:2, :]
    cr = carry_ref[0:1, :]
    ci = carry_ref[1:2, :]
    cin_r = [None] * 8
    cin_i = [None] * 8
    order = range(7, -1, -1) if reverse else range(8)
    for j in order:
        cin_r[j] = cr
        cin_i[j] = ci
        nr = atr * cr - ati * ci + e_re[j:j + 1]
        ni = atr * ci + ati * cr + e_im[j:j + 1]
        cr, ci = nr, ni
    carry_ref[0:1, :] = cr
    carry_ref[1:2, :] = ci
    cin_r = jnp.concatenate(cin_r + cin_r, axis=0)
    cin_i = jnp.concatenate(cin_i + cin_i, axis=0)

    def fix(i, _):
        rows = pl.ds(pl.multiple_of(i * 16, 16), 16)
        pr = apow_ref[rows, 0:N_STATE]
        pi = apow_ref[rows, N_STATE:2 * N_STATE]
        xr = xs_ref[rows, 0:N_STATE] + pr * cin_r - pi * cin_i
        xi = xs_ref[rows, N_STATE:2 * N_STATE] + pr * cin_i + pi * cin_r
        xb_ref[rows, 0:N_STATE] = xr.astype(BF16)
        xb_ref[rows, N_STATE:2 * N_STATE] = xi.astype(BF16)
        return 0

    lax.fori_loop(0, tc // 16, fix, 0)

    yp = jnp.dot(xb_ref[...], cc_ref[...], preferred_element_type=F32)
    hi = yp.astype(BF16)
    lo_part = (yp - hi.astype(F32)).astype(BF16)
    y_ref[0] = (jnp.dot(pmt_ref[...], hi, preferred_element_type=F32)
                + jnp.dot(pmt_ref[...], lo_part, preferred_element_type=F32))


def _s5_direction(u, pm, pmt, bb, cc, a, at, apow, tc, reverse):
    b, l, _ = u.shape
    nc = l // tc
    cmap = (lambda bi, c: (bi, nc - 1 - c, 0)) if reverse else (lambda bi, c: (bi, c, 0))
    const = lambda bi, c: (0, 0)
    kernel = functools.partial(_s5_kernel, tc=tc, reverse=reverse)
    return pl.pallas_call(
        kernel,
        grid=(b, nc),
        in_specs=[pl.BlockSpec((1, tc, SSM_WIDTH), cmap),
                  pl.BlockSpec(pm.shape, const), pl.BlockSpec(pmt.shape, const),
                  pl.BlockSpec(bb.shape, const), pl.BlockSpec(cc.shape, const),
                  pl.BlockSpec(a.shape, const), pl.BlockSpec(at.shape, const),
                  pl.BlockSpec(apow.shape, const)],
        out_specs=pl.BlockSpec((1, tc, SSM_WIDTH), cmap),
        out_shape=jax.ShapeDtypeStruct((b, l, SSM_WIDTH), F32),
        scratch_shapes=[pltpu.VMEM((tc, 2 * N_STATE), F32), pltpu.VMEM((tc, 2 * N_STATE), BF16),
                        pltpu.VMEM((2, N_STATE), F32)],
        compiler_params=_cparams(("parallel", "arbitrary")),
        name="s5_bwd" if reverse else "s5_fwd",
    )(u, pm, pmt, bb, cc, a, at, apow)


def _s5_params(lam_re, lam_im, log_dt, b_re, b_im, c_re, c_im, tc, reverse):
    tc8 = tc // 8
    dt = jnp.exp(log_dt)[:, None]
    mag = jnp.exp(lam_re * dt)
    ab_re = mag * jnp.cos(lam_im * dt)
    ab_im = mag * jnp.sin(lam_im * dt)
    den = lam_re * lam_re + lam_im * lam_im
    nr, ni = ab_re - 1.0, ab_im
    co_re = (nr * lam_re + ni * lam_im) / den
    co_im = (ni * lam_re - nr * lam_im) / den
    bb_re = co_re[..., None] * b_re - co_im[..., None] * b_im
    bb_im = co_re[..., None] * b_im + co_im[..., None] * b_re
    eye = jnp.eye(N_GROUPS, dtype=F32)
    blk_in = lambda w: jnp.einsum('gph,gk->ghkp', w, eye).reshape(SSM_WIDTH, N_STATE)
    bb = jnp.concatenate([blk_in(bb_re), blk_in(bb_im)], axis=1).astype(BF16)
    blk_out = lambda w: jnp.einsum('ghp,gk->gpkh', w, eye).reshape(N_STATE, SSM_WIDTH)
    cc = jnp.concatenate([blk_out(c_re), -blk_out(c_im)], axis=0).astype(BF16)
    a = jnp.stack([ab_re.reshape(-1), ab_im.reshape(-1)])
    n = jnp.arange(1, tc8 + 1, dtype=F32)[:, None, None]
    pmag = jnp.exp(n * (lam_re * dt)[None])
    ang = n * (lam_im * dt)[None]
    pw_re = (pmag * jnp.cos(ang)).reshape(tc8, N_STATE)
    pw_im = (pmag * jnp.sin(ang)).reshape(tc8, N_STATE)
    at = jnp.stack([pw_re[-1], pw_im[-1]])
    if reverse:
        pw_re, pw_im = pw_re[::-1], pw_im[::-1]
    apow = jnp.concatenate([jnp.repeat(pw_re, 8, axis=0), jnp.repeat(pw_im, 8, axis=0)], axis=1)
    return bb, cc, a, at, apow


def _perm_matrices(tc):
    tc8 = tc // 8
    i = np.arange(tc)
    src = (i % 8) * tc8 + i // 8
    pm = np.zeros((tc, tc), np.float32)
    pm[i, src] = 1.0
    return jnp.asarray(pm, BF16), jnp.asarray(pm.T, BF16)


def _mix_kernel(h_ref, a_ref, u_ref, yf_ref, yb_ref, d_ref, wglu_ref, bglu_ref, wout_ref,
                gffn_ref, wr_ref, br_ref, h1_ref, hn_ref, ti_ref, tg_ref):
    y = d_ref[...] * u_ref[...] + yf_ref[...] + yb_ref[...]
    y = jax.nn.gelu(y)
    z = jnp.dot(y.astype(BF16), wglu_ref[...], preferred_element_type=F32) + bglu_ref[...]
    s = y * jax.nn.sigmoid(z)
    h1 = (h_ref[...]
          + jnp.dot(a_ref[...], wout_ref[0:ATTN_WIDTH, :], preferred_element_type=F32)
          + jnp.dot(s.astype(BF16), wout_ref[ATTN_WIDTH:, :], preferred_element_type=F32))
    h1_ref[...] = h1
    hn = _rms(h1, gffn_ref[...])
    hn_ref[...] = hn
    logits = jnp.dot(hn, wr_ref[...], preferred_element_type=F32,
                     precision=lax.Precision.HIGHEST) + br_ref[...]
    lane = lax.broadcasted_iota(jnp.int32, logits.shape, 1)
    neg = jnp.float32(-jnp.inf)
    work = logits
    ti = jnp.zeros(logits.shape, jnp.int32)
    tv = jnp.full(logits.shape, neg, F32)
    for kk in range(TOP_K):
        mx = jnp.max(work, axis=-1, keepdims=True)
        idx = jnp.min(jnp.where(work == mx, lane, LANES), axis=-1, keepdims=True)
        ti = jnp.where(lane == kk, idx, ti)
        tv = jnp.where(lane == kk, mx, tv)
        work = jnp.where(lane == idx, neg, work)
    ex = jnp.exp(tv - jnp.max(tv, axis=-1, keepdims=True))
    ti_ref[...] = ti
    tg_ref[...] = ex / jnp.sum(ex, axis=-1, keepdims=True)


def _mix(h, a, u, yf, yb, d, wglu_bf, bglu, wout_bf, g_ffn, wr_pad, br_pad, tm):
    t = h.shape[0]
    row = lambda i: (i, 0)
    const = lambda i: (0, 0)
    full = lambda arr: pl.BlockSpec(arr.shape, const)
    return pl.pallas_call(
        _mix_kernel,
        grid=(t // tm,),
        in_specs=[pl.BlockSpec((tm, D_MODEL), row), pl.BlockSpec((tm, ATTN_WIDTH), row),
                  pl.BlockSpec((tm, SSM_WIDTH), row), pl.BlockSpec((tm, SSM_WIDTH), row),
                  pl.BlockSpec((tm, SSM_WIDTH), row),
                  full(d), full(wglu_bf), full(bglu), full(wout_bf), full(g_ffn), full(wr_pad),
                  full(br_pad)],
        out_specs=[pl.BlockSpec((tm, D_MODEL), row), pl.BlockSpec((tm, D_MODEL), row),
                   pl.BlockSpec((tm, LANES), row), pl.BlockSpec((tm, LANES), row)],
        out_shape=[jax.ShapeDtypeStruct((t, D_MODEL), F32), jax.ShapeDtypeStruct((t, D_MODEL), F32),
                   jax.ShapeDtypeStruct((t, LANES), jnp.int32), jax.ShapeDtypeStruct((t, LANES), F32)],
        compiler_params=_cparams(("parallel",)),
        name="mix",
    )(h, a, u, yf, yb, d, wglu_bf, bglu, wout_bf, g_ffn, wr_pad, br_pad)


def _dispatch_kernel(dest_ref, hn_ref, xs_in_ref, xs_ref, sem, *, tm):
    del xs_in_ref

    def issue(i, _):
        r = i // TOP_K
        pltpu.make_async_copy(hn_ref.at[pl.ds(r, 1), :],
                              xs_ref.at[pl.ds(dest_ref[i], 1), :], sem).start()
        return 0

    lax.fori_loop(0, tm * TOP_K, issue, 0)

    def drain(i, _):
        pltpu.make_async_copy(hn_ref.at[pl.ds(0, 1), :], xs_ref.at[pl.ds(0, 1), :], sem).wait()
        return 0

    lax.fori_loop(0, tm * TOP_K, drain, 0)


def _dispatch(dest_flat, hn, xs_init, tm):
    t = hn.shape[0]
    kernel = functools.partial(_dispatch_kernel, tm=tm)
    return pl.pallas_call(
        kernel,
        grid=(t // tm,),
        in_specs=[pl.BlockSpec((tm * TOP_K,), lambda i: (i,), memory_space=pltpu.SMEM),
                  pl.BlockSpec((tm, D_MODEL), lambda i: (i, 0)),
                  pl.BlockSpec(memory_space=pl.ANY)],
        out_specs=pl.BlockSpec(memory_space=pl.ANY),
        out_shape=jax.ShapeDtypeStruct(xs_init.shape, F32),
        scratch_shapes=[pltpu.SemaphoreType.DMA(())],
        input_output_aliases={2: 0},
        compiler_params=_cparams(("arbitrary",)),
        name="dispatch",
    )(dest_flat, hn, xs_init)


def _experts_kernel(be_ref, x_ref, wg_ref, bg_ref, wu_ref, bu_ref, wd_ref, bd_ref, y_ref):
    del be_ref
    x = x_ref[...].astype(BF16)
    gt = jnp.minimum(jnp.dot(x, wg_ref[0], preferred_element_type=F32) + bg_ref[0], SWIGLU_LIMIT)
    up = jnp.clip(jnp.dot(x, wu_ref[0], preferred_element_type=F32) + bu_ref[0],
                  -SWIGLU_LIMIT, SWIGLU_LIMIT)
    hh = (up + 1.0) * (gt * jax.nn.sigmoid(SWIGLU_ALPHA * gt))
    y_ref[...] = jnp.dot(hh.astype(BF16), wd_ref[0], preferred_element_type=F32) + bd_ref[0]


def _experts(blk_expert, xs, wg, bg, wu, bu, wd, bd, eb):
    n_pad = xs.shape[0]
    row = lambda i, be: (i, 0)
    wmap = lambda i, be: (be[i], 0, 0)
    wspec = pl.BlockSpec((1, D_MODEL, D_MODEL), wmap)
    bspec = pl.BlockSpec((1, 1, D_MODEL), wmap)
    grid_spec = pltpu.PrefetchScalarGridSpec(
        num_scalar_prefetch=1,
        grid=(n_pad // eb,),
        in_specs=[pl.BlockSpec((eb, D_MODEL), row), wspec, bspec, wspec, bspec, wspec, bspec],
        out_specs=pl.BlockSpec((eb, D_MODEL), row),
    )
    return pl.pallas_call(
        _experts_kernel,
        grid_spec=grid_spec,
        out_shape=jax.ShapeDtypeStruct((n_pad, D_MODEL), F32),
        compiler_params=_cparams(("arbitrary",)),
        name="experts",
    )(blk_expert, xs, wg, bg, wu, bu, wd, bd)


def _tail_kernel(dest_ref, h1_ref, tg_ref, p_ref, ys_ref, gple_ref, wpg_ref, wpp_ref, gfin_ref,
                 o_ref, buf_ref, sem, *, tm):
    def issue(i, _):
        r = i // TOP_K
        kk = i % TOP_K
        pltpu.make_async_copy(ys_ref.at[pl.ds(dest_ref[i], 1), :],
                              buf_ref.at[kk, pl.ds(r, 1), :], sem).start()
        return 0

    lax.fori_loop(0, tm * TOP_K, issue, 0)

    def drain(i, _):
        pltpu.make_async_copy(ys_ref.at[pl.ds(0, 1), :], buf_ref.at[0, pl.ds(0, 1), :], sem).wait()
        return 0

    lax.fori_loop(0, tm * TOP_K, drain, 0)

    tg = tg_ref[...]
    h2 = h1_ref[...]
    for kk in range(TOP_K):
        h2 = h2 + tg[:, kk:kk + 1] * buf_ref[kk]
    gate = jax.nn.sigmoid(jnp.dot(_rms(h2, gple_ref[...]).astype(BF16), wpg_ref[...],
                                  preferred_element_type=F32))
    proj = jnp.dot(p_ref[...].astype(BF16), wpp_ref[...], preferred_element_type=F32)
    o_ref[...] = _rms(h2 + gate * proj, gfin_ref[...])


def _tail(dest_flat, h1, tg, p, ys, g_ple, wpg_bf, wpp_bf, g_final, tm):
    t = h1.shape[0]
    row = lambda i: (i, 0)
    const = lambda i: (0, 0)
    full = lambda arr: pl.BlockSpec(arr.shape, const)
    kernel = functools.partial(_tail_kernel, tm=tm)
    return pl.pallas_call(
        kernel,
        grid=(t // tm,),
        in_specs=[pl.BlockSpec((tm * TOP_K,), lambda i: (i,), memory_space=pltpu.SMEM),
                  pl.BlockSpec((tm, D_MODEL), row), pl.BlockSpec((tm, LANES), row),
                  pl.BlockSpec((tm, PLE_DIM), row),
                  pl.BlockSpec(memory_space=pl.ANY),
                  full(g_ple), full(wpg_bf), full(wpp_bf), full(g_final)],
        out_specs=pl.BlockSpec((tm, D_MODEL), row),
        out_shape=jax.ShapeDtypeStruct((t, D_MODEL), F32),
        scratch_shapes=[pltpu.VMEM((TOP_K, tm, D_MODEL), F32), pltpu.SemaphoreType.DMA(())],
        compiler_params=_cparams(("arbitrary",)),
        name="tail",
    )(dest_flat, h1, tg, p, ys, g_ple, wpg_bf, wpp_bf, g_final)


def _routing(top_i, eb):
    t = top_i.shape[0]
    onehot = (top_i[:, :, None] == jnp.arange(N_EXPERTS, dtype=jnp.int32)).astype(jnp.int32)
    per_tok = onehot.sum(axis=1)
    incl = jnp.cumsum(per_tok, axis=0)
    counts = incl[-1]
    excl = incl - per_tok
    padded = ((counts + eb - 1) // eb) * eb
    pends = jnp.cumsum(padded)
    pstarts = pends - padded
    rank = jnp.take_along_axis(excl + pstarts[None, :], top_i, axis=1)
    n_blocks = -(-(t * TOP_K + N_EXPERTS * (eb - 1)) // eb)
    blk_start = jnp.arange(n_blocks, dtype=jnp.int32) * eb
    blk_expert = jnp.minimum(jnp.searchsorted(pends, blk_start, side='right'), N_EXPERTS - 1)
    return rank.reshape(-1).astype(jnp.int32), blk_expert.astype(jnp.int32), n_blocks * eb


def _pick(n, prefs):
    for c in prefs:
        if n % c == 0:
            return c
    raise ValueError(f"no tile for {n}")


def _trunk(x, p, w, lambda_init):
    b, l, _ = x.shape
    t = b * l
    tm = _pick(t, (512, 256, 128))
    tq = _pick(l, (512, 256, 128))
    tc = _pick(l, (256, 128))
    eb = 256

    q, k, v, u = _inproj(x.reshape(t, D_MODEL), w['g_mix'], w['w_in'], tm)
    a = _attention(q.reshape(b, l, -1), k.reshape(b, l, -1), v.reshape(b, l, -1), w['attn_scal'],
                   w['g_subln'], lambda_init, tq, tq)
    u3 = u.reshape(b, l, SSM_WIDTH)
    pm, pmt = _perm_matrices(tc)
    ys = [_s5_direction(u3, pm, pmt, *w['s5'][dr], tc, reverse=(dr == 1)) for dr in range(2)]
    h1, hn, ti, tg = _mix(x.reshape(t, D_MODEL), a.reshape(t, ATTN_WIDTH), u,
                          ys[0].reshape(t, SSM_WIDTH), ys[1].reshape(t, SSM_WIDTH),
                          w['ssm_d'], w['w_glu'], w['b_glu'], w['w_out'], w['g_ffn'],
                          w['w_router'], w['b_router'], tm)
    dest, blk_expert, n_pad = _routing(ti[:, :TOP_K], eb)
    xs = _dispatch(dest, hn, jnp.zeros((n_pad, D_MODEL), F32), tm // 2)
    ye = _experts(blk_expert, xs, w['w_gate'], w['b_gate'], w['w_up'], w['b_up'],
                  w['w_down'], w['b_down'], eb)
    out = _tail(dest, h1, tg, p.reshape(t, PLE_DIM), ye, w['g_ple'], w['w_ple_gate'],
                w['w_ple_proj'], w['g_final'], tm // 2)
    return out.reshape(b, l, D_MODEL)


def _prepare(i, g_mix, w_in, lambda_q1, lambda_k1, lambda_q2, lambda_k2, g_subln, ssm_lambda_re,
             ssm_lambda_im, ssm_log_dt, ssm_b_re, ssm_b_im, ssm_c_re, ssm_c_im, ssm_d, w_glu, b_glu,
             w_out, g_ffn, w_router, b_router, w_gate, b_gate, w_up, b_up, w_down, b_down, g_ple,
             w_ple_gate, w_ple_proj, g_final, tcs):
    lambda_init = 0.8 - 0.6 * math.exp(-0.3 * i)
    lam = (jnp.exp(jnp.sum(lambda_q1[i] * lambda_k1[i]))
           - jnp.exp(jnp.sum(lambda_q2[i] * lambda_k2[i])) + lambda_init)
    slopes = jnp.exp2(-8.0 * jnp.arange(1, N_HEADS + 1, dtype=F32) / N_HEADS)
    row = lambda vec: vec.reshape(1, -1).astype(F32)
    w = {
        'g_mix': g_mix[i], 'w_in': w_in[i].astype(BF16),
        'attn_scal': jnp.concatenate([lam.reshape(1), slopes]).astype(F32),
        'g_subln': g_subln[i],
        'ssm_d': row(ssm_d[i]), 'w_glu': w_glu[i].astype(BF16), 'b_glu': row(b_glu[i]),
        'w_out': w_out[i].astype(BF16), 'g_ffn': row(g_ffn[i]),
        'w_router': jnp.pad(w_router[i], ((0, 0), (0, LANES - N_EXPERTS))),
        'b_router': jnp.pad(row(b_router[i]), ((0, 0), (0, LANES - N_EXPERTS)),
                            constant_values=-jnp.inf),
        'w_gate': w_gate[i].astype(BF16), 'b_gate': b_gate[i].reshape(N_EXPERTS, 1, D_MODEL),
        'w_up': w_up[i].astype(BF16), 'b_up': b_up[i].reshape(N_EXPERTS, 1, D_MODEL),
        'w_down': w_down[i].astype(BF16), 'b_down': b_down[i].reshape(N_EXPERTS, 1, D_MODEL),
        'g_ple': row(g_ple[i]), 'w_ple_gate': w_ple_gate[i].astype(BF16),
        'w_ple_proj': w_ple_proj[i].astype(BF16), 'g_final': row(g_final),
    }
    w['s5'] = {tc: [_s5_params(ssm_lambda_re[i, dr], ssm_lambda_im[i, dr], ssm_log_dt[i, dr],
                               ssm_b_re[i, dr], ssm_b_im[i, dr], ssm_c_re[i, dr], ssm_c_im[i, dr],
                               tc, reverse=(dr == 1)) for dr in range(2)] for tc in tcs}
    return w, lambda_init


def kernel(x_prompt, x_sample, p_prompt, p_sample, g_mix, w_in, lambda_q1, lambda_k1, lambda_q2, lambda_k2, g_subln, ssm_lambda_re, ssm_lambda_im, ssm_log_dt, ssm_b_re, ssm_b_im, ssm_c_re, ssm_c_im, ssm_d, w_glu, b_glu, w_out, g_ffn, w_router, b_router, w_gate, b_gate, w_up, b_up, w_down, b_down, g_ple, w_ple_gate, w_ple_proj, g_final):
    assert w_in.shape[0] == 1, "single-layer trunk"
    tcs = {_pick(x.shape[1], (256, 128)) for x in (x_prompt, x_sample)}
    w, lambda_init = _prepare(0, g_mix, w_in, lambda_q1, lambda_k1, lambda_q2, lambda_k2, g_subln,
                              ssm_lambda_re, ssm_lambda_im, ssm_log_dt, ssm_b_re, ssm_b_im, ssm_c_re,
                              ssm_c_im, ssm_d, w_glu, b_glu, w_out, g_ffn, w_router, b_router, w_gate,
                              b_gate, w_up, b_up, w_down, b_down, g_ple, w_ple_gate, w_ple_proj,
                              g_final, tcs)
    outs = []
    for x, p in ((x_prompt, p_prompt), (x_sample, p_sample)):
        tc = _pick(x.shape[1], (256, 128))
        wt = dict(w, s5=w['s5'][tc])
        outs.append(_trunk(x, p[0], wt, lambda_init))
    return tuple(outs)
```

```python
import functools
import math

import jax
import jax.numpy as jnp
import numpy as np
from jax import lax
from jax.experimental import pallas as pl
from jax.experimental.pallas import tpu as pltpu

F32 = jnp.float32
BF16 = jnp.bfloat16

D_MODEL = 1024
PLE_DIM = 256
ATTN_WIDTH = 512
SSM_WIDTH = 512
HEAD_W = 128
DIFF_HEAD_DIM = 64
N_HEADS = 4
SSM_GROUP = 16
N_GROUPS = 32
SSM_STATE = 64
N_STATE = N_GROUPS * SSM_STATE
N_EXPERTS = 32
TOP_K = 4
SWIGLU_LIMIT = 7.0
SWIGLU_ALPHA = 1.702
EPS = 1e-6
LANES = 128
TOKEN_TILE = 512

VMEM_LIMIT = 56 * 1024 * 1024


def _cparams(sem):
    return pltpu.CompilerParams(dimension_semantics=sem, vmem_limit_bytes=VMEM_LIMIT)


def _rms(x, g):
    return x * lax.rsqrt(jnp.mean(x * x, axis=-1, keepdims=True) + EPS) * g


LOG2E = 1.4426950408889634
NT_DIMS = (((1,), (1,)), ((), ()))


def _inproj_kernel(x_ref, g_ref, wqku_ref, wvt_ref, q_ref, k_ref, vt_ref, u_ref):
    xn = _rms(x_ref[...], g_ref[...]).astype(BF16)
    proj = jnp.dot(xn, wqku_ref[...], preferred_element_type=F32)
    scale = LOG2E / math.sqrt(DIFF_HEAD_DIM)
    q_ref[...] = (proj[:, :ATTN_WIDTH] * scale).astype(BF16)
    k_ref[...] = proj[:, ATTN_WIDTH:2 * ATTN_WIDTH].astype(BF16)
    u_ref[...] = proj[:, 2 * ATTN_WIDTH:]
    vt_ref[0] = lax.dot_general(wvt_ref[...], xn, NT_DIMS, preferred_element_type=F32).astype(BF16)


def _inproj(x, g_mix, w_qku_bf, w_vt_bf, tm):
    t = x.shape[0]
    row = lambda i: (i, 0)
    const = lambda i: (0, 0)
    return pl.pallas_call(
        _inproj_kernel,
        grid=(t // tm,),
        in_specs=[pl.BlockSpec((tm, D_MODEL), row),
                  pl.BlockSpec((1, D_MODEL), const),
                  pl.BlockSpec(w_qku_bf.shape, const),
                  pl.BlockSpec(w_vt_bf.shape, const)],
        out_specs=[pl.BlockSpec((tm, ATTN_WIDTH), row), pl.BlockSpec((tm, ATTN_WIDTH), row),
                   pl.BlockSpec((1, ATTN_WIDTH, tm), lambda i: (i, 0, 0)),
                   pl.BlockSpec((tm, SSM_WIDTH), row)],
        out_shape=[jax.ShapeDtypeStruct((t, ATTN_WIDTH), BF16), jax.ShapeDtypeStruct((t, ATTN_WIDTH), BF16),
                   jax.ShapeDtypeStruct((t // tm, ATTN_WIDTH, tm), BF16),
                   jax.ShapeDtypeStruct((t, SSM_WIDTH), F32)],
        compiler_params=_cparams(("parallel",)),
        name="inproj",
    )(x, g_mix.reshape(1, D_MODEL), w_qku_bf, w_vt_bf)


ONES_ROWS = 16
ACC_ROWS = HEAD_W + ONES_ROWS
POS_SPLIT = 16
N_BIAS_PARTS = 3


def _attn_kernel(sc_ref, q_ref, k_ref, vt_ref, qf_ref, kf_ref, g_ref, o_ref,
                 qa1_ref, qa2_ref, m1_ref, m2_ref, acc1_ref, acc2_ref, *, ts, out_scale):
    h = pl.program_id(1)
    qi = pl.program_id(2)
    nsub = k_ref.shape[1] // ts
    bslope = sc_ref[1 + h]

    q = q_ref[0]
    lane = lax.broadcasted_iota(jnp.int32, q.shape, 1)
    zero = jnp.zeros_like(q)
    qf = qf_ref[0]
    qa1_ref[...] = jnp.concatenate([jnp.where(lane < DIFF_HEAD_DIM, q, zero), qf], axis=1)
    qa2_ref[...] = jnp.concatenate([jnp.where(lane >= DIFF_HEAD_DIM, q, zero), qf], axis=1)
    m1_ref[...] = jnp.full_like(m1_ref, -jnp.inf)
    m2_ref[...] = jnp.full_like(m2_ref, -jnp.inf)
    acc1_ref[...] = jnp.zeros_like(acc1_ref)
    acc2_ref[...] = jnp.zeros_like(acc2_ref)
    ones = jnp.ones((ONES_ROWS, ts), BF16)

    def sub_tile(j, diag):
        rows = pl.ds(pl.multiple_of(j * ts, ts), ts)
        if diag:
            kf = kf_ref[0, 2]
            c = jnp.float32(0.0)
            kpos = lax.broadcasted_iota(jnp.int32, (ts, ts), 0)
            qpos = lax.broadcasted_iota(jnp.int32, (ts, ts), 1)
            bias = jnp.abs(qpos - kpos).astype(F32) * (-bslope)
        else:
            kf = kf_ref[0, jnp.where(j < qi, 0, 1)]
            c = -bslope * (jnp.abs(qi - j) * ts).astype(F32)
        ka = jnp.concatenate([k_ref[0, rows, :], kf], axis=1)
        vta = jnp.concatenate([vt_ref[j], ones], axis=0)
        for qa_ref, m_ref, acc_ref in ((qa1_ref, m1_ref, acc1_ref), (qa2_ref, m2_ref, acc2_ref)):
            s = lax.dot_general(ka, qa_ref[...], NT_DIMS, preferred_element_type=F32)
            if diag:
                s = s + bias
            m_old = m_ref[...]
            m_new = jnp.maximum(m_old, jnp.max(s, axis=0, keepdims=True) + c)
            alpha = jnp.exp2(m_old - m_new)
            p = jnp.exp2(s - (m_new - c)).astype(BF16)
            acc_ref[...] = alpha * acc_ref[...] + jnp.dot(vta, p, preferred_element_type=F32)
            m_ref[...] = m_new

    def body(j, carry):
        @pl.when(j != qi)
        def _():
            sub_tile(j, False)

        @pl.when(j == qi)
        def _():
            sub_tile(j, True)

        return carry

    lax.fori_loop(0, nsub, body, 0)

    lam = sc_ref[0]
    a1 = acc1_ref[...]
    a2 = acc2_ref[...]
    out_t = (a1[:HEAD_W] / a1[HEAD_W:HEAD_W + 1] - lam * (a2[:HEAD_W] / a2[HEAD_W:HEAD_W + 1]))
    ms = jnp.mean(out_t * out_t, axis=0, keepdims=True)
    out_t = out_t * lax.rsqrt(ms + EPS) * (g_ref[...] * out_scale)
    o_ref[0] = out_t.T.astype(o_ref.dtype)


def _attn_features(bparts, ts):
    pos = jnp.arange(ts, dtype=jnp.int32)
    hi = ((pos // POS_SPLIT) * POS_SPLIT).astype(F32)[None, :, None]
    lo = (pos % POS_SPLIT).astype(F32)[None, :, None]
    bp = bparts[:, None, :]
    one = jnp.ones_like(bp)
    qf = jnp.stack([bp + 0 * hi, bp + 0 * hi, hi * one, lo * one], axis=-1)
    kf = jnp.stack([hi * one, lo * one, -bp + 0 * hi, -bp + 0 * hi], axis=-1)
    pad = lambda f: jnp.pad(f.reshape(N_HEADS, ts, 4 * N_BIAS_PARTS),
                            ((0, 0), (0, 0), (0, HEAD_W - 4 * N_BIAS_PARTS)))
    qf, kf = pad(qf), pad(kf)
    kf = jnp.stack([kf, -kf, jnp.zeros_like(kf)], axis=1)
    return qf.astype(BF16), kf.astype(BF16)


def _attention(q, k, vt, scal, qf, kf, g_subln, lambda_init, ts):
    b, l, _ = q.shape
    nsub = l // ts
    kernel = functools.partial(_attn_kernel, ts=ts, out_scale=1.0 - lambda_init)
    return pl.pallas_call(
        kernel,
        grid=(b, N_HEADS, nsub),
        in_specs=[pl.BlockSpec(memory_space=pltpu.SMEM),
                  pl.BlockSpec((1, ts, HEAD_W), lambda bi, h, qi: (bi, qi, h)),
                  pl.BlockSpec((1, l, HEAD_W), lambda bi, h, qi: (bi, 0, h)),
                  pl.BlockSpec((nsub, HEAD_W, ts), lambda bi, h, qi: (bi, h, 0)),
                  pl.BlockSpec((1, ts, HEAD_W), lambda bi, h, qi: (h, 0, 0)),
                  pl.BlockSpec((1, 3, ts, HEAD_W), lambda bi, h, qi: (h, 0, 0, 0)),
                  pl.BlockSpec((HEAD_W, 1), lambda bi, h, qi: (0, 0))],
        out_specs=pl.BlockSpec((1, ts, HEAD_W), lambda bi, h, qi: (bi, qi, h)),
        out_shape=jax.ShapeDtypeStruct((b, l, ATTN_WIDTH), BF16),
        scratch_shapes=[pltpu.VMEM((ts, 2 * HEAD_W), BF16), pltpu.VMEM((ts, 2 * HEAD_W), BF16),
                        pltpu.VMEM((1, ts), F32), pltpu.VMEM((1, ts), F32),
                        pltpu.VMEM((ACC_ROWS, ts), F32), pltpu.VMEM((ACC_ROWS, ts), F32)],
        compiler_params=_cparams(("parallel", "parallel", "arbitrary")),
        name="attn",
    )(scal, q, k, vt, qf, kf, g_subln.reshape(HEAD_W, 1))


SCAN_LANES = 512


def _s5_kernel(u_ref, pm_ref, pmt_ref, bb_ref, cc_ref, a_ref, at_ref, apow_ref, y_ref,
               xs_ref, xb_ref, carry_ref, *, tc, reverse):
    tc8 = tc // 8
    c = pl.program_id(1)

    @pl.when(c == 0)
    def _():
        carry_ref[...] = jnp.zeros_like(carry_ref)

    up = jnp.dot(pm_ref[...], u_ref[0].astype(BF16), preferred_element_type=F32).astype(BF16)
    xs_ref[...] = jnp.dot(up, bb_ref[...], preferred_element_type=F32)

    for cb in range(N_STATE // SCAN_LANES):
        lo = cb * SCAN_LANES
        re_sl = pl.ds(lo, SCAN_LANES)
        im_sl = pl.ds(N_STATE + lo, SCAN_LANES)
        ar = jnp.broadcast_to(a_ref[0:1, lo:lo + SCAN_LANES], (8, SCAN_LANES))
        ai = jnp.broadcast_to(a_ref[1:2, lo:lo + SCAN_LANES], (8, SCAN_LANES))

        def step(i, carry):
            xr, xi = carry
            r = (tc8 - 1 - i) if reverse else i
            rows = pl.ds(pl.multiple_of(r * 8, 8), 8)
            nr = ar * xr - ai * xi + xs_ref[rows, re_sl]
            ni = ar * xi + ai * xr + xs_ref[rows, im_sl]
            xs_ref[rows, re_sl] = nr
            xs_ref[rows, im_sl] = ni
            return nr, ni

        z = jnp.zeros((8, SCAN_LANES), F32)
        lax.fori_loop(0, tc8, step, (z, z), unroll=4)

    last = 0 if reverse else (tc8 - 1) * 8
    e_re = xs_ref[last:last + 8, 0:N_STATE]
    e_im = xs_ref[last:last + 8, N_STATE:2 * N_STATE]
    atr = at_ref[0:1, :]
    ati = at_ref[1:2, :]
    cr = carry_ref[0:1, :]
    ci = carry_ref[1:2, :]
    cin_r = [None] * 8
    cin_i = [None] * 8
    order = range(7, -1, -1) if reverse else range(8)
    for j in order:
        cin_r[j] = cr
        cin_i[j] = ci
        nr = atr * cr - ati * ci + e_re[j:j + 1]
        ni = atr * ci + ati * cr + e_im[j:j + 1]
        cr, ci = nr, ni
    carry_ref[0:1, :] = cr
    carry_ref[1:2, :] = ci
    cin_r = jnp.concatenate(cin_r + cin_r, axis=0)
    cin_i = jnp.concatenate(cin_i + cin_i, axis=0)

    def fix(i, _):
        rows = pl.ds(pl.multiple_of(i * 16, 16), 16)
        pr = apow_ref[rows, 0:N_STATE]
        pi = apow_ref[rows, N_STATE:2 * N_STATE]
        xr = xs_ref[rows, 0:N_STATE] + pr * cin_r - pi * cin_i
        xi = xs_ref[rows, N_STATE:2 * N_STATE] + pr * cin_i + pi * cin_r
        xb_ref[rows, 0:N_STATE] = xr.astype(BF16)
        xb_ref[rows, N_STATE:2 * N_STATE] = xi.astype(BF16)
        return 0

    lax.fori_loop(0, tc // 16, fix, 0)

    yp = jnp.dot(xb_ref[...], cc_ref[...], preferred_element_type=F32)
    hi = yp.astype(BF16)
    lo_part = (yp - hi.astype(F32)).astype(BF16)
    y_ref[0] = (jnp.dot(pmt_ref[...], hi, preferred_element_type=F32)
                + jnp.dot(pmt_ref[...], lo_part, preferred_element_type=F32))


def _s5_direction(u, pm, pmt, bb, cc, a, at, apow, tc, reverse):
    b, l, _ = u.shape
    nc = l // tc
    cmap = (lambda bi, c: (bi, nc - 1 - c, 0)) if reverse else (lambda bi, c: (bi, c, 0))
    const = lambda bi, c: (0, 0)
    kernel = functools.partial(_s5_kernel, tc=tc, reverse=reverse)
    return pl.pallas_call(
        kernel,
        grid=(b, nc),
        in_specs=[pl.BlockSpec((1, tc, SSM_WIDTH), cmap),
                  pl.BlockSpec(pm.shape, const), pl.BlockSpec(pmt.shape, const),
                  pl.BlockSpec(bb.shape, const), pl.BlockSpec(cc.shape, const),
                  pl.BlockSpec(a.shape, const), pl.BlockSpec(at.shape, const),
                  pl.BlockSpec(apow.shape, const)],
        out_specs=pl.BlockSpec((1, tc, SSM_WIDTH), cmap),
        out_shape=jax.ShapeDtypeStruct((b, l, SSM_WIDTH), F32),
        scratch_shapes=[pltpu.VMEM((tc, 2 * N_STATE), F32), pltpu.VMEM((tc, 2 * N_STATE), BF16),
                        pltpu.VMEM((2, N_STATE), F32)],
        compiler_params=_cparams(("parallel", "arbitrary")),
        name="s5_bwd" if reverse else "s5_fwd",
    )(u, pm, pmt, bb, cc, a, at, apow)


def _s5_params(lam_re, lam_im, log_dt, b_re, b_im, c_re, c_im, tc, reverse):
    tc8 = tc // 8
    dt = jnp.exp(log_dt)[:, None]
    mag = jnp.exp(lam_re * dt)
    ab_re = mag * jnp.cos(lam_im * dt)
    ab_im = mag * jnp.sin(lam_im * dt)
    den = lam_re * lam_re + lam_im * lam_im
    nr, ni = ab_re - 1.0, ab_im
    co_re = (nr * lam_re + ni * lam_im) / den
    co_im = (ni * lam_re - nr * lam_im) / den
    bb_re = co_re[..., None] * b_re - co_im[..., None] * b_im
    bb_im = co_re[..., None] * b_im + co_im[..., None] * b_re
    eye = jnp.eye(N_GROUPS, dtype=F32)
    blk_in = lambda w: jnp.einsum('gph,gk->ghkp', w, eye).reshape(SSM_WIDTH, N_STATE)
    bb = jnp.concatenate([blk_in(bb_re), blk_in(bb_im)], axis=1).astype(BF16)
    blk_out = lambda w: jnp.einsum('ghp,gk->gpkh', w, eye).reshape(N_STATE, SSM_WIDTH)
    cc = jnp.concatenate([blk_out(c_re), -blk_out(c_im)], axis=0).astype(BF16)
    a = jnp.stack([ab_re.reshape(-1), ab_im.reshape(-1)])
    n = jnp.arange(1, tc8 + 1, dtype=F32)[:, None, None]
    pmag = jnp.exp(n * (lam_re * dt)[None])
    ang = n * (lam_im * dt)[None]
    pw_re = (pmag * jnp.cos(ang)).reshape(tc8, N_STATE)
    pw_im = (pmag * jnp.sin(ang)).reshape(tc8, N_STATE)
    at = jnp.stack([pw_re[-1], pw_im[-1]])
    if reverse:
        pw_re, pw_im = pw_re[::-1], pw_im[::-1]
    apow = jnp.concatenate([jnp.repeat(pw_re, 8, axis=0), jnp.repeat(pw_im, 8, axis=0)], axis=1)
    return bb, cc, a, at, apow


def _perm_matrices(tc):
    tc8 = tc // 8
    i = np.arange(tc)
    src = (i % 8) * tc8 + i // 8
    pm = np.zeros((tc, tc), np.float32)
    pm[i, src] = 1.0
    return jnp.asarray(pm, BF16), jnp.asarray(pm.T, BF16)


def _mix_kernel(h_ref, a_ref, u_ref, yf_ref, yb_ref, d_ref, wglu_ref, bglu_ref, wout_ref,
                gffn_ref, wr_ref, br_ref, h1_ref, hn_ref, ti_ref, tg_ref):
    y = d_ref[...] * u_ref[...] + yf_ref[...] + yb_ref[...]
    y = jax.nn.gelu(y)
    z = jnp.dot(y.astype(BF16), wglu_ref[...], preferred_element_type=F32) + bglu_ref[...]
    s = y * jax.nn.sigmoid(z)
    h1 = (h_ref[...]
          + jnp.dot(a_ref[...], wout_ref[0:ATTN_WIDTH, :], preferred_element_type=F32)
          + jnp.dot(s.astype(BF16), wout_ref[ATTN_WIDTH:, :], preferred_element_type=F32))
    h1_ref[...] = h1
    hn = _rms(h1, gffn_ref[...])
    hn_ref[...] = hn
    logits = jnp.dot(hn, wr_ref[...], preferred_element_type=F32,
                     precision=lax.Precision.HIGHEST) + br_ref[...]
    lane = lax.broadcasted_iota(jnp.int32, logits.shape, 1)
    neg = jnp.float32(-jnp.inf)
    work = logits
    ti = jnp.zeros(logits.shape, jnp.int32)
    tv = jnp.full(logits.shape, neg, F32)
    for kk in range(TOP_K):
        mx = jnp.max(work, axis=-1, keepdims=True)
        idx = jnp.min(jnp.where(work == mx, lane, LANES), axis=-1, keepdims=True)
        ti = jnp.where(lane == kk, idx, ti)
        tv = jnp.where(lane == kk, mx, tv)
        work = jnp.where(lane == idx, neg, work)
    ex = jnp.exp(tv - jnp.max(tv, axis=-1, keepdims=True))
    ti_ref[...] = ti
    tg_ref[...] = ex / jnp.sum(ex, axis=-1, keepdims=True)


def _mix(h, a, u, yf, yb, d, wglu_bf, bglu, wout_bf, g_ffn, wr_pad, br_pad, tm):
    t = h.shape[0]
    row = lambda i: (i, 0)
    const = lambda i: (0, 0)
    full = lambda arr: pl.BlockSpec(arr.shape, const)
    return pl.pallas_call(
        _mix_kernel,
        grid=(t // tm,),
        in_specs=[pl.BlockSpec((tm, D_MODEL), row), pl.BlockSpec((tm, ATTN_WIDTH), row),
                  pl.BlockSpec((tm, SSM_WIDTH), row), pl.BlockSpec((tm, SSM_WIDTH), row),
                  pl.BlockSpec((tm, SSM_WIDTH), row),
                  full(d), full(wglu_bf), full(bglu), full(wout_bf), full(g_ffn), full(wr_pad),
                  full(br_pad)],
        out_specs=[pl.BlockSpec((tm, D_MODEL), row), pl.BlockSpec((tm, D_MODEL), row),
                   pl.BlockSpec((tm, LANES), row), pl.BlockSpec((tm, LANES), row)],
        out_shape=[jax.ShapeDtypeStruct((t, D_MODEL), F32), jax.ShapeDtypeStruct((t, D_MODEL), F32),
                   jax.ShapeDtypeStruct((t, LANES), jnp.int32), jax.ShapeDtypeStruct((t, LANES), F32)],
        compiler_params=_cparams(("parallel",)),
        name="mix",
    )(h, a, u, yf, yb, d, wglu_bf, bglu, wout_bf, g_ffn, wr_pad, br_pad)


def _dispatch_kernel(dest_ref, hn_ref, xs_in_ref, xs_ref, sem, *, tm):
    del xs_in_ref

    def issue(i, _):
        r = i // TOP_K
        pltpu.make_async_copy(hn_ref.at[pl.ds(r, 1), :],
                              xs_ref.at[pl.ds(dest_ref[i], 1), :], sem).start()
        return 0

    lax.fori_loop(0, tm * TOP_K, issue, 0)

    def drain(i, _):
        pltpu.make_async_copy(hn_ref.at[pl.ds(0, 1), :], xs_ref.at[pl.ds(0, 1), :], sem).wait()
        return 0

    lax.fori_loop(0, tm * TOP_K, drain, 0)


def _dispatch(dest_flat, hn, xs_init, tm):
    t = hn.shape[0]
    kernel = functools.partial(_dispatch_kernel, tm=tm)
    return pl.pallas_call(
        kernel,
        grid=(t // tm,),
        in_specs=[pl.BlockSpec((tm * TOP_K,), lambda i: (i,), memory_space=pltpu.SMEM),
                  pl.BlockSpec((tm, D_MODEL), lambda i: (i, 0)),
                  pl.BlockSpec(memory_space=pl.ANY)],
        out_specs=pl.BlockSpec(memory_space=pl.ANY),
        out_shape=jax.ShapeDtypeStruct(xs_init.shape, F32),
        scratch_shapes=[pltpu.SemaphoreType.DMA(())],
        input_output_aliases={2: 0},
        compiler_params=_cparams(("arbitrary",)),
        name="dispatch",
    )(dest_flat, hn, xs_init)


def _experts_kernel(be_ref, x_ref, wg_ref, bg_ref, wu_ref, bu_ref, wd_ref, bd_ref, y_ref):
    del be_ref
    x = x_ref[...].astype(BF16)
    gt = jnp.minimum(jnp.dot(x, wg_ref[0], preferred_element_type=F32) + bg_ref[0], SWIGLU_LIMIT)
    up = jnp.clip(jnp.dot(x, wu_ref[0], preferred_element_type=F32) + bu_ref[0],
                  -SWIGLU_LIMIT, SWIGLU_LIMIT)
    hh = (up + 1.0) * (gt * jax.nn.sigmoid(SWIGLU_ALPHA * gt))
    y_ref[...] = jnp.dot(hh.astype(BF16), wd_ref[0], preferred_element_type=F32) + bd_ref[0]


def _experts(blk_expert, xs, wg, bg, wu, bu, wd, bd, eb):
    n_pad = xs.shape[0]
    row = lambda i, be: (i, 0)
    wmap = lambda i, be: (be[i], 0, 0)
    wspec = pl.BlockSpec((1, D_MODEL, D_MODEL), wmap)
    bspec = pl.BlockSpec((1, 1, D_MODEL), wmap)
    grid_spec = pltpu.PrefetchScalarGridSpec(
        num_scalar_prefetch=1,
        grid=(n_pad // eb,),
        in_specs=[pl.BlockSpec((eb, D_MODEL), row), wspec, bspec, wspec, bspec, wspec, bspec],
        out_specs=pl.BlockSpec((eb, D_MODEL), row),
    )
    return pl.pallas_call(
        _experts_kernel,
        grid_spec=grid_spec,
        out_shape=jax.ShapeDtypeStruct((n_pad, D_MODEL), F32),
        compiler_params=_cparams(("arbitrary",)),
        name="experts",
    )(blk_expert, xs, wg, bg, wu, bu, wd, bd)


def _tail_kernel(dest_ref, h1_ref, tg_ref, p_ref, ys_ref, gple_ref, wpg_ref, wpp_ref, gfin_ref,
                 o_ref, buf_ref, sem, *, tm):
    def issue(i, _):
        r = i // TOP_K
        kk = i % TOP_K
        pltpu.make_async_copy(ys_ref.at[pl.ds(dest_ref[i], 1), :],
                              buf_ref.at[kk, pl.ds(r, 1), :], sem).start()
        return 0

    lax.fori_loop(0, tm * TOP_K, issue, 0)

    def drain(i, _):
        pltpu.make_async_copy(ys_ref.at[pl.ds(0, 1), :], buf_ref.at[0, pl.ds(0, 1), :], sem).wait()
        return 0

    lax.fori_loop(0, tm * TOP_K, drain, 0)

    tg = tg_ref[...]
    h2 = h1_ref[...]
    for kk in range(TOP_K):
        h2 = h2 + tg[:, kk:kk + 1] * buf_ref[kk]
    gate = jax.nn.sigmoid(jnp.dot(_rms(h2, gple_ref[...]).astype(BF16), wpg_ref[...],
                                  preferred_element_type=F32))
    proj = jnp.dot(p_ref[...].astype(BF16), wpp_ref[...], preferred_element_type=F32)
    o_ref[...] = _rms(h2 + gate * proj, gfin_ref[...])


def _tail(dest_flat, h1, tg, p, ys, g_ple, wpg_bf, wpp_bf, g_final, tm):
    t = h1.shape[0]
    row = lambda i: (i, 0)
    const = lambda i: (0, 0)
    full = lambda arr: pl.BlockSpec(arr.shape, const)
    kernel = functools.partial(_tail_kernel, tm=tm)
    return pl.pallas_call(
        kernel,
        grid=(t // tm,),
        in_specs=[pl.BlockSpec((tm * TOP_K,), lambda i: (i,), memory_space=pltpu.SMEM),
                  pl.BlockSpec((tm, D_MODEL), row), pl.BlockSpec((tm, LANES), row),
                  pl.BlockSpec((tm, PLE_DIM), row),
                  pl.BlockSpec(memory_space=pl.ANY),
                  full(g_ple), full(wpg_bf), full(wpp_bf), full(g_final)],
        out_specs=pl.BlockSpec((tm, D_MODEL), row),
        out_shape=jax.ShapeDtypeStruct((t, D_MODEL), F32),
        scratch_shapes=[pltpu.VMEM((TOP_K, tm, D_MODEL), F32), pltpu.SemaphoreType.DMA(())],
        compiler_params=_cparams(("arbitrary",)),
        name="tail",
    )(dest_flat, h1, tg, p, ys, g_ple, wpg_bf, wpp_bf, g_final)


def _routing(top_i, eb):
    t = top_i.shape[0]
    onehot = (top_i[:, :, None] == jnp.arange(N_EXPERTS, dtype=jnp.int32)).astype(jnp.int32)
    per_tok = onehot.sum(axis=1)
    incl = jnp.cumsum(per_tok, axis=0)
    counts = incl[-1]
    excl = incl - per_tok
    padded = ((counts + eb - 1) // eb) * eb
    pends = jnp.cumsum(padded)
    pstarts = pends - padded
    rank = jnp.take_along_axis(excl + pstarts[None, :], top_i, axis=1)
    n_blocks = -(-(t * TOP_K + N_EXPERTS * (eb - 1)) // eb)
    blk_start = jnp.arange(n_blocks, dtype=jnp.int32) * eb
    blk_expert = jnp.minimum(jnp.searchsorted(pends, blk_start, side='right'), N_EXPERTS - 1)
    return rank.reshape(-1).astype(jnp.int32), blk_expert.astype(jnp.int32), n_blocks * eb


def _pick(n, prefs):
    for c in prefs:
        if n % c == 0:
            return c
    raise ValueError(f"no tile for {n}")


def _trunk(x, p, w, lambda_init):
    b, l, _ = x.shape
    t = b * l
    tm = TOKEN_TILE
    assert l % tm == 0
    tc = _pick(l, (256, 128))
    eb = 256

    q, k, vt, u = _inproj(x.reshape(t, D_MODEL), w['g_mix'], w['w_qku'], w['w_vt'], tm)
    a = _attention(q.reshape(b, l, -1), k.reshape(b, l, -1), vt, w['attn_scal'], w['attn_qf'],
                   w['attn_kf'], w['g_subln'], lambda_init, tm)
    u3 = u.reshape(b, l, SSM_WIDTH)
    pm, pmt = _perm_matrices(tc)
    ys = [_s5_direction(u3, pm, pmt, *w['s5'][dr], tc, reverse=(dr == 1)) for dr in range(2)]
    h1, hn, ti, tg = _mix(x.reshape(t, D_MODEL), a.reshape(t, ATTN_WIDTH), u,
                          ys[0].reshape(t, SSM_WIDTH), ys[1].reshape(t, SSM_WIDTH),
                          w['ssm_d'], w['w_glu'], w['b_glu'], w['w_out'], w['g_ffn'],
                          w['w_router'], w['b_router'], tm)
    dest, blk_expert, n_pad = _routing(ti[:, :TOP_K], eb)
    xs = _dispatch(dest, hn, jnp.zeros((n_pad, D_MODEL), F32), tm // 2)
    ye = _experts(blk_expert, xs, w['w_gate'], w['b_gate'], w['w_up'], w['b_up'],
                  w['w_down'], w['b_down'], eb)
    out = _tail(dest, h1, tg, p.reshape(t, PLE_DIM), ye, w['g_ple'], w['w_ple_gate'],
                w['w_ple_proj'], w['g_final'], tm // 2)
    return out.reshape(b, l, D_MODEL)


def _prepare(i, g_mix, w_in, lambda_q1, lambda_k1, lambda_q2, lambda_k2, g_subln, ssm_lambda_re,
             ssm_lambda_im, ssm_log_dt, ssm_b_re, ssm_b_im, ssm_c_re, ssm_c_im, ssm_d, w_glu, b_glu,
             w_out, g_ffn, w_router, b_router, w_gate, b_gate, w_up, b_up, w_down, b_down, g_ple,
             w_ple_gate, w_ple_proj, g_final, tcs):
    lambda_init = 0.8 - 0.6 * math.exp(-0.3 * i)
    lam = (jnp.exp(jnp.sum(lambda_q1[i] * lambda_k1[i]))
           - jnp.exp(jnp.sum(lambda_q2[i] * lambda_k2[i])) + lambda_init)
    slopes = jnp.exp2(-8.0 * jnp.arange(1, N_HEADS + 1, dtype=F32) / N_HEADS)
    bslopes = slopes * LOG2E
    parts, rest = [], bslopes
    for _ in range(N_BIAS_PARTS):
        part = rest.astype(BF16).astype(F32)
        parts.append(part)
        rest = rest - part
    attn_qf, attn_kf = _attn_features(jnp.stack(parts, axis=1), TOKEN_TILE)
    w_qku = jnp.concatenate([w_in[i][:, :2 * ATTN_WIDTH], w_in[i][:, 3 * ATTN_WIDTH:]], axis=1)
    row = lambda vec: vec.reshape(1, -1).astype(F32)
    w = {
        'g_mix': g_mix[i], 'w_qku': w_qku.astype(BF16),
        'w_vt': w_in[i][:, 2 * ATTN_WIDTH:3 * ATTN_WIDTH].T.astype(BF16),
        'attn_scal': jnp.concatenate([lam.reshape(1), bslopes]).astype(F32),
        'attn_qf': attn_qf, 'attn_kf': attn_kf,
        'g_subln': g_subln[i],
        'ssm_d': row(ssm_d[i]), 'w_glu': w_glu[i].astype(BF16), 'b_glu': row(b_glu[i]),
        'w_out': w_out[i].astype(BF16), 'g_ffn': row(g_ffn[i]),
        'w_router': jnp.pad(w_router[i], ((0, 0), (0, LANES - N_EXPERTS))),
        'b_router': jnp.pad(row(b_router[i]), ((0, 0), (0, LANES - N_EXPERTS)),
                            constant_values=-jnp.inf),
        'w_gate': w_gate[i].astype(BF16), 'b_gate': b_gate[i].reshape(N_EXPERTS, 1, D_MODEL),
        'w_up': w_up[i].astype(BF16), 'b_up': b_up[i].reshape(N_EXPERTS, 1, D_MODEL),
        'w_down': w_down[i].astype(BF16), 'b_down': b_down[i].reshape(N_EXPERTS, 1, D_MODEL),
        'g_ple': row(g_ple[i]), 'w_ple_gate': w_ple_gate[i].astype(BF16),
        'w_ple_proj': w_ple_proj[i].astype(BF16), 'g_final': row(g_final),
    }
    w['s5'] = {tc: [_s5_params(ssm_lambda_re[i, dr], ssm_lambda_im[i, dr], ssm_log_dt[i, dr],
                               ssm_b_re[i, dr], ssm_b_im[i, dr], ssm_c_re[i, dr], ssm_c_im[i, dr],
                               tc, reverse=(dr == 1)) for dr in range(2)] for tc in tcs}
    return w, lambda_init


def kernel(x_prompt, x_sample, p_prompt, p_sample, g_mix, w_in, lambda_q1, lambda_k1, lambda_q2, lambda_k2, g_subln, ssm_lambda_re, ssm_lambda_im, ssm_log_dt, ssm_b_re, ssm_b_im, ssm_c_re, ssm_c_im, ssm_d, w_glu, b_glu, w_out, g_ffn, w_router, b_router, w_gate, b_gate, w_up, b_up, w_down, b_down, g_ple, w_ple_gate, w_ple_proj, g_final):
    assert w_in.shape[0] == 1, "single-layer trunk"
    tcs = {_pick(x.shape[1], (256, 128)) for x in (x_prompt, x_sample)}
    w, lambda_init = _prepare(0, g_mix, w_in, lambda_q1, lambda_k1, lambda_q2, lambda_k2, g_subln,
                              ssm_lambda_re, ssm_lambda_im, ssm_log_dt, ssm_b_re, ssm_b_im, ssm_c_re,
                              ssm_c_im, ssm_d, w_glu, b_glu, w_out, g_ffn, w_router, b_router, w_gate,
                              b_gate, w_up, b_up, w_down, b_down, g_ple, w_ple_gate, w_ple_proj,
                              g_final, tcs)
    outs = []
    for x, p in ((x_prompt, p_prompt), (x_sample, p_sample)):
        tc = _pick(x.shape[1], (256, 128))
        wt = dict(w, s5=w['s5'][tc])
        outs.append(_trunk(x, p[0], wt, lambda_init))
    return tuple(outs)
```

```python
import functools
import math

import jax
import jax.numpy as jnp
import numpy as np
from jax import lax
from jax.experimental import pallas as pl
from jax.experimental.pallas import tpu as pltpu

F32 = jnp.float32
BF16 = jnp.bfloat16

D_MODEL = 1024
PLE_DIM = 256
ATTN_WIDTH = 512
SSM_WIDTH = 512
HEAD_W = 128
DIFF_HEAD_DIM = 64
N_HEADS = 4
SSM_GROUP = 16
N_GROUPS = 32
SSM_STATE = 64
N_STATE = N_GROUPS * SSM_STATE
N_EXPERTS = 32
TOP_K = 4
SWIGLU_LIMIT = 7.0
SWIGLU_ALPHA = 1.702
EPS = 1e-6
LANES = 128
TOKEN_TILE = 512

VMEM_LIMIT = 56 * 1024 * 1024


def _cparams(sem):
    return pltpu.CompilerParams(dimension_semantics=sem, vmem_limit_bytes=VMEM_LIMIT)


def _rms(x, g):
    return x * lax.rsqrt(jnp.mean(x * x, axis=-1, keepdims=True) + EPS) * g


LOG2E = 1.4426950408889634
NT_DIMS = (((1,), (1,)), ((), ()))


def _inproj_kernel(x_ref, g_ref, wqku_ref, wvt_ref, q_ref, k_ref, vt_ref, u_ref):
    xn = _rms(x_ref[...], g_ref[...]).astype(BF16)
    proj = jnp.dot(xn, wqku_ref[...], preferred_element_type=F32)
    scale = LOG2E / math.sqrt(DIFF_HEAD_DIM)
    q_ref[...] = (proj[:, :ATTN_WIDTH] * scale).astype(BF16)
    k_ref[...] = proj[:, ATTN_WIDTH:2 * ATTN_WIDTH].astype(BF16)
    u_ref[...] = proj[:, 2 * ATTN_WIDTH:]
    vt_ref[0] = lax.dot_general(wvt_ref[...], xn, NT_DIMS, preferred_element_type=F32).astype(BF16)


def _inproj(x, g_mix, w_qku_bf, w_vt_bf, tm):
    t = x.shape[0]
    row = lambda i: (i, 0)
    const = lambda i: (0, 0)
    return pl.pallas_call(
        _inproj_kernel,
        grid=(t // tm,),
        in_specs=[pl.BlockSpec((tm, D_MODEL), row),
                  pl.BlockSpec((1, D_MODEL), const),
                  pl.BlockSpec(w_qku_bf.shape, const),
                  pl.BlockSpec(w_vt_bf.shape, const)],
        out_specs=[pl.BlockSpec((tm, ATTN_WIDTH), row), pl.BlockSpec((tm, ATTN_WIDTH), row),
                   pl.BlockSpec((1, ATTN_WIDTH, tm), lambda i: (i, 0, 0)),
                   pl.BlockSpec((tm, SSM_WIDTH), row)],
        out_shape=[jax.ShapeDtypeStruct((t, ATTN_WIDTH), BF16), jax.ShapeDtypeStruct((t, ATTN_WIDTH), BF16),
                   jax.ShapeDtypeStruct((t // tm, ATTN_WIDTH, tm), BF16),
                   jax.ShapeDtypeStruct((t, SSM_WIDTH), F32)],
        compiler_params=_cparams(("parallel",)),
        name="inproj",
    )(x, g_mix.reshape(1, D_MODEL), w_qku_bf, w_vt_bf)


ONES_ROWS = 16
ACC_ROWS = HEAD_W + ONES_ROWS
POS_SPLIT = 16
N_BIAS_PARTS = 3


def _attn_kernel(sc_ref, q_ref, k_ref, vt_ref, qf_ref, kf_ref, g_ref, o_ref,
                 qa1_ref, qa2_ref, m1_ref, m2_ref, acc1_ref, acc2_ref, *, ts, out_scale):
    h = pl.program_id(1)
    qi = pl.program_id(2)
    nsub = k_ref.shape[1] // ts
    bslope = sc_ref[1 + h]

    q = q_ref[0]
    lane = lax.broadcasted_iota(jnp.int32, q.shape, 1)
    zero = jnp.zeros_like(q)
    qf = qf_ref[0]
    qa1_ref[...] = jnp.concatenate([jnp.where(lane < DIFF_HEAD_DIM, q, zero), qf], axis=1)
    qa2_ref[...] = jnp.concatenate([jnp.where(lane >= DIFF_HEAD_DIM, q, zero), qf], axis=1)
    m1_ref[...] = jnp.full_like(m1_ref, -jnp.inf)
    m2_ref[...] = jnp.full_like(m2_ref, -jnp.inf)
    acc1_ref[...] = jnp.zeros_like(acc1_ref)
    acc2_ref[...] = jnp.zeros_like(acc2_ref)
    ones = jnp.ones((ONES_ROWS, ts), BF16)

    def sub_tile(j, diag):
        rows = pl.ds(pl.multiple_of(j * ts, ts), ts)
        if diag:
            kf = kf_ref[0, 2]
            c = jnp.float32(0.0)
            kpos = lax.broadcasted_iota(jnp.int32, (ts, ts), 0)
            qpos = lax.broadcasted_iota(jnp.int32, (ts, ts), 1)
            bias = jnp.abs(qpos - kpos).astype(F32) * (-bslope)
        else:
            kf = kf_ref[0, jnp.where(j < qi, 0, 1)]
            c = -bslope * (jnp.abs(qi - j) * ts).astype(F32)
        ka = jnp.concatenate([k_ref[0, rows, :], kf], axis=1)
        vta = jnp.concatenate([vt_ref[j], ones], axis=0)
        for qa_ref, m_ref, acc_ref in ((qa1_ref, m1_ref, acc1_ref), (qa2_ref, m2_ref, acc2_ref)):
            s = lax.dot_general(ka, qa_ref[...], NT_DIMS, preferred_element_type=F32)
            if diag:
                s = s + bias
            m_old = m_ref[...]
            m_new = jnp.maximum(m_old, jnp.max(s, axis=0, keepdims=True) + c)
            alpha = jnp.exp2(m_old - m_new)
            p = jnp.exp2(s - (m_new - c)).astype(BF16)
            acc_ref[...] = alpha * acc_ref[...] + jnp.dot(vta, p, preferred_element_type=F32)
            m_ref[...] = m_new

    def body(j, carry):
        @pl.when(j != qi)
        def _():
            sub_tile(j, False)

        @pl.when(j == qi)
        def _():
            sub_tile(j, True)

        return carry

    lax.fori_loop(0, nsub, body, 0)

    lam = sc_ref[0]
    a1 = acc1_ref[...]
    a2 = acc2_ref[...]
    out_t = (a1[:HEAD_W] / a1[HEAD_W:HEAD_W + 1] - lam * (a2[:HEAD_W] / a2[HEAD_W:HEAD_W + 1]))
    ms = jnp.mean(out_t * out_t, axis=0, keepdims=True)
    out_t = out_t * lax.rsqrt(ms + EPS) * (g_ref[...] * out_scale)
    o_ref[0] = out_t.T.astype(o_ref.dtype)


def _attn_features(bparts, ts):
    pos = jnp.arange(ts, dtype=jnp.int32)
    hi = ((pos // POS_SPLIT) * POS_SPLIT).astype(F32)[None, :, None]
    lo = (pos % POS_SPLIT).astype(F32)[None, :, None]
    bp = bparts[:, None, :]
    one = jnp.ones_like(bp)
    qf = jnp.stack([bp + 0 * hi, bp + 0 * hi, hi * one, lo * one], axis=-1)
    kf = jnp.stack([hi * one, lo * one, -bp + 0 * hi, -bp + 0 * hi], axis=-1)
    pad = lambda f: jnp.pad(f.reshape(N_HEADS, ts, 4 * N_BIAS_PARTS),
                            ((0, 0), (0, 0), (0, HEAD_W - 4 * N_BIAS_PARTS)))
    qf, kf = pad(qf), pad(kf)
    kf = jnp.stack([kf, -kf, jnp.zeros_like(kf)], axis=1)
    return qf.astype(BF16), kf.astype(BF16)


def _attention(q, k, vt, scal, qf, kf, g_subln, lambda_init, ts):
    b, l, _ = q.shape
    nsub = l // ts
    kernel = functools.partial(_attn_kernel, ts=ts, out_scale=1.0 - lambda_init)
    return pl.pallas_call(
        kernel,
        grid=(b, N_HEADS, nsub),
        in_specs=[pl.BlockSpec(memory_space=pltpu.SMEM),
                  pl.BlockSpec((1, ts, HEAD_W), lambda bi, h, qi: (bi, qi, h)),
                  pl.BlockSpec((1, l, HEAD_W), lambda bi, h, qi: (bi, 0, h)),
                  pl.BlockSpec((nsub, HEAD_W, ts), lambda bi, h, qi: (bi, h, 0)),
                  pl.BlockSpec((1, ts, HEAD_W), lambda bi, h, qi: (h, 0, 0)),
                  pl.BlockSpec((1, 3, ts, HEAD_W), lambda bi, h, qi: (h, 0, 0, 0)),
                  pl.BlockSpec((HEAD_W, 1), lambda bi, h, qi: (0, 0))],
        out_specs=pl.BlockSpec((1, ts, HEAD_W), lambda bi, h, qi: (bi, qi, h)),
        out_shape=jax.ShapeDtypeStruct((b, l, ATTN_WIDTH), BF16),
        scratch_shapes=[pltpu.VMEM((ts, 2 * HEAD_W), BF16), pltpu.VMEM((ts, 2 * HEAD_W), BF16),
                        pltpu.VMEM((1, ts), F32), pltpu.VMEM((1, ts), F32),
                        pltpu.VMEM((ACC_ROWS, ts), F32), pltpu.VMEM((ACC_ROWS, ts), F32)],
        compiler_params=_cparams(("parallel", "parallel", "arbitrary")),
        name="attn",
    )(scal, q, k, vt, qf, kf, g_subln.reshape(HEAD_W, 1))


SCAN_LANES = 512


def _s5_kernel(u_ref, pm_ref, pmt_ref, bb_ref, cc_ref, a_ref, at_ref, apow_ref, y_ref,
               xs_ref, xb_ref, carry_ref, *, tc, reverse):
    tc8 = tc // 8
    c = pl.program_id(1)

    @pl.when(c == 0)
    def _():
        carry_ref[...] = jnp.zeros_like(carry_ref)

    up = jnp.dot(pm_ref[...], u_ref[0].astype(BF16), preferred_element_type=F32).astype(BF16)
    xs_ref[...] = jnp.dot(up, bb_ref[...], preferred_element_type=F32)

    for cb in range(N_STATE // SCAN_LANES):
        lo = cb * SCAN_LANES
        re_sl = pl.ds(lo, SCAN_LANES)
        im_sl = pl.ds(N_STATE + lo, SCAN_LANES)
        ar = jnp.broadcast_to(a_ref[0:1, lo:lo + SCAN_LANES], (8, SCAN_LANES))
        ai = jnp.broadcast_to(a_ref[1:2, lo:lo + SCAN_LANES], (8, SCAN_LANES))

        def step(i, carry):
            xr, xi = carry
            r = (tc8 - 1 - i) if reverse else i
            rows = pl.ds(pl.multiple_of(r * 8, 8), 8)
            nr = ar * xr - ai * xi + xs_ref[rows, re_sl]
            ni = ar * xi + ai * xr + xs_ref[rows, im_sl]
            xs_ref[rows, re_sl] = nr
            xs_ref[rows, im_sl] = ni
            return nr, ni

        z = jnp.zeros((8, SCAN_LANES), F32)
        lax.fori_loop(0, tc8, step, (z, z), unroll=4)

    last = 0 if reverse else (tc8 - 1) * 8
    e_re = xs_ref[last:last + 8, 0:N_STATE]
    e_im = xs_ref[last:last + 8, N_STATE:2 * N_STATE]
    atr = at_ref[0:1, :]
    ati = at_ref[1:2, :]
    cr = carry_ref[0:1, :]
    ci = carry_ref[1:2, :]
    cin_r = [None] * 8
    cin_i = [None] * 8
    order = range(7, -1, -1) if reverse else range(8)
    for j in order:
        cin_r[j] = cr
        cin_i[j] = ci
        nr = atr * cr - ati * ci + e_re[j:j + 1]
        ni = atr * ci + ati * cr + e_im[j:j + 1]
        cr, ci = nr, ni
    carry_ref[0:1, :] = cr
    carry_ref[1:2, :] = ci
    cin_r = jnp.concatenate(cin_r + cin_r, axis=0)
    cin_i = jnp.concatenate(cin_i + cin_i, axis=0)

    def fix(i, _):
        rows = pl.ds(pl.multiple_of(i * 16, 16), 16)
        pr = apow_ref[rows, 0:N_STATE]
        pi = apow_ref[rows, N_STATE:2 * N_STATE]
        xr = xs_ref[rows, 0:N_STATE] + pr * cin_r - pi * cin_i
        xi = xs_ref[rows, N_STATE:2 * N_STATE] + pr * cin_i + pi * cin_r
        xb_ref[rows, 0:N_STATE] = xr.astype(BF16)
        xb_ref[rows, N_STATE:2 * N_STATE] = xi.astype(BF16)
        return 0

    lax.fori_loop(0, tc // 16, fix, 0)

    yp = jnp.dot(xb_ref[...], cc_ref[...], preferred_element_type=F32)
    hi = yp.astype(BF16)
    lo_part = (yp - hi.astype(F32)).astype(BF16)
    y_ref[0] = (jnp.dot(pmt_ref[...], hi, preferred_element_type=F32)
                + jnp.dot(pmt_ref[...], lo_part, preferred_element_type=F32))


def _s5_direction(u, pm, pmt, bb, cc, a, at, apow, tc, reverse):
    b, l, _ = u.shape
    nc = l // tc
    cmap = (lambda bi, c: (bi, nc - 1 - c, 0)) if reverse else (lambda bi, c: (bi, c, 0))
    const = lambda bi, c: (0, 0)
    kernel = functools.partial(_s5_kernel, tc=tc, reverse=reverse)
    return pl.pallas_call(
        kernel,
        grid=(b, nc),
        in_specs=[pl.BlockSpec((1, tc, SSM_WIDTH), cmap),
                  pl.BlockSpec(pm.shape, const), pl.BlockSpec(pmt.shape, const),
                  pl.BlockSpec(bb.shape, const), pl.BlockSpec(cc.shape, const),
                  pl.BlockSpec(a.shape, const), pl.BlockSpec(at.shape, const),
                  pl.BlockSpec(apow.shape, const)],
        out_specs=pl.BlockSpec((1, tc, SSM_WIDTH), cmap),
        out_shape=jax.ShapeDtypeStruct((b, l, SSM_WIDTH), F32),
        scratch_shapes=[pltpu.VMEM((tc, 2 * N_STATE), F32), pltpu.VMEM((tc, 2 * N_STATE), BF16),
                        pltpu.VMEM((2, N_STATE), F32)],
        compiler_params=_cparams(("parallel", "arbitrary")),
        name="s5_bwd" if reverse else "s5_fwd",
    )(u, pm, pmt, bb, cc, a, at, apow)


def _s5_params(lam_re, lam_im, log_dt, b_re, b_im, c_re, c_im, tc, reverse):
    tc8 = tc // 8
    dt = jnp.exp(log_dt)[:, None]
    mag = jnp.exp(lam_re * dt)
    ab_re = mag * jnp.cos(lam_im * dt)
    ab_im = mag * jnp.sin(lam_im * dt)
    den = lam_re * lam_re + lam_im * lam_im
    nr, ni = ab_re - 1.0, ab_im
    co_re = (nr * lam_re + ni * lam_im) / den
    co_im = (ni * lam_re - nr * lam_im) / den
    bb_re = co_re[..., None] * b_re - co_im[..., None] * b_im
    bb_im = co_re[..., None] * b_im + co_im[..., None] * b_re
    eye = jnp.eye(N_GROUPS, dtype=F32)
    blk_in = lambda w: jnp.einsum('gph,gk->ghkp', w, eye).reshape(SSM_WIDTH, N_STATE)
    bb = jnp.concatenate([blk_in(bb_re), blk_in(bb_im)], axis=1).astype(BF16)
    blk_out = lambda w: jnp.einsum('ghp,gk->gpkh', w, eye).reshape(N_STATE, SSM_WIDTH)
    cc = jnp.concatenate([blk_out(c_re), -blk_out(c_im)], axis=0).astype(BF16)
    a = jnp.stack([ab_re.reshape(-1), ab_im.reshape(-1)])
    n = jnp.arange(1, tc8 + 1, dtype=F32)[:, None, None]
    pmag = jnp.exp(n * (lam_re * dt)[None])
    ang = n * (lam_im * dt)[None]
    pw_re = (pmag * jnp.cos(ang)).reshape(tc8, N_STATE)
    pw_im = (pmag * jnp.sin(ang)).reshape(tc8, N_STATE)
    at = jnp.stack([pw_re[-1], pw_im[-1]])
    if reverse:
        pw_re, pw_im = pw_re[::-1], pw_im[::-1]
    apow = jnp.concatenate([jnp.repeat(pw_re, 8, axis=0), jnp.repeat(pw_im, 8, axis=0)], axis=1)
    return bb, cc, a, at, apow


def _perm_matrices(tc):
    tc8 = tc // 8
    i = np.arange(tc)
    src = (i % 8) * tc8 + i // 8
    pm = np.zeros((tc, tc), np.float32)
    pm[i, src] = 1.0
    return jnp.asarray(pm, BF16), jnp.asarray(pm.T, BF16)


def _mix_kernel(h_ref, a_ref, u_ref, yf_ref, yb_ref, d_ref, wglu_ref, bglu_ref, wout_ref,
                gffn_ref, wr_ref, br_ref, h1_ref, hn_ref, ti_ref, tg_ref):
    y = d_ref[...] * u_ref[...] + yf_ref[...] + yb_ref[...]
    y = jax.nn.gelu(y)
    z = jnp.dot(y.astype(BF16), wglu_ref[...], preferred_element_type=F32) + bglu_ref[...]
    s = y * jax.nn.sigmoid(z)
    h1 = (h_ref[...]
          + jnp.dot(a_ref[...], wout_ref[0:ATTN_WIDTH, :], preferred_element_type=F32)
          + jnp.dot(s.astype(BF16), wout_ref[ATTN_WIDTH:, :], preferred_element_type=F32))
    h1_ref[...] = h1
    hn = _rms(h1, gffn_ref[...])
    hn_ref[...] = hn
    logits = jnp.dot(hn, wr_ref[...], preferred_element_type=F32,
                     precision=lax.Precision.HIGHEST) + br_ref[...]
    lane = lax.broadcasted_iota(jnp.int32, logits.shape, 1)
    neg = jnp.float32(-jnp.inf)
    work = logits
    ti = jnp.zeros(logits.shape, jnp.int32)
    tv = jnp.full(logits.shape, neg, F32)
    for kk in range(TOP_K):
        mx = jnp.max(work, axis=-1, keepdims=True)
        idx = jnp.min(jnp.where(work == mx, lane, LANES), axis=-1, keepdims=True)
        ti = jnp.where(lane == kk, idx, ti)
        tv = jnp.where(lane == kk, mx, tv)
        work = jnp.where(lane == idx, neg, work)
    ex = jnp.exp(tv - jnp.max(tv, axis=-1, keepdims=True))
    ti_ref[...] = ti
    tg_ref[...] = ex / jnp.sum(ex, axis=-1, keepdims=True)


def _mix(h, a, u, yf, yb, d, wglu_bf, bglu, wout_bf, g_ffn, wr_pad, br_pad, tm):
    t = h.shape[0]
    row = lambda i: (i, 0)
    const = lambda i: (0, 0)
    full = lambda arr: pl.BlockSpec(arr.shape, const)
    return pl.pallas_call(
        _mix_kernel,
        grid=(t // tm,),
        in_specs=[pl.BlockSpec((tm, D_MODEL), row), pl.BlockSpec((tm, ATTN_WIDTH), row),
                  pl.BlockSpec((tm, SSM_WIDTH), row), pl.BlockSpec((tm, SSM_WIDTH), row),
                  pl.BlockSpec((tm, SSM_WIDTH), row),
                  full(d), full(wglu_bf), full(bglu), full(wout_bf), full(g_ffn), full(wr_pad),
                  full(br_pad)],
        out_specs=[pl.BlockSpec((tm, D_MODEL), row), pl.BlockSpec((tm, D_MODEL), row),
                   pl.BlockSpec((tm, LANES), row), pl.BlockSpec((tm, LANES), row)],
        out_shape=[jax.ShapeDtypeStruct((t, D_MODEL), F32), jax.ShapeDtypeStruct((t, D_MODEL), F32),
                   jax.ShapeDtypeStruct((t, LANES), jnp.int32), jax.ShapeDtypeStruct((t, LANES), F32)],
        compiler_params=_cparams(("parallel",)),
        name="mix",
    )(h, a, u, yf, yb, d, wglu_bf, bglu, wout_bf, g_ffn, wr_pad, br_pad)


ROWS_PER_ISSUE = 4

def _dispatch_kernel(dest_ref, hn_ref, xs_in_ref, xs_ref, sem, *, tm):
    del xs_in_ref

    def issue(g, _):
        for rr in range(ROWS_PER_ISSUE):
            r = g * ROWS_PER_ISSUE + rr
            for kk in range(TOP_K):
                pltpu.make_async_copy(hn_ref.at[pl.ds(r, 1), :],
                                      xs_ref.at[pl.ds(dest_ref[r * TOP_K + kk], 1), :], sem).start()
        return 0

    lax.fori_loop(0, tm // ROWS_PER_ISSUE, issue, 0)

    def drain(g, _):
        for _unused in range(ROWS_PER_ISSUE * TOP_K):
            pltpu.make_async_copy(hn_ref.at[pl.ds(0, 1), :], xs_ref.at[pl.ds(0, 1), :], sem).wait()
        return 0

    lax.fori_loop(0, tm // ROWS_PER_ISSUE, drain, 0)


def _dispatch(dest_flat, hn, xs_init, tm):
    t = hn.shape[0]
    kernel = functools.partial(_dispatch_kernel, tm=tm)
    return pl.pallas_call(
        kernel,
        grid=(t // tm,),
        in_specs=[pl.BlockSpec((tm * TOP_K,), lambda i: (i,), memory_space=pltpu.SMEM),
                  pl.BlockSpec((tm, D_MODEL), lambda i: (i, 0)),
                  pl.BlockSpec(memory_space=pl.ANY)],
        out_specs=pl.BlockSpec(memory_space=pl.ANY),
        out_shape=jax.ShapeDtypeStruct(xs_init.shape, F32),
        scratch_shapes=[pltpu.SemaphoreType.DMA(())],
        input_output_aliases={2: 0},
        compiler_params=_cparams(("arbitrary",)),
        name="dispatch",
    )(dest_flat, hn, xs_init)


def _experts_kernel(be_ref, x_ref, wg_ref, bg_ref, wu_ref, bu_ref, wd_ref, bd_ref, y_ref):
    del be_ref
    x = x_ref[...].astype(BF16)
    gt = jnp.minimum(jnp.dot(x, wg_ref[0], preferred_element_type=F32) + bg_ref[0], SWIGLU_LIMIT)
    up = jnp.clip(jnp.dot(x, wu_ref[0], preferred_element_type=F32) + bu_ref[0],
                  -SWIGLU_LIMIT, SWIGLU_LIMIT)
    hh = (up + 1.0) * (gt * jax.nn.sigmoid(SWIGLU_ALPHA * gt))
    y_ref[...] = jnp.dot(hh.astype(BF16), wd_ref[0], preferred_element_type=F32) + bd_ref[0]


def _experts(blk_expert, xs, wg, bg, wu, bu, wd, bd, eb):
    n_pad = xs.shape[0]
    row = lambda i, be: (i, 0)
    wmap = lambda i, be: (be[i], 0, 0)
    wspec = pl.BlockSpec((1, D_MODEL, D_MODEL), wmap)
    bspec = pl.BlockSpec((1, 1, D_MODEL), wmap)
    grid_spec = pltpu.PrefetchScalarGridSpec(
        num_scalar_prefetch=1,
        grid=(n_pad // eb,),
        in_specs=[pl.BlockSpec((eb, D_MODEL), row), wspec, bspec, wspec, bspec, wspec, bspec],
        out_specs=pl.BlockSpec((eb, D_MODEL), row),
    )
    return pl.pallas_call(
        _experts_kernel,
        grid_spec=grid_spec,
        out_shape=jax.ShapeDtypeStruct((n_pad, D_MODEL), F32),
        compiler_params=_cparams(("arbitrary",)),
        name="experts",
    )(blk_expert, xs, wg, bg, wu, bu, wd, bd)


def _tail_kernel(dest_ref, h1_ref, tg_ref, p_ref, ys_ref, gple_ref, wpg_ref, wpp_ref, gfin_ref,
                 o_ref, buf_ref, sem, *, tm):
    def issue(g, _):
        for rr in range(ROWS_PER_ISSUE):
            r = g * ROWS_PER_ISSUE + rr
            for kk in range(TOP_K):
                pltpu.make_async_copy(ys_ref.at[pl.ds(dest_ref[r * TOP_K + kk], 1), :],
                                      buf_ref.at[kk, pl.ds(r, 1), :], sem).start()
        return 0

    lax.fori_loop(0, tm // ROWS_PER_ISSUE, issue, 0)

    def drain(g, _):
        for _unused in range(ROWS_PER_ISSUE * TOP_K):
            pltpu.make_async_copy(ys_ref.at[pl.ds(0, 1), :], buf_ref.at[0, pl.ds(0, 1), :], sem).wait()
        return 0

    lax.fori_loop(0, tm // ROWS_PER_ISSUE, drain, 0)

    tg = tg_ref[...]
    h2 = h1_ref[...]
    for kk in range(TOP_K):
        h2 = h2 + tg[:, kk:kk + 1] * buf_ref[kk]
    gate = jax.nn.sigmoid(jnp.dot(_rms(h2, gple_ref[...]).astype(BF16), wpg_ref[...],
                                  preferred_element_type=F32))
    proj = jnp.dot(p_ref[...].astype(BF16), wpp_ref[...], preferred_element_type=F32)
    o_ref[...] = _rms(h2 + gate * proj, gfin_ref[...])


def _tail(dest_flat, h1, tg, p, ys, g_ple, wpg_bf, wpp_bf, g_final, tm):
    t = h1.shape[0]
    row = lambda i: (i, 0)
    const = lambda i: (0, 0)
    full = lambda arr: pl.BlockSpec(arr.shape, const)
    kernel = functools.partial(_tail_kernel, tm=tm)
    return pl.pallas_call(
        kernel,
        grid=(t // tm,),
        in_specs=[pl.BlockSpec((tm * TOP_K,), lambda i: (i,), memory_space=pltpu.SMEM),
                  pl.BlockSpec((tm, D_MODEL), row), pl.BlockSpec((tm, LANES), row),
                  pl.BlockSpec((tm, PLE_DIM), row),
                  pl.BlockSpec(memory_space=pl.ANY),
                  full(g_ple), full(wpg_bf), full(wpp_bf), full(g_final)],
        out_specs=pl.BlockSpec((tm, D_MODEL), row),
        out_shape=jax.ShapeDtypeStruct((t, D_MODEL), F32),
        scratch_shapes=[pltpu.VMEM((TOP_K, tm, D_MODEL), F32), pltpu.SemaphoreType.DMA(())],
        compiler_params=_cparams(("arbitrary",)),
        name="tail",
    )(dest_flat, h1, tg, p, ys, g_ple, wpg_bf, wpp_bf, g_final)


def _routing(top_i, eb):
    t = top_i.shape[0]
    onehot = (top_i[:, :, None] == jnp.arange(N_EXPERTS, dtype=jnp.int32)).astype(jnp.int32)
    per_tok = onehot.sum(axis=1)
    incl = jnp.cumsum(per_tok, axis=0)
    counts = incl[-1]
    excl = incl - per_tok
    padded = ((counts + eb - 1) // eb) * eb
    pends = jnp.cumsum(padded)
    pstarts = pends - padded
    rank = jnp.take_along_axis(excl + pstarts[None, :], top_i, axis=1)
    n_blocks = -(-(t * TOP_K + N_EXPERTS * (eb - 1)) // eb)
    blk_start = jnp.arange(n_blocks, dtype=jnp.int32) * eb
    blk_expert = jnp.minimum(jnp.searchsorted(pends, blk_start, side='right'), N_EXPERTS - 1)
    return rank.reshape(-1).astype(jnp.int32), blk_expert.astype(jnp.int32), n_blocks * eb


def _pick(n, prefs):
    for c in prefs:
        if n % c == 0:
            return c
    raise ValueError(f"no tile for {n}")


def _trunk(x, p, w, lambda_init):
    b, l, _ = x.shape
    t = b * l
    tm = TOKEN_TILE
    assert l % tm == 0
    tc = _pick(l, (256, 128))
    eb = 256

    q, k, vt, u = _inproj(x.reshape(t, D_MODEL), w['g_mix'], w['w_qku'], w['w_vt'], tm)
    a = _attention(q.reshape(b, l, -1), k.reshape(b, l, -1), vt, w['attn_scal'], w['attn_qf'],
                   w['attn_kf'], w['g_subln'], lambda_init, tm)
    u3 = u.reshape(b, l, SSM_WIDTH)
    pm, pmt = _perm_matrices(tc)
    ys = [_s5_direction(u3, pm, pmt, *w['s5'][dr], tc, reverse=(dr == 1)) for dr in range(2)]
    h1, hn, ti, tg = _mix(x.reshape(t, D_MODEL), a.reshape(t, ATTN_WIDTH), u,
                          ys[0].reshape(t, SSM_WIDTH), ys[1].reshape(t, SSM_WIDTH),
                          w['ssm_d'], w['w_glu'], w['b_glu'], w['w_out'], w['g_ffn'],
                          w['w_router'], w['b_router'], tm)
    dest, blk_expert, n_pad = _routing(ti[:, :TOP_K], eb)
    xs = _dispatch(dest, hn, jnp.zeros((n_pad, D_MODEL), F32), tm // 2)
    ye = _experts(blk_expert, xs, w['w_gate'], w['b_gate'], w['w_up'], w['b_up'],
                  w['w_down'], w['b_down'], eb)
    out = _tail(dest, h1, tg, p.reshape(t, PLE_DIM), ye, w['g_ple'], w['w_ple_gate'],
                w['w_ple_proj'], w['g_final'], tm // 2)
    return out.reshape(b, l, D_MODEL)


def _prepare(i, g_mix, w_in, lambda_q1, lambda_k1, lambda_q2, lambda_k2, g_subln, ssm_lambda_re,
             ssm_lambda_im, ssm_log_dt, ssm_b_re, ssm_b_im, ssm_c_re, ssm_c_im, ssm_d, w_glu, b_glu,
             w_out, g_ffn, w_router, b_router, w_gate, b_gate, w_up, b_up, w_down, b_down, g_ple,
             w_ple_gate, w_ple_proj, g_final, tcs):
    lambda_init = 0.8 - 0.6 * math.exp(-0.3 * i)
    lam = (jnp.exp(jnp.sum(lambda_q1[i] * lambda_k1[i]))
           - jnp.exp(jnp.sum(lambda_q2[i] * lambda_k2[i])) + lambda_init)
    slopes = jnp.exp2(-8.0 * jnp.arange(1, N_HEADS + 1, dtype=F32) / N_HEADS)
    bslopes = slopes * LOG2E
    parts, rest = [], bslopes
    for _ in range(N_BIAS_PARTS):
        part = rest.astype(BF16).astype(F32)
        parts.append(part)
        rest = rest - part
    attn_qf, attn_kf = _attn_features(jnp.stack(parts, axis=1), TOKEN_TILE)
    w_qku = jnp.concatenate([w_in[i][:, :2 * ATTN_WIDTH], w_in[i][:, 3 * ATTN_WIDTH:]], axis=1)
    row = lambda vec: vec.reshape(1, -1).astype(F32)
    w = {
        'g_mix': g_mix[i], 'w_qku': w_qku.astype(BF16),
        'w_vt': w_in[i][:, 2 * ATTN_WIDTH:3 * ATTN_WIDTH].T.astype(BF16),
        'attn_scal': jnp.concatenate([lam.reshape(1), bslopes]).astype(F32),
        'attn_qf': attn_qf, 'attn_kf': attn_kf,
        'g_subln': g_subln[i],
        'ssm_d': row(ssm_d[i]), 'w_glu': w_glu[i].astype(BF16), 'b_glu': row(b_glu[i]),
        'w_out': w_out[i].astype(BF16), 'g_ffn': row(g_ffn[i]),
        'w_router': jnp.pad(w_router[i], ((0, 0), (0, LANES - N_EXPERTS))),
        'b_router': jnp.pad(row(b_router[i]), ((0, 0), (0, LANES - N_EXPERTS)),
                            constant_values=-jnp.inf),
        'w_gate': w_gate[i].astype(BF16), 'b_gate': b_gate[i].reshape(N_EXPERTS, 1, D_MODEL),
        'w_up': w_up[i].astype(BF16), 'b_up': b_up[i].reshape(N_EXPERTS, 1, D_MODEL),
        'w_down': w_down[i].astype(BF16), 'b_down': b_down[i].reshape(N_EXPERTS, 1, D_MODEL),
        'g_ple': row(g_ple[i]), 'w_ple_gate': w_ple_gate[i].astype(BF16),
        'w_ple_proj': w_ple_proj[i].astype(BF16), 'g_final': row(g_final),
    }
    w['s5'] = {tc: [_s5_params(ssm_lambda_re[i, dr], ssm_lambda_im[i, dr], ssm_log_dt[i, dr],
                               ssm_b_re[i, dr], ssm_b_im[i, dr], ssm_c_re[i, dr], ssm_c_im[i, dr],
                               tc, reverse=(dr == 1)) for dr in range(2)] for tc in tcs}
    return w, lambda_init


def kernel(x_prompt, x_sample, p_prompt, p_sample, g_mix, w_in, lambda_q1, lambda_k1, lambda_q2, lambda_k2, g_subln, ssm_lambda_re, ssm_lambda_im, ssm_log_dt, ssm_b_re, ssm_b_im, ssm_c_re, ssm_c_im, ssm_d, w_glu, b_glu, w_out, g_ffn, w_router, b_router, w_gate, b_gate, w_up, b_up, w_down, b_down, g_ple, w_ple_gate, w_ple_proj, g_final):
    assert w_in.shape[0] == 1, "single-layer trunk"
    tcs = {_pick(x.shape[1], (256, 128)) for x in (x_prompt, x_sample)}
    w, lambda_init = _prepare(0, g_mix, w_in, lambda_q1, lambda_k1, lambda_q2, lambda_k2, g_subln,
                              ssm_lambda_re, ssm_lambda_im, ssm_log_dt, ssm_b_re, ssm_b_im, ssm_c_re,
                              ssm_c_im, ssm_d, w_glu, b_glu, w_out, g_ffn, w_router, b_router, w_gate,
                              b_gate, w_up, b_up, w_down, b_down, g_ple, w_ple_gate, w_ple_proj,
                              g_final, tcs)
    outs = []
    for x, p in ((x_prompt, p_prompt), (x_sample, p_sample)):
        tc = _pick(x.shape[1], (256, 128))
        wt = dict(w, s5=w['s5'][tc])
        outs.append(_trunk(x, p[0], wt, lambda_init))
    return tuple(outs)
```

```python
import functools
import math

import jax
import jax.numpy as jnp
import numpy as np
from jax import lax
from jax.experimental import pallas as pl
from jax.experimental.pallas import tpu as pltpu

F32 = jnp.float32
BF16 = jnp.bfloat16

D_MODEL = 1024
PLE_DIM = 256
ATTN_WIDTH = 512
SSM_WIDTH = 512
HEAD_W = 128
DIFF_HEAD_DIM = 64
N_HEADS = 4
SSM_GROUP = 16
N_GROUPS = 32
SSM_STATE = 64
N_STATE = N_GROUPS * SSM_STATE
N_EXPERTS = 32
TOP_K = 4
SWIGLU_LIMIT = 7.0
SWIGLU_ALPHA = 1.702
EPS = 1e-6
LANES = 128
TOKEN_TILE = 512

VMEM_LIMIT = 56 * 1024 * 1024


def _cparams(sem):
    return pltpu.CompilerParams(dimension_semantics=sem, vmem_limit_bytes=VMEM_LIMIT)


def _rms(x, g):
    return x * lax.rsqrt(jnp.mean(x * x, axis=-1, keepdims=True) + EPS) * g


LOG2E = 1.4426950408889634
NT_DIMS = (((1,), (1,)), ((), ()))


def _inproj_kernel(x_ref, g_ref, wqku_ref, wvt_ref, q_ref, k_ref, vt_ref, u_ref):
    xn = _rms(x_ref[...], g_ref[...]).astype(BF16)
    proj = jnp.dot(xn, wqku_ref[...], preferred_element_type=F32)
    scale = LOG2E / math.sqrt(DIFF_HEAD_DIM)
    q_ref[...] = (proj[:, :ATTN_WIDTH] * scale).astype(BF16)
    k_ref[...] = proj[:, ATTN_WIDTH:2 * ATTN_WIDTH].astype(BF16)
    u_ref[...] = proj[:, 2 * ATTN_WIDTH:]
    vt_ref[0] = lax.dot_general(wvt_ref[...], xn, NT_DIMS, preferred_element_type=F32).astype(BF16)


def _inproj(x, g_mix, w_qku_bf, w_vt_bf, tm):
    t = x.shape[0]
    row = lambda i: (i, 0)
    const = lambda i: (0, 0)
    return pl.pallas_call(
        _inproj_kernel,
        grid=(t // tm,),
        in_specs=[pl.BlockSpec((tm, D_MODEL), row),
                  pl.BlockSpec((1, D_MODEL), const),
                  pl.BlockSpec(w_qku_bf.shape, const),
                  pl.BlockSpec(w_vt_bf.shape, const)],
        out_specs=[pl.BlockSpec((tm, ATTN_WIDTH), row), pl.BlockSpec((tm, ATTN_WIDTH), row),
                   pl.BlockSpec((1, ATTN_WIDTH, tm), lambda i: (i, 0, 0)),
                   pl.BlockSpec((tm, SSM_WIDTH), row)],
        out_shape=[jax.ShapeDtypeStruct((t, ATTN_WIDTH), BF16), jax.ShapeDtypeStruct((t, ATTN_WIDTH), BF16),
                   jax.ShapeDtypeStruct((t // tm, ATTN_WIDTH, tm), BF16),
                   jax.ShapeDtypeStruct((t, SSM_WIDTH), F32)],
        compiler_params=_cparams(("parallel",)),
        name="inproj",
    )(x, g_mix.reshape(1, D_MODEL), w_qku_bf, w_vt_bf)


ONES_ROWS = 16
ACC_ROWS = HEAD_W + ONES_ROWS
POS_SPLIT = 16
N_BIAS_PARTS = 3


def _attn_kernel(sc_ref, q_ref, k_ref, vt_ref, qf_ref, kf_ref, g_ref, o_ref,
                 qa_ref, m_ref, acc_ref, sa_ref, sb_ref, pa_ref, pb_ref, ala_ref, alb_ref,
                 *, ts, out_scale):
    h = pl.program_id(1)
    qi = pl.program_id(2)
    nsub = k_ref.shape[1] // ts
    n_off = nsub - 1
    bslope = sc_ref[1 + h]

    q = q_ref[0]
    lane = lax.broadcasted_iota(jnp.int32, q.shape, 1)
    zero = jnp.zeros_like(q)
    qf = qf_ref[0]
    qa_ref[0] = jnp.concatenate([jnp.where(lane < DIFF_HEAD_DIM, q, zero), qf], axis=1)
    qa_ref[1] = jnp.concatenate([jnp.where(lane >= DIFF_HEAD_DIM, q, zero), qf], axis=1)
    ones = jnp.ones((ONES_ROWS, ts), BF16)

    def sub_index(n):
        n = jnp.clip(n, 0, max(n_off - 1, 0))
        return n + (n >= qi).astype(jnp.int32)

    def keys(j, variant):
        rows = pl.ds(pl.multiple_of(j * ts, ts), ts)
        return jnp.concatenate([k_ref[0, rows, :], kf_ref[0, variant]], axis=1)

    def values(j):
        return jnp.concatenate([vt_ref[j], ones], axis=0)

    def qk(n, s_ref):
        j = sub_index(n)
        ka = keys(j, jnp.where(j < qi, 0, 1))
        for st in range(2):
            s_ref[st] = lax.dot_general(ka, qa_ref[st], NT_DIMS, preferred_element_type=F32)

    def softmax(n, s_ref, p_ref, al_ref):
        j = sub_index(n)
        c = -bslope * (jnp.abs(qi - j) * ts).astype(F32)
        for st in range(2):
            m_old = m_ref[st]
            m_new = jnp.maximum(m_old, jnp.max(s_ref[st], axis=0, keepdims=True) + c)
            al_ref[st] = jnp.exp2(m_old - m_new)
            p_ref[st] = jnp.exp2(s_ref[st] - (m_new - c)).astype(BF16)
            m_ref[st] = m_new

    def pv(n, p_ref, al_ref):
        vta = values(sub_index(n))
        for st in range(2):
            acc_ref[st] = al_ref[st] * acc_ref[st] + jnp.dot(vta, p_ref[st], preferred_element_type=F32)

    kpos = lax.broadcasted_iota(jnp.int32, (ts, ts), 0)
    qpos = lax.broadcasted_iota(jnp.int32, (ts, ts), 1)
    bias = jnp.abs(qpos - kpos).astype(F32) * (-bslope)
    ka = keys(qi, 2)
    vta = values(qi)
    for st in range(2):
        s = lax.dot_general(ka, qa_ref[st], NT_DIMS, preferred_element_type=F32) + bias
        m_new = jnp.max(s, axis=0, keepdims=True)
        p = jnp.exp2(s - m_new).astype(BF16)
        acc_ref[st] = jnp.dot(vta, p, preferred_element_type=F32)
        m_ref[st] = m_new

    if n_off > 0:
        for p_ref, al_ref in ((pa_ref, ala_ref), (pb_ref, alb_ref)):
            p_ref[...] = jnp.zeros_like(p_ref)
            al_ref[...] = jnp.ones_like(al_ref)
        qk(0, sa_ref)

        def pair(g, carry):
            n = 2 * g
            pv(n - 2, pa_ref, ala_ref)
            qk(n + 1, sb_ref)
            softmax(n, sa_ref, pa_ref, ala_ref)
            pv(n - 1, pb_ref, alb_ref)
            qk(n + 2, sa_ref)
            softmax(n + 1, sb_ref, pb_ref, alb_ref)
            return carry

        n_pairs = n_off // 2
        lax.fori_loop(0, n_pairs, pair, 0)
        pv(2 * n_pairs - 2, pa_ref, ala_ref)
        if n_off % 2:
            softmax(n_off - 1, sa_ref, pa_ref, ala_ref)
        pv(2 * n_pairs - 1, pb_ref, alb_ref)
        if n_off % 2:
            pv(n_off - 1, pa_ref, ala_ref)

    lam = sc_ref[0]
    a1 = acc_ref[0]
    a2 = acc_ref[1]
    out_t = (a1[:HEAD_W] / a1[HEAD_W:HEAD_W + 1] - lam * (a2[:HEAD_W] / a2[HEAD_W:HEAD_W + 1]))
    ms = jnp.mean(out_t * out_t, axis=0, keepdims=True)
    out_t = out_t * lax.rsqrt(ms + EPS) * (g_ref[...] * out_scale)
    o_ref[0] = out_t.T.astype(o_ref.dtype)


def _attn_features(bparts, ts):
    pos = jnp.arange(ts, dtype=jnp.int32)
    hi = ((pos // POS_SPLIT) * POS_SPLIT).astype(F32)[None, :, None]
    lo = (pos % POS_SPLIT).astype(F32)[None, :, None]
    bp = bparts[:, None, :]
    one = jnp.ones_like(bp)
    qf = jnp.stack([bp + 0 * hi, bp + 0 * hi, hi * one, lo * one], axis=-1)
    kf = jnp.stack([hi * one, lo * one, -bp + 0 * hi, -bp + 0 * hi], axis=-1)
    pad = lambda f: jnp.pad(f.reshape(N_HEADS, ts, 4 * N_BIAS_PARTS),
                            ((0, 0), (0, 0), (0, HEAD_W - 4 * N_BIAS_PARTS)))
    qf, kf = pad(qf), pad(kf)
    kf = jnp.stack([kf, -kf, jnp.zeros_like(kf)], axis=1)
    return qf.astype(BF16), kf.astype(BF16)


def _attention(q, k, vt, scal, qf, kf, g_subln, lambda_init, ts):
    b, l, _ = q.shape
    nsub = l // ts
    kernel = functools.partial(_attn_kernel, ts=ts, out_scale=1.0 - lambda_init)
    return pl.pallas_call(
        kernel,
        grid=(b, N_HEADS, nsub),
        in_specs=[pl.BlockSpec(memory_space=pltpu.SMEM),
                  pl.BlockSpec((1, ts, HEAD_W), lambda bi, h, qi: (bi, qi, h)),
                  pl.BlockSpec((1, l, HEAD_W), lambda bi, h, qi: (bi, 0, h)),
                  pl.BlockSpec((nsub, HEAD_W, ts), lambda bi, h, qi: (bi, h, 0)),
                  pl.BlockSpec((1, ts, HEAD_W), lambda bi, h, qi: (h, 0, 0)),
                  pl.BlockSpec((1, 3, ts, HEAD_W), lambda bi, h, qi: (h, 0, 0, 0)),
                  pl.BlockSpec((HEAD_W, 1), lambda bi, h, qi: (0, 0))],
        out_specs=pl.BlockSpec((1, ts, HEAD_W), lambda bi, h, qi: (bi, qi, h)),
        out_shape=jax.ShapeDtypeStruct((b, l, ATTN_WIDTH), BF16),
        scratch_shapes=[pltpu.VMEM((2, ts, 2 * HEAD_W), BF16),
                        pltpu.VMEM((2, 1, ts), F32),
                        pltpu.VMEM((2, ACC_ROWS, ts), F32),
                        pltpu.VMEM((2, ts, ts), F32), pltpu.VMEM((2, ts, ts), F32),
                        pltpu.VMEM((2, ts, ts), BF16), pltpu.VMEM((2, ts, ts), BF16),
                        pltpu.VMEM((2, 1, ts), F32), pltpu.VMEM((2, 1, ts), F32)],
        compiler_params=_cparams(("parallel", "parallel", "arbitrary")),
        name="attn",
    )(scal, q, k, vt, qf, kf, g_subln.reshape(HEAD_W, 1))


SCAN_LANES = 512


def _s5_kernel(u_ref, pm_ref, pmt_ref, bb_ref, cc_ref, a_ref, at_ref, apow_ref, y_ref,
               xs_ref, xb_ref, carry_ref, *, tc, reverse):
    tc8 = tc // 8
    c = pl.program_id(1)

    @pl.when(c == 0)
    def _():
        carry_ref[...] = jnp.zeros_like(carry_ref)

    up = jnp.dot(pm_ref[...], u_ref[0].astype(BF16), preferred_element_type=F32).astype(BF16)
    xs_ref[...] = jnp.dot(up, bb_ref[...], preferred_element_type=F32)

    for cb in range(N_STATE // SCAN_LANES):
        lo = cb * SCAN_LANES
        re_sl = pl.ds(lo, SCAN_LANES)
        im_sl = pl.ds(N_STATE + lo, SCAN_LANES)
        ar = jnp.broadcast_to(a_ref[0:1, lo:lo + SCAN_LANES], (8, SCAN_LANES))
        ai = jnp.broadcast_to(a_ref[1:2, lo:lo + SCAN_LANES], (8, SCAN_LANES))

        def step(i, carry):
            xr, xi = carry
            r = (tc8 - 1 - i) if reverse else i
            rows = pl.ds(pl.multiple_of(r * 8, 8), 8)
            nr = ar * xr - ai * xi + xs_ref[rows, re_sl]
            ni = ar * xi + ai * xr + xs_ref[rows, im_sl]
            xs_ref[rows, re_sl] = nr
            xs_ref[rows, im_sl] = ni
            return nr, ni

        z = jnp.zeros((8, SCAN_LANES), F32)
        lax.fori_loop(0, tc8, step, (z, z), unroll=4)

    last = 0 if reverse else (tc8 - 1) * 8
    e_re = xs_ref[last:last + 8, 0:N_STATE]
    e_im = xs_ref[last:last + 8, N_STATE:2 * N_STATE]
    atr = at_ref[0:1, :]
    ati = at_ref[1:2, :]
    cr = carry_ref[0:1, :]
    ci = carry_ref[1:2, :]
    cin_r = [None] * 8
    cin_i = [None] * 8
    order = range(7, -1, -1) if reverse else range(8)
    for j in order:
        cin_r[j] = cr
        cin_i[j] = ci
        nr = atr * cr - ati * ci + e_re[j:j + 1]
        ni = atr * ci + ati * cr + e_im[j:j + 1]
        cr, ci = nr, ni
    carry_ref[0:1, :] = cr
    carry_ref[1:2, :] = ci
    cin_r = jnp.concatenate(cin_r + cin_r, axis=0)
    cin_i = jnp.concatenate(cin_i + cin_i, axis=0)

    def fix(i, _):
        rows = pl.ds(pl.multiple_of(i * 16, 16), 16)
        pr = apow_ref[rows, 0:N_STATE]
        pi = apow_ref[rows, N_STATE:2 * N_STATE]
        xr = xs_ref[rows, 0:N_STATE] + pr * cin_r - pi * cin_i
        xi = xs_ref[rows, N_STATE:2 * N_STATE] + pr * cin_i + pi * cin_r
        xb_ref[rows, 0:N_STATE] = xr.astype(BF16)
        xb_ref[rows, N_STATE:2 * N_STATE] = xi.astype(BF16)
        return 0

    lax.fori_loop(0, tc // 16, fix, 0)

    yp = jnp.dot(xb_ref[...], cc_ref[...], preferred_element_type=F32)
    hi = yp.astype(BF16)
    lo_part = (yp - hi.astype(F32)).astype(BF16)
    y_ref[0] = (jnp.dot(pmt_ref[...], hi, preferred_element_type=F32)
                + jnp.dot(pmt_ref[...], lo_part, preferred_element_type=F32))


def _s5_direction(u, pm, pmt, bb, cc, a, at, apow, tc, reverse):
    b, l, _ = u.shape
    nc = l // tc
    cmap = (lambda bi, c: (bi, nc - 1 - c, 0)) if reverse else (lambda bi, c: (bi, c, 0))
    const = lambda bi, c: (0, 0)
    kernel = functools.partial(_s5_kernel, tc=tc, reverse=reverse)
    return pl.pallas_call(
        kernel,
        grid=(b, nc),
        in_specs=[pl.BlockSpec((1, tc, SSM_WIDTH), cmap),
                  pl.BlockSpec(pm.shape, const), pl.BlockSpec(pmt.shape, const),
                  pl.BlockSpec(bb.shape, const), pl.BlockSpec(cc.shape, const),
                  pl.BlockSpec(a.shape, const), pl.BlockSpec(at.shape, const),
                  pl.BlockSpec(apow.shape, const)],
        out_specs=pl.BlockSpec((1, tc, SSM_WIDTH), cmap),
        out_shape=jax.ShapeDtypeStruct((b, l, SSM_WIDTH), F32),
        scratch_shapes=[pltpu.VMEM((tc, 2 * N_STATE), F32), pltpu.VMEM((tc, 2 * N_STATE), BF16),
                        pltpu.VMEM((2, N_STATE), F32)],
        compiler_params=_cparams(("parallel", "arbitrary")),
        name="s5_bwd" if reverse else "s5_fwd",
    )(u, pm, pmt, bb, cc, a, at, apow)


def _s5_params(lam_re, lam_im, log_dt, b_re, b_im, c_re, c_im, tc, reverse):
    tc8 = tc // 8
    dt = jnp.exp(log_dt)[:, None]
    mag = jnp.exp(lam_re * dt)
    ab_re = mag * jnp.cos(lam_im * dt)
    ab_im = mag * jnp.sin(lam_im * dt)
    den = lam_re * lam_re + lam_im * lam_im
    nr, ni = ab_re - 1.0, ab_im
    co_re = (nr * lam_re + ni * lam_im) / den
    co_im = (ni * lam_re - nr * lam_im) / den
    bb_re = co_re[..., None] * b_re - co_im[..., None] * b_im
    bb_im = co_re[..., None] * b_im + co_im[..., None] * b_re
    eye = jnp.eye(N_GROUPS, dtype=F32)
    blk_in = lambda w: jnp.einsum('gph,gk->ghkp', w, eye).reshape(SSM_WIDTH, N_STATE)
    bb = jnp.concatenate([blk_in(bb_re), blk_in(bb_im)], axis=1).astype(BF16)
    blk_out = lambda w: jnp.einsum('ghp,gk->gpkh', w, eye).reshape(N_STATE, SSM_WIDTH)
    cc = jnp.concatenate([blk_out(c_re), -blk_out(c_im)], axis=0).astype(BF16)
    a = jnp.stack([ab_re.reshape(-1), ab_im.reshape(-1)])
    n = jnp.arange(1, tc8 + 1, dtype=F32)[:, None, None]
    pmag = jnp.exp(n * (lam_re * dt)[None])
    ang = n * (lam_im * dt)[None]
    pw_re = (pmag * jnp.cos(ang)).reshape(tc8, N_STATE)
    pw_im = (pmag * jnp.sin(ang)).reshape(tc8, N_STATE)
    at = jnp.stack([pw_re[-1], pw_im[-1]])
    if reverse:
        pw_re, pw_im = pw_re[::-1], pw_im[::-1]
    apow = jnp.concatenate([jnp.repeat(pw_re, 8, axis=0), jnp.repeat(pw_im, 8, axis=0)], axis=1)
    return bb, cc, a, at, apow


def _perm_matrices(tc):
    tc8 = tc // 8
    i = np.arange(tc)
    src = (i % 8) * tc8 + i // 8
    pm = np.zeros((tc, tc), np.float32)
    pm[i, src] = 1.0
    return jnp.asarray(pm, BF16), jnp.asarray(pm.T, BF16)


def _mix_kernel(h_ref, a_ref, u_ref, yf_ref, yb_ref, d_ref, wglu_ref, bglu_ref, wout_ref,
                gffn_ref, wr_ref, br_ref, h1_ref, hn_ref, ti_ref, tg_ref):
    y = d_ref[...] * u_ref[...] + yf_ref[...] + yb_ref[...]
    y = jax.nn.gelu(y)
    z = jnp.dot(y.astype(BF16), wglu_ref[...], preferred_element_type=F32) + bglu_ref[...]
    s = y * jax.nn.sigmoid(z)
    h1 = (h_ref[...]
          + jnp.dot(a_ref[...], wout_ref[0:ATTN_WIDTH, :], preferred_element_type=F32)
          + jnp.dot(s.astype(BF16), wout_ref[ATTN_WIDTH:, :], preferred_element_type=F32))
    h1_ref[...] = h1
    hn = _rms(h1, gffn_ref[...])
    hn_ref[...] = hn
    logits = jnp.dot(hn, wr_ref[...], preferred_element_type=F32,
                     precision=lax.Precision.HIGHEST) + br_ref[...]
    lane = lax.broadcasted_iota(jnp.int32, logits.shape, 1)
    neg = jnp.float32(-jnp.inf)
    work = logits
    ti = jnp.zeros(logits.shape, jnp.int32)
    tv = jnp.full(logits.shape, neg, F32)
    for kk in range(TOP_K):
        mx = jnp.max(work, axis=-1, keepdims=True)
        idx = jnp.min(jnp.where(work == mx, lane, LANES), axis=-1, keepdims=True)
        ti = jnp.where(lane == kk, idx, ti)
        tv = jnp.where(lane == kk, mx, tv)
        work = jnp.where(lane == idx, neg, work)
    ex = jnp.exp(tv - jnp.max(tv, axis=-1, keepdims=True))
    ti_ref[...] = ti
    tg_ref[...] = ex / jnp.sum(ex, axis=-1, keepdims=True)


def _mix(h, a, u, yf, yb, d, wglu_bf, bglu, wout_bf, g_ffn, wr_pad, br_pad, tm):
    t = h.shape[0]
    row = lambda i: (i, 0)
    const = lambda i: (0, 0)
    full = lambda arr: pl.BlockSpec(arr.shape, const)
    return pl.pallas_call(
        _mix_kernel,
        grid=(t // tm,),
        in_specs=[pl.BlockSpec((tm, D_MODEL), row), pl.BlockSpec((tm, ATTN_WIDTH), row),
                  pl.BlockSpec((tm, SSM_WIDTH), row), pl.BlockSpec((tm, SSM_WIDTH), row),
                  pl.BlockSpec((tm, SSM_WIDTH), row),
                  full(d), full(wglu_bf), full(bglu), full(wout_bf), full(g_ffn), full(wr_pad),
                  full(br_pad)],
        out_specs=[pl.BlockSpec((tm, D_MODEL), row), pl.BlockSpec((tm, D_MODEL), row),
                   pl.BlockSpec((tm, LANES), row), pl.BlockSpec((tm, LANES), row)],
        out_shape=[jax.ShapeDtypeStruct((t, D_MODEL), F32), jax.ShapeDtypeStruct((t, D_MODEL), F32),
                   jax.ShapeDtypeStruct((t, LANES), jnp.int32), jax.ShapeDtypeStruct((t, LANES), F32)],
        compiler_params=_cparams(("parallel",)),
        name="mix",
    )(h, a, u, yf, yb, d, wglu_bf, bglu, wout_bf, g_ffn, wr_pad, br_pad)


ROWS_PER_ISSUE = 4

def _dispatch_kernel(dest_ref, hn_ref, xs_in_ref, xs_ref, sem, *, tm):
    del xs_in_ref

    def issue(g, _):
        for rr in range(ROWS_PER_ISSUE):
            r = g * ROWS_PER_ISSUE + rr
            for kk in range(TOP_K):
                pltpu.make_async_copy(hn_ref.at[pl.ds(r, 1), :],
                                      xs_ref.at[pl.ds(dest_ref[r * TOP_K + kk], 1), :], sem).start()
        return 0

    lax.fori_loop(0, tm // ROWS_PER_ISSUE, issue, 0)

    def drain(g, _):
        for _unused in range(ROWS_PER_ISSUE * TOP_K):
            pltpu.make_async_copy(hn_ref.at[pl.ds(0, 1), :], xs_ref.at[pl.ds(0, 1), :], sem).wait()
        return 0

    lax.fori_loop(0, tm // ROWS_PER_ISSUE, drain, 0)


def _dispatch(dest_flat, hn, xs_init, tm):
    t = hn.shape[0]
    kernel = functools.partial(_dispatch_kernel, tm=tm)
    return pl.pallas_call(
        kernel,
        grid=(t // tm,),
        in_specs=[pl.BlockSpec((tm * TOP_K,), lambda i: (i,), memory_space=pltpu.SMEM),
                  pl.BlockSpec((tm, D_MODEL), lambda i: (i, 0)),
                  pl.BlockSpec(memory_space=pl.ANY)],
        out_specs=pl.BlockSpec(memory_space=pl.ANY),
        out_shape=jax.ShapeDtypeStruct(xs_init.shape, F32),
        scratch_shapes=[pltpu.SemaphoreType.DMA(())],
        input_output_aliases={2: 0},
        compiler_params=_cparams(("arbitrary",)),
        name="dispatch",
    )(dest_flat, hn, xs_init)


def _experts_kernel(be_ref, x_ref, wg_ref, bg_ref, wu_ref, bu_ref, wd_ref, bd_ref, y_ref):
    del be_ref
    x = x_ref[...].astype(BF16)
    gt = jnp.minimum(jnp.dot(x, wg_ref[0], preferred_element_type=F32) + bg_ref[0], SWIGLU_LIMIT)
    up = jnp.clip(jnp.dot(x, wu_ref[0], preferred_element_type=F32) + bu_ref[0],
                  -SWIGLU_LIMIT, SWIGLU_LIMIT)
    hh = (up + 1.0) * (gt * jax.nn.sigmoid(SWIGLU_ALPHA * gt))
    y_ref[...] = jnp.dot(hh.astype(BF16), wd_ref[0], preferred_element_type=F32) + bd_ref[0]


def _experts(blk_expert, xs, wg, bg, wu, bu, wd, bd, eb):
    n_pad = xs.shape[0]
    row = lambda i, be: (i, 0)
    wmap = lambda i, be: (be[i], 0, 0)
    wspec = pl.BlockSpec((1, D_MODEL, D_MODEL), wmap)
    bspec = pl.BlockSpec((1, 1, D_MODEL), wmap)
    grid_spec = pltpu.PrefetchScalarGridSpec(
        num_scalar_prefetch=1,
        grid=(n_pad // eb,),
        in_specs=[pl.BlockSpec((eb, D_MODEL), row), wspec, bspec, wspec, bspec, wspec, bspec],
        out_specs=pl.BlockSpec((eb, D_MODEL), row),
    )
    return pl.pallas_call(
        _experts_kernel,
        grid_spec=grid_spec,
        out_shape=jax.ShapeDtypeStruct((n_pad, D_MODEL), F32),
        compiler_params=_cparams(("arbitrary",)),
        name="experts",
    )(blk_expert, xs, wg, bg, wu, bu, wd, bd)


def _tail_kernel(dest_ref, h1_ref, tg_ref, p_ref, ys_ref, gple_ref, wpg_ref, wpp_ref, gfin_ref,
                 o_ref, buf_ref, sem, *, tm):
    def issue(g, _):
        for rr in range(ROWS_PER_ISSUE):
            r = g * ROWS_PER_ISSUE + rr
            for kk in range(TOP_K):
                pltpu.make_async_copy(ys_ref.at[pl.ds(dest_ref[r * TOP_K + kk], 1), :],
                                      buf_ref.at[kk, pl.ds(r, 1), :], sem).start()
        return 0

    lax.fori_loop(0, tm // ROWS_PER_ISSUE, issue, 0)

    def drain(g, _):
        for _unused in range(ROWS_PER_ISSUE * TOP_K):
            pltpu.make_async_copy(ys_ref.at[pl.ds(0, 1), :], buf_ref.at[0, pl.ds(0, 1), :], sem).wait()
        return 0

    lax.fori_loop(0, tm // ROWS_PER_ISSUE, drain, 0)

    tg = tg_ref[...]
    h2 = h1_ref[...]
    for kk in range(TOP_K):
        h2 = h2 + tg[:, kk:kk + 1] * buf_ref[kk]
    gate = jax.nn.sigmoid(jnp.dot(_rms(h2, gple_ref[...]).astype(BF16), wpg_ref[...],
                                  preferred_element_type=F32))
    proj = jnp.dot(p_ref[...].astype(BF16), wpp_ref[...], preferred_element_type=F32)
    o_ref[...] = _rms(h2 + gate * proj, gfin_ref[...])


def _tail(dest_flat, h1, tg, p, ys, g_ple, wpg_bf, wpp_bf, g_final, tm):
    t = h1.shape[0]
    row = lambda i: (i, 0)
    const = lambda i: (0, 0)
    full = lambda arr: pl.BlockSpec(arr.shape, const)
    kernel = functools.partial(_tail_kernel, tm=tm)
    return pl.pallas_call(
        kernel,
        grid=(t // tm,),
        in_specs=[pl.BlockSpec((tm * TOP_K,), lambda i: (i,), memory_space=pltpu.SMEM),
                  pl.BlockSpec((tm, D_MODEL), row), pl.BlockSpec((tm, LANES), row),
                  pl.BlockSpec((tm, PLE_DIM), row),
                  pl.BlockSpec(memory_space=pl.ANY),
                  full(g_ple), full(wpg_bf), full(wpp_bf), full(g_final)],
        out_specs=pl.BlockSpec((tm, D_MODEL), row),
        out_shape=jax.ShapeDtypeStruct((t, D_MODEL), F32),
        scratch_shapes=[pltpu.VMEM((TOP_K, tm, D_MODEL), F32), pltpu.SemaphoreType.DMA(())],
        compiler_params=_cparams(("arbitrary",)),
        name="tail",
    )(dest_flat, h1, tg, p, ys, g_ple, wpg_bf, wpp_bf, g_final)


def _routing(top_i, eb):
    t = top_i.shape[0]
    onehot = (top_i[:, :, None] == jnp.arange(N_EXPERTS, dtype=jnp.int32)).astype(jnp.int32)
    per_tok = onehot.sum(axis=1)
    incl = jnp.cumsum(per_tok, axis=0)
    counts = incl[-1]
    excl = incl - per_tok
    padded = ((counts + eb - 1) // eb) * eb
    pends = jnp.cumsum(padded)
    pstarts = pends - padded
    rank = jnp.take_along_axis(excl + pstarts[None, :], top_i, axis=1)
    n_blocks = -(-(t * TOP_K + N_EXPERTS * (eb - 1)) // eb)
    blk_start = jnp.arange(n_blocks, dtype=jnp.int32) * eb
    blk_expert = jnp.minimum(jnp.searchsorted(pends, blk_start, side='right'), N_EXPERTS - 1)
    return rank.reshape(-1).astype(jnp.int32), blk_expert.astype(jnp.int32), n_blocks * eb


def _pick(n, prefs):
    for c in prefs:
        if n % c == 0:
            return c
    raise ValueError(f"no tile for {n}")


def _trunk(x, p, w, lambda_init):
    b, l, _ = x.shape
    t = b * l
    tm = TOKEN_TILE
    assert l % tm == 0
    tc = _pick(l, (256, 128))
    eb = 256

    q, k, vt, u = _inproj(x.reshape(t, D_MODEL), w['g_mix'], w['w_qku'], w['w_vt'], tm)
    a = _attention(q.reshape(b, l, -1), k.reshape(b, l, -1), vt, w['attn_scal'], w['attn_qf'],
                   w['attn_kf'], w['g_subln'], lambda_init, tm)
    u3 = u.reshape(b, l, SSM_WIDTH)
    pm, pmt = _perm_matrices(tc)
    ys = [_s5_direction(u3, pm, pmt, *w['s5'][dr], tc, reverse=(dr == 1)) for dr in range(2)]
    h1, hn, ti, tg = _mix(x.reshape(t, D_MODEL), a.reshape(t, ATTN_WIDTH), u,
                          ys[0].reshape(t, SSM_WIDTH), ys[1].reshape(t, SSM_WIDTH),
                          w['ssm_d'], w['w_glu'], w['b_glu'], w['w_out'], w['g_ffn'],
                          w['w_router'], w['b_router'], tm)
    dest, blk_expert, n_pad = _routing(ti[:, :TOP_K], eb)
    xs = _dispatch(dest, hn, jnp.zeros((n_pad, D_MODEL), F32), tm // 2)
    ye = _experts(blk_expert, xs, w['w_gate'], w['b_gate'], w['w_up'], w['b_up'],
                  w['w_down'], w['b_down'], eb)
    out = _tail(dest, h1, tg, p.reshape(t, PLE_DIM), ye, w['g_ple'], w['w_ple_gate'],
                w['w_ple_proj'], w['g_final'], tm // 2)
    return out.reshape(b, l, D_MODEL)


def _prepare(i, g_mix, w_in, lambda_q1, lambda_k1, lambda_q2, lambda_k2, g_subln, ssm_lambda_re,
             ssm_lambda_im, ssm_log_dt, ssm_b_re, ssm_b_im, ssm_c_re, ssm_c_im, ssm_d, w_glu, b_glu,
             w_out, g_ffn, w_router, b_router, w_gate, b_gate, w_up, b_up, w_down, b_down, g_ple,
             w_ple_gate, w_ple_proj, g_final, tcs):
    lambda_init = 0.8 - 0.6 * math.exp(-0.3 * i)
    lam = (jnp.exp(jnp.sum(lambda_q1[i] * lambda_k1[i]))
           - jnp.exp(jnp.sum(lambda_q2[i] * lambda_k2[i])) + lambda_init)
    slopes = jnp.exp2(-8.0 * jnp.arange(1, N_HEADS + 1, dtype=F32) / N_HEADS)
    bslopes = slopes * LOG2E
    parts, rest = [], bslopes
    for _ in range(N_BIAS_PARTS):
        part = rest.astype(BF16).astype(F32)
        parts.append(part)
        rest = rest - part
    attn_qf, attn_kf = _attn_features(jnp.stack(parts, axis=1), TOKEN_TILE)
    w_qku = jnp.concatenate([w_in[i][:, :2 * ATTN_WIDTH], w_in[i][:, 3 * ATTN_WIDTH:]], axis=1)
    row = lambda vec: vec.reshape(1, -1).astype(F32)
    w = {
        'g_mix': g_mix[i], 'w_qku': w_qku.astype(BF16),
        'w_vt': w_in[i][:, 2 * ATTN_WIDTH:3 * ATTN_WIDTH].T.astype(BF16),
        'attn_scal': jnp.concatenate([lam.reshape(1), bslopes]).astype(F32),
        'attn_qf': attn_qf, 'attn_kf': attn_kf,
        'g_subln': g_subln[i],
        'ssm_d': row(ssm_d[i]), 'w_glu': w_glu[i].astype(BF16), 'b_glu': row(b_glu[i]),
        'w_out': w_out[i].astype(BF16), 'g_ffn': row(g_ffn[i]),
        'w_router': jnp.pad(w_router[i], ((0, 0), (0, LANES - N_EXPERTS))),
        'b_router': jnp.pad(row(b_router[i]), ((0, 0), (0, LANES - N_EXPERTS)),
                            constant_values=-jnp.inf),
        'w_gate': w_gate[i].astype(BF16), 'b_gate': b_gate[i].reshape(N_EXPERTS, 1, D_MODEL),
        'w_up': w_up[i].astype(BF16), 'b_up': b_up[i].reshape(N_EXPERTS, 1, D_MODEL),
        'w_down': w_down[i].astype(BF16), 'b_down': b_down[i].reshape(N_EXPERTS, 1, D_MODEL),
        'g_ple': row(g_ple[i]), 'w_ple_gate': w_ple_gate[i].astype(BF16),
        'w_ple_proj': w_ple_proj[i].astype(BF16), 'g_final': row(g_final),
    }
    w['s5'] = {tc: [_s5_params(ssm_lambda_re[i, dr], ssm_lambda_im[i, dr], ssm_log_dt[i, dr],
                               ssm_b_re[i, dr], ssm_b_im[i, dr], ssm_c_re[i, dr], ssm_c_im[i, dr],
                               tc, reverse=(dr == 1)) for dr in range(2)] for tc in tcs}
    return w, lambda_init


def kernel(x_prompt, x_sample, p_prompt, p_sample, g_mix, w_in, lambda_q1, lambda_k1, lambda_q2, lambda_k2, g_subln, ssm_lambda_re, ssm_lambda_im, ssm_log_dt, ssm_b_re, ssm_b_im, ssm_c_re, ssm_c_im, ssm_d, w_glu, b_glu, w_out, g_ffn, w_router, b_router, w_gate, b_gate, w_up, b_up, w_down, b_down, g_ple, w_ple_gate, w_ple_proj, g_final):
    assert w_in.shape[0] == 1, "single-layer trunk"
    tcs = {_pick(x.shape[1], (256, 128)) for x in (x_prompt, x_sample)}
    w, lambda_init = _prepare(0, g_mix, w_in, lambda_q1, lambda_k1, lambda_q2, lambda_k2, g_subln,
                              ssm_lambda_re, ssm_lambda_im, ssm_log_dt, ssm_b_re, ssm_b_im, ssm_c_re,
                              ssm_c_im, ssm_d, w_glu, b_glu, w_out, g_ffn, w_router, b_router, w_gate,
                              b_gate, w_up, b_up, w_down, b_down, g_ple, w_ple_gate, w_ple_proj,
                              g_final, tcs)
    outs = []
    for x, p in ((x_prompt, p_prompt), (x_sample, p_sample)):
        tc = _pick(x.shape[1], (256, 128))
        wt = dict(w, s5=w['s5'][tc])
        outs.append(_trunk(x, p[0], wt, lambda_init))
    return tuple(outs)
```

```python
import functools
import math

import jax
import jax.numpy as jnp
import numpy as np
from jax import lax
from jax.experimental import pallas as pl
from jax.experimental.pallas import tpu as pltpu

F32 = jnp.float32
BF16 = jnp.bfloat16

D_MODEL = 1024
PLE_DIM = 256
ATTN_WIDTH = 512
SSM_WIDTH = 512
HEAD_W = 128
DIFF_HEAD_DIM = 64
N_HEADS = 4
SSM_GROUP = 16
N_GROUPS = 32
SSM_STATE = 64
N_STATE = N_GROUPS * SSM_STATE
N_EXPERTS = 32
TOP_K = 4
SWIGLU_LIMIT = 7.0
SWIGLU_ALPHA = 1.702
EPS = 1e-6
LANES = 128
TOKEN_TILE = 512

VMEM_LIMIT = 56 * 1024 * 1024


def _cparams(sem):
    return pltpu.CompilerParams(dimension_semantics=sem, vmem_limit_bytes=VMEM_LIMIT)


def _rms(x, g):
    return x * lax.rsqrt(jnp.mean(x * x, axis=-1, keepdims=True) + EPS) * g


LOG2E = 1.4426950408889634
NT_DIMS = (((1,), (1,)), ((), ()))


def _inproj_kernel(x_ref, g_ref, wqku_ref, wvt_ref, seg_ref, q_ref, k_ref, vt_ref, u_ref, nrm_ref):
    xn = _rms(x_ref[...], g_ref[...]).astype(BF16)
    proj = jnp.dot(xn, wqku_ref[...], preferred_element_type=F32)
    scale = LOG2E / math.sqrt(DIFF_HEAD_DIM)
    qk = jnp.concatenate([proj[:, :ATTN_WIDTH] * scale, proj[:, ATTN_WIDTH:2 * ATTN_WIDTH]], axis=1)
    q_ref[...] = qk[:, :ATTN_WIDTH].astype(BF16)
    k_ref[...] = qk[:, ATTN_WIDTH:].astype(BF16)
    u_ref[...] = proj[:, 2 * ATTN_WIDTH:]
    nsq = jnp.dot((qk * qk).astype(BF16), seg_ref[...], preferred_element_type=F32)
    nrm_ref[0] = jnp.max(nsq.reshape(nsq.shape[0] // 8, 8, LANES), axis=0)
    vt_ref[0] = lax.dot_general(wvt_ref[...], xn, NT_DIMS, preferred_element_type=F32).astype(BF16)


def _inproj(x, g_mix, w_qku_bf, w_vt_bf, tm):
    t = x.shape[0]
    row = lambda i: (i, 0)
    const = lambda i: (0, 0)
    col = np.arange(2 * ATTN_WIDTH)
    seg = np.zeros((2 * ATTN_WIDTH, LANES), np.float32)
    seg[col, col // DIFF_HEAD_DIM] = 1.0
    seg = jnp.asarray(seg, BF16)
    return pl.pallas_call(
        _inproj_kernel,
        grid=(t // tm,),
        in_specs=[pl.BlockSpec((tm, D_MODEL), row),
                  pl.BlockSpec((1, D_MODEL), const),
                  pl.BlockSpec(w_qku_bf.shape, const),
                  pl.BlockSpec(w_vt_bf.shape, const),
                  pl.BlockSpec(seg.shape, const)],
        out_specs=[pl.BlockSpec((tm, ATTN_WIDTH), row), pl.BlockSpec((tm, ATTN_WIDTH), row),
                   pl.BlockSpec((1, ATTN_WIDTH, tm), lambda i: (i, 0, 0)),
                   pl.BlockSpec((tm, SSM_WIDTH), row),
                   pl.BlockSpec((1, 8, LANES), lambda i: (i, 0, 0))],
        out_shape=[jax.ShapeDtypeStruct((t, ATTN_WIDTH), BF16), jax.ShapeDtypeStruct((t, ATTN_WIDTH), BF16),
                   jax.ShapeDtypeStruct((t // tm, ATTN_WIDTH, tm), BF16),
                   jax.ShapeDtypeStruct((t, SSM_WIDTH), F32),
                   jax.ShapeDtypeStruct((t // tm, 8, LANES), F32)],
        compiler_params=_cparams(("parallel",)),
        name="inproj",
    )(x, g_mix.reshape(1, D_MODEL), w_qku_bf, w_vt_bf, seg)


ONES_ROWS = 16
ACC_ROWS = HEAD_W + ONES_ROWS
POS_SPLIT = 16
N_BIAS_PARTS = 3
SKIP_BITS = 130.0
NORM_INFLATE = 1.0 + 2.0 ** -5


def _attn_kernel(sc_ref, ub_ref, q_ref, k_ref, vt_ref, qf_ref, kf_ref, g_ref, o_ref,
                 qa_ref, m_ref, acc_ref, sa_ref, sb_ref, pa_ref, pb_ref, ala_ref, alb_ref,
                 *, ts, out_scale):
    bi = pl.program_id(0)
    h = pl.program_id(1)
    qi = pl.program_id(2)
    nsub = k_ref.shape[1] // ts
    bslope = sc_ref[1 + h]

    q = q_ref[0]
    lane = lax.broadcasted_iota(jnp.int32, q.shape, 1)
    zero = jnp.zeros_like(q)
    qf = qf_ref[0]
    qa_ref[0] = jnp.concatenate([jnp.where(lane < DIFF_HEAD_DIM, q, zero), qf], axis=1)
    qa_ref[1] = jnp.concatenate([jnp.where(lane >= DIFF_HEAD_DIM, q, zero), qf], axis=1)
    ones = jnp.ones((ONES_ROWS, ts), BF16)

    def keys(j, variant):
        rows = pl.ds(pl.multiple_of(j * ts, ts), ts)
        return jnp.concatenate([k_ref[0, rows, :], kf_ref[0, variant]], axis=1)

    def values(j):
        return jnp.concatenate([vt_ref[j], ones], axis=0)

    kpos = lax.broadcasted_iota(jnp.int32, (ts, ts), 0)
    qpos = lax.broadcasted_iota(jnp.int32, (ts, ts), 1)
    bias = jnp.abs(qpos - kpos).astype(F32) * (-bslope)
    ka = keys(qi, 2)
    vta = values(qi)
    for st in range(2):
        s = lax.dot_general(ka, qa_ref[st], NT_DIMS, preferred_element_type=F32) + bias
        m_new = jnp.max(s, axis=0, keepdims=True)
        p = jnp.exp2(s - m_new).astype(BF16)
        acc_ref[st] = jnp.dot(vta, p, preferred_element_type=F32)
        m_ref[st] = m_new

    ub = ub_ref[(bi * N_HEADS + h) * nsub + qi]
    slack = ub - jnp.min(m_ref[...]) + SKIP_BITS
    x = (slack * sc_ref[1 + N_HEADS + h] - 1.0) * (1.0 / ts)
    w = jnp.where(x < 0.0, 0, jnp.minimum(x, float(nsub)).astype(jnp.int32) + 1)
    lo = jnp.maximum(qi - w, 0)
    n_off = jnp.minimum(qi + w, nsub - 1) - lo

    def sub_index(n):
        j = lo + jnp.clip(n, 0, jnp.maximum(n_off - 1, 0))
        return j + (j >= qi).astype(jnp.int32)

    def qk(n, s_ref):
        j = sub_index(n)
        ka = keys(j, jnp.where(j < qi, 0, 1))
        for st in range(2):
            s_ref[st] = lax.dot_general(ka, qa_ref[st], NT_DIMS, preferred_element_type=F32)

    def softmax(n, s_ref, p_ref, al_ref):
        j = sub_index(n)
        c = -bslope * (jnp.abs(qi - j) * ts).astype(F32)
        for st in range(2):
            m_old = m_ref[st]
            m_new = jnp.maximum(m_old, jnp.max(s_ref[st], axis=0, keepdims=True) + c)
            al_ref[st] = jnp.exp2(m_old - m_new)
            p_ref[st] = jnp.exp2(s_ref[st] - (m_new - c)).astype(BF16)
            m_ref[st] = m_new

    def pv(n, p_ref, al_ref):
        vta = values(sub_index(n))
        for st in range(2):
            acc_ref[st] = al_ref[st] * acc_ref[st] + jnp.dot(vta, p_ref[st], preferred_element_type=F32)

    @pl.when(n_off > 0)
    def _():
        for p_ref, al_ref in ((pa_ref, ala_ref), (pb_ref, alb_ref)):
            p_ref[...] = jnp.zeros_like(p_ref)
            al_ref[...] = jnp.ones_like(al_ref)
        qk(0, sa_ref)

        def pair(g, carry):
            n = 2 * g
            pv(n - 2, pa_ref, ala_ref)
            qk(n + 1, sb_ref)
            softmax(n, sa_ref, pa_ref, ala_ref)
            pv(n - 1, pb_ref, alb_ref)
            qk(n + 2, sa_ref)
            softmax(n + 1, sb_ref, pb_ref, alb_ref)
            return carry

        n_pairs = n_off // 2
        odd = n_off % 2 == 1
        lax.fori_loop(0, n_pairs, pair, 0)
        pv(2 * n_pairs - 2, pa_ref, ala_ref)

        @pl.when(odd)
        def _():
            softmax(n_off - 1, sa_ref, pa_ref, ala_ref)

        pv(2 * n_pairs - 1, pb_ref, alb_ref)

        @pl.when(odd)
        def _():
            pv(n_off - 1, pa_ref, ala_ref)

    lam = sc_ref[0]
    a1 = acc_ref[0]
    a2 = acc_ref[1]
    out_t = (a1[:HEAD_W] / a1[HEAD_W:HEAD_W + 1] - lam * (a2[:HEAD_W] / a2[HEAD_W:HEAD_W + 1]))
    ms = jnp.mean(out_t * out_t, axis=0, keepdims=True)
    out_t = out_t * lax.rsqrt(ms + EPS) * (g_ref[...] * out_scale)
    o_ref[0] = out_t.T.astype(o_ref.dtype)


def _attn_features(bparts, ts):
    pos = jnp.arange(ts, dtype=jnp.int32)
    hi = ((pos // POS_SPLIT) * POS_SPLIT).astype(F32)[None, :, None]
    lo = (pos % POS_SPLIT).astype(F32)[None, :, None]
    bp = bparts[:, None, :]
    one = jnp.ones_like(bp)
    qf = jnp.stack([bp + 0 * hi, bp + 0 * hi, hi * one, lo * one], axis=-1)
    kf = jnp.stack([hi * one, lo * one, -bp + 0 * hi, -bp + 0 * hi], axis=-1)
    pad = lambda f: jnp.pad(f.reshape(N_HEADS, ts, 4 * N_BIAS_PARTS),
                            ((0, 0), (0, 0), (0, HEAD_W - 4 * N_BIAS_PARTS)))
    qf, kf = pad(qf), pad(kf)
    kf = jnp.stack([kf, -kf, jnp.zeros_like(kf)], axis=1)
    return qf.astype(BF16), kf.astype(BF16)


def _score_bounds(nrm, b, nsub):
    nrm = jnp.max(nrm, axis=1)[:, :2 * N_HEADS * 2].reshape(b, nsub, 2, N_HEADS, 2)
    q2 = nrm[:, :, 0]
    k2 = jnp.max(nrm[:, :, 1], axis=1)
    ub = jnp.max(jnp.sqrt(q2 * k2[:, None]), axis=-1) * NORM_INFLATE
    return jnp.transpose(ub, (0, 2, 1)).reshape(-1)


def _attention(q, k, vt, scal, ub, qf, kf, g_subln, lambda_init, ts):
    b, l, _ = q.shape
    nsub = l // ts
    kernel = functools.partial(_attn_kernel, ts=ts, out_scale=1.0 - lambda_init)
    return pl.pallas_call(
        kernel,
        grid=(b, N_HEADS, nsub),
        in_specs=[pl.BlockSpec(memory_space=pltpu.SMEM),
                  pl.BlockSpec(memory_space=pltpu.SMEM),
                  pl.BlockSpec((1, ts, HEAD_W), lambda bi, h, qi: (bi, qi, h)),
                  pl.BlockSpec((1, l, HEAD_W), lambda bi, h, qi: (bi, 0, h)),
                  pl.BlockSpec((nsub, HEAD_W, ts), lambda bi, h, qi: (bi, h, 0)),
                  pl.BlockSpec((1, ts, HEAD_W), lambda bi, h, qi: (h, 0, 0)),
                  pl.BlockSpec((1, 3, ts, HEAD_W), lambda bi, h, qi: (h, 0, 0, 0)),
                  pl.BlockSpec((HEAD_W, 1), lambda bi, h, qi: (0, 0))],
        out_specs=pl.BlockSpec((1, ts, HEAD_W), lambda bi, h, qi: (bi, qi, h)),
        out_shape=jax.ShapeDtypeStruct((b, l, ATTN_WIDTH), BF16),
        scratch_shapes=[pltpu.VMEM((2, ts, 2 * HEAD_W), BF16),
                        pltpu.VMEM((2, 1, ts), F32),
                        pltpu.VMEM((2, ACC_ROWS, ts), F32),
                        pltpu.VMEM((2, ts, ts), F32), pltpu.VMEM((2, ts, ts), F32),
                        pltpu.VMEM((2, ts, ts), BF16), pltpu.VMEM((2, ts, ts), BF16),
                        pltpu.VMEM((2, 1, ts), F32), pltpu.VMEM((2, 1, ts), F32)],
        compiler_params=_cparams(("parallel", "parallel", "arbitrary")),
        name="attn",
    )(scal, ub, q, k, vt, qf, kf, g_subln.reshape(HEAD_W, 1))


SCAN_LANES = 512


def _s5_kernel(u_ref, pm_ref, pmt_ref, bb_ref, cc_ref, a_ref, at_ref, apow_ref, y_ref,
               xs_ref, xb_ref, carry_ref, *, tc, reverse):
    tc8 = tc // 8
    c = pl.program_id(1)

    @pl.when(c == 0)
    def _():
        carry_ref[...] = jnp.zeros_like(carry_ref)

    up = jnp.dot(pm_ref[...], u_ref[0].astype(BF16), preferred_element_type=F32).astype(BF16)
    xs_ref[...] = jnp.dot(up, bb_ref[...], preferred_element_type=F32)

    for cb in range(N_STATE // SCAN_LANES):
        lo = cb * SCAN_LANES
        re_sl = pl.ds(lo, SCAN_LANES)
        im_sl = pl.ds(N_STATE + lo, SCAN_LANES)
        ar = jnp.broadcast_to(a_ref[0:1, lo:lo + SCAN_LANES], (8, SCAN_LANES))
        ai = jnp.broadcast_to(a_ref[1:2, lo:lo + SCAN_LANES], (8, SCAN_LANES))

        def step(i, carry):
            xr, xi = carry
            r = (tc8 - 1 - i) if reverse else i
            rows = pl.ds(pl.multiple_of(r * 8, 8), 8)
            nr = ar * xr - ai * xi + xs_ref[rows, re_sl]
            ni = ar * xi + ai * xr + xs_ref[rows, im_sl]
            xs_ref[rows, re_sl] = nr
            xs_ref[rows, im_sl] = ni
            return nr, ni

        z = jnp.zeros((8, SCAN_LANES), F32)
        lax.fori_loop(0, tc8, step, (z, z), unroll=4)

    last = 0 if reverse else (tc8 - 1) * 8
    e_re = xs_ref[last:last + 8, 0:N_STATE]
    e_im = xs_ref[last:last + 8, N_STATE:2 * N_STATE]
    atr = at_ref[0:1, :]
    ati = at_ref[1:2, :]
    cr = carry_ref[0:1, :]
    ci = carry_ref[1:2, :]
    cin_r = [None] * 8
    cin_i = [None] * 8
    order = range(7, -1, -1) if reverse else range(8)
    for j in order:
        cin_r[j] = cr
        cin_i[j] = ci
        nr = atr * cr - ati * ci + e_re[j:j + 1]
        ni = atr * ci + ati * cr + e_im[j:j + 1]
        cr, ci = nr, ni
    carry_ref[0:1, :] = cr
    carry_ref[1:2, :] = ci
    cin_r = jnp.concatenate(cin_r + cin_r, axis=0)
    cin_i = jnp.concatenate(cin_i + cin_i, axis=0)

    def fix(i, _):
        rows = pl.ds(pl.multiple_of(i * 16, 16), 16)
        pr = apow_ref[rows, 0:N_STATE]
        pi = apow_ref[rows, N_STATE:2 * N_STATE]
        xr = xs_ref[rows, 0:N_STATE] + pr * cin_r - pi * cin_i
        xi = xs_ref[rows, N_STATE:2 * N_STATE] + pr * cin_i + pi * cin_r
        xb_ref[rows, 0:N_STATE] = xr.astype(BF16)
        xb_ref[rows, N_STATE:2 * N_STATE] = xi.astype(BF16)
        return 0

    lax.fori_loop(0, tc // 16, fix, 0)

    yp = jnp.dot(xb_ref[...], cc_ref[...], preferred_element_type=F32)
    hi = yp.astype(BF16)
    lo_part = (yp - hi.astype(F32)).astype(BF16)
    y_ref[0] = (jnp.dot(pmt_ref[...], hi, preferred_element_type=F32)
                + jnp.dot(pmt_ref[...], lo_part, preferred_element_type=F32))


def _s5_direction(u, pm, pmt, bb, cc, a, at, apow, tc, reverse):
    b, l, _ = u.shape
    nc = l // tc
    cmap = (lambda bi, c: (bi, nc - 1 - c, 0)) if reverse else (lambda bi, c: (bi, c, 0))
    const = lambda bi, c: (0, 0)
    kernel = functools.partial(_s5_kernel, tc=tc, reverse=reverse)
    return pl.pallas_call(
        kernel,
        grid=(b, nc),
        in_specs=[pl.BlockSpec((1, tc, SSM_WIDTH), cmap),
                  pl.BlockSpec(pm.shape, const), pl.BlockSpec(pmt.shape, const),
                  pl.BlockSpec(bb.shape, const), pl.BlockSpec(cc.shape, const),
                  pl.BlockSpec(a.shape, const), pl.BlockSpec(at.shape, const),
                  pl.BlockSpec(apow.shape, const)],
        out_specs=pl.BlockSpec((1, tc, SSM_WIDTH), cmap),
        out_shape=jax.ShapeDtypeStruct((b, l, SSM_WIDTH), F32),
        scratch_shapes=[pltpu.VMEM((tc, 2 * N_STATE), F32), pltpu.VMEM((tc, 2 * N_STATE), BF16),
                        pltpu.VMEM((2, N_STATE), F32)],
        compiler_params=_cparams(("parallel", "arbitrary")),
        name="s5_bwd" if reverse else "s5_fwd",
    )(u, pm, pmt, bb, cc, a, at, apow)


def _s5_params(lam_re, lam_im, log_dt, b_re, b_im, c_re, c_im, tc, reverse):
    tc8 = tc // 8
    dt = jnp.exp(log_dt)[:, None]
    mag = jnp.exp(lam_re * dt)
    ab_re = mag * jnp.cos(lam_im * dt)
    ab_im = mag * jnp.sin(lam_im * dt)
    den = lam_re * lam_re + lam_im * lam_im
    nr, ni = ab_re - 1.0, ab_im
    co_re = (nr * lam_re + ni * lam_im) / den
    co_im = (ni * lam_re - nr * lam_im) / den
    bb_re = co_re[..., None] * b_re - co_im[..., None] * b_im
    bb_im = co_re[..., None] * b_im + co_im[..., None] * b_re
    eye = jnp.eye(N_GROUPS, dtype=F32)
    blk_in = lambda w: jnp.einsum('gph,gk->ghkp', w, eye).reshape(SSM_WIDTH, N_STATE)
    bb = jnp.concatenate([blk_in(bb_re), blk_in(bb_im)], axis=1).astype(BF16)
    blk_out = lambda w: jnp.einsum('ghp,gk->gpkh', w, eye).reshape(N_STATE, SSM_WIDTH)
    cc = jnp.concatenate([blk_out(c_re), -blk_out(c_im)], axis=0).astype(BF16)
    a = jnp.stack([ab_re.reshape(-1), ab_im.reshape(-1)])
    n = jnp.arange(1, tc8 + 1, dtype=F32)[:, None, None]
    pmag = jnp.exp(n * (lam_re * dt)[None])
    ang = n * (lam_im * dt)[None]
    pw_re = (pmag * jnp.cos(ang)).reshape(tc8, N_STATE)
    pw_im = (pmag * jnp.sin(ang)).reshape(tc8, N_STATE)
    at = jnp.stack([pw_re[-1], pw_im[-1]])
    if reverse:
        pw_re, pw_im = pw_re[::-1], pw_im[::-1]
    apow = jnp.concatenate([jnp.repeat(pw_re, 8, axis=0), jnp.repeat(pw_im, 8, axis=0)], axis=1)
    return bb, cc, a, at, apow


def _perm_matrices(tc):
    tc8 = tc // 8
    i = np.arange(tc)
    src = (i % 8) * tc8 + i // 8
    pm = np.zeros((tc, tc), np.float32)
    pm[i, src] = 1.0
    return jnp.asarray(pm, BF16), jnp.asarray(pm.T, BF16)


def _mix_kernel(h_ref, a_ref, u_ref, yf_ref, yb_ref, d_ref, wglu_ref, bglu_ref, wout_ref,
                gffn_ref, wr_ref, br_ref, h1_ref, hn_ref, ti_ref, tg_ref):
    y = d_ref[...] * u_ref[...] + yf_ref[...] + yb_ref[...]
    y = jax.nn.gelu(y)
    z = jnp.dot(y.astype(BF16), wglu_ref[...], preferred_element_type=F32) + bglu_ref[...]
    s = y * jax.nn.sigmoid(z)
    h1 = (h_ref[...]
          + jnp.dot(a_ref[...], wout_ref[0:ATTN_WIDTH, :], preferred_element_type=F32)
          + jnp.dot(s.astype(BF16), wout_ref[ATTN_WIDTH:, :], preferred_element_type=F32))
    h1_ref[...] = h1
    hn = _rms(h1, gffn_ref[...])
    hn_ref[...] = hn
    logits = jnp.dot(hn, wr_ref[...], preferred_element_type=F32,
                     precision=lax.Precision.HIGHEST) + br_ref[...]
    lane = lax.broadcasted_iota(jnp.int32, logits.shape, 1)
    neg = jnp.float32(-jnp.inf)
    work = logits
    ti = jnp.zeros(logits.shape, jnp.int32)
    tv = jnp.full(logits.shape, neg, F32)
    for kk in range(TOP_K):
        mx = jnp.max(work, axis=-1, keepdims=True)
        idx = jnp.min(jnp.where(work == mx, lane, LANES), axis=-1, keepdims=True)
        ti = jnp.where(lane == kk, idx, ti)
        tv = jnp.where(lane == kk, mx, tv)
        work = jnp.where(lane == idx, neg, work)
    ex = jnp.exp(tv - jnp.max(tv, axis=-1, keepdims=True))
    ti_ref[...] = ti
    tg_ref[...] = ex / jnp.sum(ex, axis=-1, keepdims=True)


def _mix(h, a, u, yf, yb, d, wglu_bf, bglu, wout_bf, g_ffn, wr_pad, br_pad, tm):
    t = h.shape[0]
    row = lambda i: (i, 0)
    const = lambda i: (0, 0)
    full = lambda arr: pl.BlockSpec(arr.shape, const)
    return pl.pallas_call(
        _mix_kernel,
        grid=(t // tm,),
        in_specs=[pl.BlockSpec((tm, D_MODEL), row), pl.BlockSpec((tm, ATTN_WIDTH), row),
                  pl.BlockSpec((tm, SSM_WIDTH), row), pl.BlockSpec((tm, SSM_WIDTH), row),
                  pl.BlockSpec((tm, SSM_WIDTH), row),
                  full(d), full(wglu_bf), full(bglu), full(wout_bf), full(g_ffn), full(wr_pad),
                  full(br_pad)],
        out_specs=[pl.BlockSpec((tm, D_MODEL), row), pl.BlockSpec((tm, D_MODEL), row),
                   pl.BlockSpec((tm, LANES), row), pl.BlockSpec((tm, LANES), row)],
        out_shape=[jax.ShapeDtypeStruct((t, D_MODEL), F32), jax.ShapeDtypeStruct((t, D_MODEL), F32),
                   jax.ShapeDtypeStruct((t, LANES), jnp.int32), jax.ShapeDtypeStruct((t, LANES), F32)],
        compiler_params=_cparams(("parallel",)),
        name="mix",
    )(h, a, u, yf, yb, d, wglu_bf, bglu, wout_bf, g_ffn, wr_pad, br_pad)


ROWS_PER_ISSUE = 4

def _dispatch_kernel(dest_ref, hn_ref, xs_in_ref, xs_ref, sem, *, tm):
    del xs_in_ref

    def issue(g, _):
        for rr in range(ROWS_PER_ISSUE):
            r = g * ROWS_PER_ISSUE + rr
            for kk in range(TOP_K):
                pltpu.make_async_copy(hn_ref.at[pl.ds(r, 1), :],
                                      xs_ref.at[pl.ds(dest_ref[r * TOP_K + kk], 1), :], sem).start()
        return 0

    lax.fori_loop(0, tm // ROWS_PER_ISSUE, issue, 0)

    def drain(g, _):
        for _unused in range(ROWS_PER_ISSUE * TOP_K):
            pltpu.make_async_copy(hn_ref.at[pl.ds(0, 1), :], xs_ref.at[pl.ds(0, 1), :], sem).wait()
        return 0

    lax.fori_loop(0, tm // ROWS_PER_ISSUE, drain, 0)


def _dispatch(dest_flat, hn, xs_init, tm):
    t = hn.shape[0]
    kernel = functools.partial(_dispatch_kernel, tm=tm)
    return pl.pallas_call(
        kernel,
        grid=(t // tm,),
        in_specs=[pl.BlockSpec((tm * TOP_K,), lambda i: (i,), memory_space=pltpu.SMEM),
                  pl.BlockSpec((tm, D_MODEL), lambda i: (i, 0)),
                  pl.BlockSpec(memory_space=pl.ANY)],
        out_specs=pl.BlockSpec(memory_space=pl.ANY),
        out_shape=jax.ShapeDtypeStruct(xs_init.shape, F32),
        scratch_shapes=[pltpu.SemaphoreType.DMA(())],
        input_output_aliases={2: 0},
        compiler_params=_cparams(("arbitrary",)),
        name="dispatch",
    )(dest_flat, hn, xs_init)


def _experts_kernel(be_ref, x_ref, wg_ref, bg_ref, wu_ref, bu_ref, wd_ref, bd_ref, y_ref):
    del be_ref
    x = x_ref[...].astype(BF16)
    gt = jnp.minimum(jnp.dot(x, wg_ref[0], preferred_element_type=F32) + bg_ref[0], SWIGLU_LIMIT)
    up = jnp.clip(jnp.dot(x, wu_ref[0], preferred_element_type=F32) + bu_ref[0],
                  -SWIGLU_LIMIT, SWIGLU_LIMIT)
    hh = (up + 1.0) * (gt * jax.nn.sigmoid(SWIGLU_ALPHA * gt))
    y_ref[...] = jnp.dot(hh.astype(BF16), wd_ref[0], preferred_element_type=F32) + bd_ref[0]


def _experts(blk_expert, xs, wg, bg, wu, bu, wd, bd, eb):
    n_pad = xs.shape[0]
    row = lambda i, be: (i, 0)
    wmap = lambda i, be: (be[i], 0, 0)
    wspec = pl.BlockSpec((1, D_MODEL, D_MODEL), wmap)
    bspec = pl.BlockSpec((1, 1, D_MODEL), wmap)
    grid_spec = pltpu.PrefetchScalarGridSpec(
        num_scalar_prefetch=1,
        grid=(n_pad // eb,),
        in_specs=[pl.BlockSpec((eb, D_MODEL), row), wspec, bspec, wspec, bspec, wspec, bspec],
        out_specs=pl.BlockSpec((eb, D_MODEL), row),
    )
    return pl.pallas_call(
        _experts_kernel,
        grid_spec=grid_spec,
        out_shape=jax.ShapeDtypeStruct((n_pad, D_MODEL), F32),
        compiler_params=_cparams(("arbitrary",)),
        name="experts",
    )(blk_expert, xs, wg, bg, wu, bu, wd, bd)


def _tail_kernel(dest_ref, h1_ref, tg_ref, p_ref, ys_ref, gple_ref, wpg_ref, wpp_ref, gfin_ref,
                 o_ref, buf_ref, sem, *, tm):
    def issue(g, _):
        for rr in range(ROWS_PER_ISSUE):
            r = g * ROWS_PER_ISSUE + rr
            for kk in range(TOP_K):
                pltpu.make_async_copy(ys_ref.at[pl.ds(dest_ref[r * TOP_K + kk], 1), :],
                                      buf_ref.at[kk, pl.ds(r, 1), :], sem).start()
        return 0

    lax.fori_loop(0, tm // ROWS_PER_ISSUE, issue, 0)

    def drain(g, _):
        for _unused in range(ROWS_PER_ISSUE * TOP_K):
            pltpu.make_async_copy(ys_ref.at[pl.ds(0, 1), :], buf_ref.at[0, pl.ds(0, 1), :], sem).wait()
        return 0

    lax.fori_loop(0, tm // ROWS_PER_ISSUE, drain, 0)

    tg = tg_ref[...]
    h2 = h1_ref[...]
    for kk in range(TOP_K):
        h2 = h2 + tg[:, kk:kk + 1] * buf_ref[kk]
    gate = jax.nn.sigmoid(jnp.dot(_rms(h2, gple_ref[...]).astype(BF16), wpg_ref[...],
                                  preferred_element_type=F32))
    proj = jnp.dot(p_ref[...].astype(BF16), wpp_ref[...], preferred_element_type=F32)
    o_ref[...] = _rms(h2 + gate * proj, gfin_ref[...])


def _tail(dest_flat, h1, tg, p, ys, g_ple, wpg_bf, wpp_bf, g_final, tm):
    t = h1.shape[0]
    row = lambda i: (i, 0)
    const = lambda i: (0, 0)
    full = lambda arr: pl.BlockSpec(arr.shape, const)
    kernel = functools.partial(_tail_kernel, tm=tm)
    return pl.pallas_call(
        kernel,
        grid=(t // tm,),
        in_specs=[pl.BlockSpec((tm * TOP_K,), lambda i: (i,), memory_space=pltpu.SMEM),
                  pl.BlockSpec((tm, D_MODEL), row), pl.BlockSpec((tm, LANES), row),
                  pl.BlockSpec((tm, PLE_DIM), row),
                  pl.BlockSpec(memory_space=pl.ANY),
                  full(g_ple), full(wpg_bf), full(wpp_bf), full(g_final)],
        out_specs=pl.BlockSpec((tm, D_MODEL), row),
        out_shape=jax.ShapeDtypeStruct((t, D_MODEL), F32),
        scratch_shapes=[pltpu.VMEM((TOP_K, tm, D_MODEL), F32), pltpu.SemaphoreType.DMA(())],
        compiler_params=_cparams(("arbitrary",)),
        name="tail",
    )(dest_flat, h1, tg, p, ys, g_ple, wpg_bf, wpp_bf, g_final)


def _routing(top_i, eb):
    t = top_i.shape[0]
    onehot = (top_i[:, :, None] == jnp.arange(N_EXPERTS, dtype=jnp.int32)).astype(jnp.int32)
    per_tok = onehot.sum(axis=1)
    incl = jnp.cumsum(per_tok, axis=0)
    counts = incl[-1]
    excl = incl - per_tok
    padded = ((counts + eb - 1) // eb) * eb
    pends = jnp.cumsum(padded)
    pstarts = pends - padded
    rank = jnp.take_along_axis(excl + pstarts[None, :], top_i, axis=1)
    n_blocks = -(-(t * TOP_K + N_EXPERTS * (eb - 1)) // eb)
    blk_start = jnp.arange(n_blocks, dtype=jnp.int32) * eb
    blk_expert = jnp.minimum(jnp.searchsorted(pends, blk_start, side='right'), N_EXPERTS - 1)
    return rank.reshape(-1).astype(jnp.int32), blk_expert.astype(jnp.int32), n_blocks * eb


def _pick(n, prefs):
    for c in prefs:
        if n % c == 0:
            return c
    raise ValueError(f"no tile for {n}")


def _trunk(x, p, w, lambda_init):
    b, l, _ = x.shape
    t = b * l
    tm = TOKEN_TILE
    assert l % tm == 0
    tc = _pick(l, (256, 128))
    eb = 256

    q, k, vt, u, nrm = _inproj(x.reshape(t, D_MODEL), w['g_mix'], w['w_qku'], w['w_vt'], tm)
    a = _attention(q.reshape(b, l, -1), k.reshape(b, l, -1), vt, w['attn_scal'],
                   _score_bounds(nrm, b, l // tm), w['attn_qf'], w['attn_kf'], w['g_subln'],
                   lambda_init, tm)
    u3 = u.reshape(b, l, SSM_WIDTH)
    pm, pmt = _perm_matrices(tc)
    ys = [_s5_direction(u3, pm, pmt, *w['s5'][dr], tc, reverse=(dr == 1)) for dr in range(2)]
    h1, hn, ti, tg = _mix(x.reshape(t, D_MODEL), a.reshape(t, ATTN_WIDTH), u,
                          ys[0].reshape(t, SSM_WIDTH), ys[1].reshape(t, SSM_WIDTH),
                          w['ssm_d'], w['w_glu'], w['b_glu'], w['w_out'], w['g_ffn'],
                          w['w_router'], w['b_router'], tm)
    dest, blk_expert, n_pad = _routing(ti[:, :TOP_K], eb)
    xs = _dispatch(dest, hn, jnp.zeros((n_pad, D_MODEL), F32), tm // 2)
    ye = _experts(blk_expert, xs, w['w_gate'], w['b_gate'], w['w_up'], w['b_up'],
                  w['w_down'], w['b_down'], eb)
    out = _tail(dest, h1, tg, p.reshape(t, PLE_DIM), ye, w['g_ple'], w['w_ple_gate'],
                w['w_ple_proj'], w['g_final'], tm // 2)
    return out.reshape(b, l, D_MODEL)


def _prepare(i, g_mix, w_in, lambda_q1, lambda_k1, lambda_q2, lambda_k2, g_subln, ssm_lambda_re,
             ssm_lambda_im, ssm_log_dt, ssm_b_re, ssm_b_im, ssm_c_re, ssm_c_im, ssm_d, w_glu, b_glu,
             w_out, g_ffn, w_router, b_router, w_gate, b_gate, w_up, b_up, w_down, b_down, g_ple,
             w_ple_gate, w_ple_proj, g_final, tcs):
    lambda_init = 0.8 - 0.6 * math.exp(-0.3 * i)
    lam = (jnp.exp(jnp.sum(lambda_q1[i] * lambda_k1[i]))
           - jnp.exp(jnp.sum(lambda_q2[i] * lambda_k2[i])) + lambda_init)
    slopes = jnp.exp2(-8.0 * jnp.arange(1, N_HEADS + 1, dtype=F32) / N_HEADS)
    bslopes = slopes * LOG2E
    parts, rest = [], bslopes
    for _ in range(N_BIAS_PARTS):
        part = rest.astype(BF16).astype(F32)
        parts.append(part)
        rest = rest - part
    attn_qf, attn_kf = _attn_features(jnp.stack(parts, axis=1), TOKEN_TILE)
    w_qku = jnp.concatenate([w_in[i][:, :2 * ATTN_WIDTH], w_in[i][:, 3 * ATTN_WIDTH:]], axis=1)
    row = lambda vec: vec.reshape(1, -1).astype(F32)
    w = {
        'g_mix': g_mix[i], 'w_qku': w_qku.astype(BF16),
        'w_vt': w_in[i][:, 2 * ATTN_WIDTH:3 * ATTN_WIDTH].T.astype(BF16),
        'attn_scal': jnp.concatenate([lam.reshape(1), bslopes, 1.0 / bslopes]).astype(F32),
        'attn_qf': attn_qf, 'attn_kf': attn_kf,
        'g_subln': g_subln[i],
        'ssm_d': row(ssm_d[i]), 'w_glu': w_glu[i].astype(BF16), 'b_glu': row(b_glu[i]),
        'w_out': w_out[i].astype(BF16), 'g_ffn': row(g_ffn[i]),
        'w_router': jnp.pad(w_router[i], ((0, 0), (0, LANES - N_EXPERTS))),
        'b_router': jnp.pad(row(b_router[i]), ((0, 0), (0, LANES - N_EXPERTS)),
                            constant_values=-jnp.inf),
        'w_gate': w_gate[i].astype(BF16), 'b_gate': b_gate[i].reshape(N_EXPERTS, 1, D_MODEL),
        'w_up': w_up[i].astype(BF16), 'b_up': b_up[i].reshape(N_EXPERTS, 1, D_MODEL),
        'w_down': w_down[i].astype(BF16), 'b_down': b_down[i].reshape(N_EXPERTS, 1, D_MODEL),
        'g_ple': row(g_ple[i]), 'w_ple_gate': w_ple_gate[i].astype(BF16),
        'w_ple_proj': w_ple_proj[i].astype(BF16), 'g_final': row(g_final),
    }
    w['s5'] = {tc: [_s5_params(ssm_lambda_re[i, dr], ssm_lambda_im[i, dr], ssm_log_dt[i, dr],
                               ssm_b_re[i, dr], ssm_b_im[i, dr], ssm_c_re[i, dr], ssm_c_im[i, dr],
                               tc, reverse=(dr == 1)) for dr in range(2)] for tc in tcs}
    return w, lambda_init


def kernel(x_prompt, x_sample, p_prompt, p_sample, g_mix, w_in, lambda_q1, lambda_k1, lambda_q2, lambda_k2, g_subln, ssm_lambda_re, ssm_lambda_im, ssm_log_dt, ssm_b_re, ssm_b_im, ssm_c_re, ssm_c_im, ssm_d, w_glu, b_glu, w_out, g_ffn, w_router, b_router, w_gate, b_gate, w_up, b_up, w_down, b_down, g_ple, w_ple_gate, w_ple_proj, g_final):
    assert w_in.shape[0] == 1, "single-layer trunk"
    tcs = {_pick(x.shape[1], (256, 128)) for x in (x_prompt, x_sample)}
    w, lambda_init = _prepare(0, g_mix, w_in, lambda_q1, lambda_k1, lambda_q2, lambda_k2, g_subln,
                              ssm_lambda_re, ssm_lambda_im, ssm_log_dt, ssm_b_re, ssm_b_im, ssm_c_re,
                              ssm_c_im, ssm_d, w_glu, b_glu, w_out, g_ffn, w_router, b_router, w_gate,
                              b_gate, w_up, b_up, w_down, b_down, g_ple, w_ple_gate, w_ple_proj,
                              g_final, tcs)
    outs = []
    for x, p in ((x_prompt, p_prompt), (x_sample, p_sample)):
        tc = _pick(x.shape[1], (256, 128))
        wt = dict(w, s5=w['s5'][tc])
        outs.append(_trunk(x, p[0], wt, lambda_init))
    return tuple(outs)
```

```python
import functools
import math

import jax
import jax.numpy as jnp
import numpy as np
from jax import lax
from jax.experimental import pallas as pl
from jax.experimental.pallas import tpu as pltpu

F32 = jnp.float32
BF16 = jnp.bfloat16

D_MODEL = 1024
PLE_DIM = 256
ATTN_WIDTH = 512
SSM_WIDTH = 512
HEAD_W = 128
DIFF_HEAD_DIM = 64
N_HEADS = 4
SSM_GROUP = 16
N_GROUPS = 32
SSM_STATE = 64
N_STATE = N_GROUPS * SSM_STATE
N_EXPERTS = 32
TOP_K = 4
SWIGLU_LIMIT = 7.0
SWIGLU_ALPHA = 1.702
EPS = 1e-6
LANES = 128
TOKEN_TILE = 512

VMEM_LIMIT = 56 * 1024 * 1024


def _cparams(sem):
    return pltpu.CompilerParams(dimension_semantics=sem, vmem_limit_bytes=VMEM_LIMIT)


def _rms(x, g):
    return x * lax.rsqrt(jnp.mean(x * x, axis=-1, keepdims=True) + EPS) * g


LOG2E = 1.4426950408889634
NT_DIMS = (((1,), (1,)), ((), ()))


def _inproj_kernel(x_ref, g_ref, wqku_ref, wvt_ref, seg_ref, q_ref, k_ref, vt_ref, u_ref, nrm_ref):
    xn = _rms(x_ref[...], g_ref[...]).astype(BF16)
    proj = jnp.dot(xn, wqku_ref[...], preferred_element_type=F32)
    scale = LOG2E / math.sqrt(DIFF_HEAD_DIM)
    qk = jnp.concatenate([proj[:, :ATTN_WIDTH] * scale, proj[:, ATTN_WIDTH:2 * ATTN_WIDTH]], axis=1)
    q_ref[...] = qk[:, :ATTN_WIDTH].astype(BF16)
    k_ref[...] = qk[:, ATTN_WIDTH:].astype(BF16)
    u_ref[...] = proj[:, 2 * ATTN_WIDTH:]
    nsq = jnp.dot((qk * qk).astype(BF16), seg_ref[...], preferred_element_type=F32)
    nrm_ref[0] = jnp.max(nsq.reshape(nsq.shape[0] // 8, 8, LANES), axis=0)
    vt_ref[0] = lax.dot_general(wvt_ref[...], xn, NT_DIMS, preferred_element_type=F32).astype(BF16)


def _inproj(x, g_mix, w_qku_bf, w_vt_bf, tm):
    t = x.shape[0]
    row = lambda i: (i, 0)
    const = lambda i: (0, 0)
    col = np.arange(2 * ATTN_WIDTH)
    seg = np.zeros((2 * ATTN_WIDTH, LANES), np.float32)
    seg[col, col // DIFF_HEAD_DIM] = 1.0
    seg = jnp.asarray(seg, BF16)
    return pl.pallas_call(
        _inproj_kernel,
        grid=(t // tm,),
        in_specs=[pl.BlockSpec((tm, D_MODEL), row),
                  pl.BlockSpec((1, D_MODEL), const),
                  pl.BlockSpec(w_qku_bf.shape, const),
                  pl.BlockSpec(w_vt_bf.shape, const),
                  pl.BlockSpec(seg.shape, const)],
        out_specs=[pl.BlockSpec((tm, ATTN_WIDTH), row), pl.BlockSpec((tm, ATTN_WIDTH), row),
                   pl.BlockSpec((1, ATTN_WIDTH, tm), lambda i: (i, 0, 0)),
                   pl.BlockSpec((tm, SSM_WIDTH), row),
                   pl.BlockSpec((1, 8, LANES), lambda i: (i, 0, 0))],
        out_shape=[jax.ShapeDtypeStruct((t, ATTN_WIDTH), BF16), jax.ShapeDtypeStruct((t, ATTN_WIDTH), BF16),
                   jax.ShapeDtypeStruct((t // tm, ATTN_WIDTH, tm), BF16),
                   jax.ShapeDtypeStruct((t, SSM_WIDTH), F32),
                   jax.ShapeDtypeStruct((t // tm, 8, LANES), F32)],
        compiler_params=_cparams(("parallel",)),
        name="inproj",
    )(x, g_mix.reshape(1, D_MODEL), w_qku_bf, w_vt_bf, seg)


ONES_ROWS = 16
ACC_ROWS = HEAD_W + ONES_ROWS
POS_SPLIT = 16
N_BIAS_PARTS = 3
SKIP_BITS = 130.0
NORM_INFLATE = 1.0 + 2.0 ** -5


def _attn_kernel(sc_ref, ub_ref, q_ref, k_ref, vt_ref, qf_ref, kf_ref, g_ref, o_ref,
                 qa_ref, m_ref, acc_ref, sa_ref, sb_ref, pa_ref, pb_ref, ala_ref, alb_ref,
                 *, ts, out_scale):
    bi = pl.program_id(0)
    h = pl.program_id(1)
    qi = pl.program_id(2)
    nsub = k_ref.shape[1] // ts
    bslope = sc_ref[1 + h]

    q = q_ref[0]
    lane = lax.broadcasted_iota(jnp.int32, q.shape, 1)
    zero = jnp.zeros_like(q)
    qf = qf_ref[0]
    qa_ref[0] = jnp.concatenate([jnp.where(lane < DIFF_HEAD_DIM, q, zero), qf], axis=1)
    qa_ref[1] = jnp.concatenate([jnp.where(lane >= DIFF_HEAD_DIM, q, zero), qf], axis=1)
    ones = jnp.ones((ONES_ROWS, ts), BF16)

    def keys(j, variant):
        rows = pl.ds(pl.multiple_of(j * ts, ts), ts)
        return jnp.concatenate([k_ref[0, rows, :], kf_ref[0, variant]], axis=1)

    def values(j):
        return jnp.concatenate([vt_ref[j], ones], axis=0)

    kpos = lax.broadcasted_iota(jnp.int32, (ts, ts), 0)
    qpos = lax.broadcasted_iota(jnp.int32, (ts, ts), 1)
    bias = jnp.abs(qpos - kpos).astype(F32) * (-bslope)
    ka = keys(qi, 2)
    vta = values(qi)
    for st in range(2):
        s = lax.dot_general(ka, qa_ref[st], NT_DIMS, preferred_element_type=F32) + bias
        m_new = jnp.max(s, axis=0, keepdims=True)
        p = jnp.exp2(s - m_new).astype(BF16)
        acc_ref[st] = jnp.dot(vta, p, preferred_element_type=F32)
        m_ref[st] = m_new

    ub = ub_ref[(bi * N_HEADS + h) * nsub + qi]
    slack = ub - jnp.min(m_ref[...]) + SKIP_BITS
    x = (slack * sc_ref[1 + N_HEADS + h] - 1.0) * (1.0 / ts)
    w = jnp.where(x < 0.0, 0, jnp.minimum(x, float(nsub)).astype(jnp.int32) + 1)
    lo = jnp.maximum(qi - w, 0)
    n_off = jnp.minimum(qi + w, nsub - 1) - lo

    def sub_index(n):
        j = lo + jnp.clip(n, 0, jnp.maximum(n_off - 1, 0))
        return j + (j >= qi).astype(jnp.int32)

    def qk(n, s_ref):
        j = sub_index(n)
        ka = keys(j, jnp.where(j < qi, 0, 1))
        for st in range(2):
            s_ref[st] = lax.dot_general(ka, qa_ref[st], NT_DIMS, preferred_element_type=F32)

    def softmax(n, s_ref, p_ref, al_ref):
        j = sub_index(n)
        c = -bslope * (jnp.abs(qi - j) * ts).astype(F32)
        for st in range(2):
            m_old = m_ref[st]
            m_new = jnp.maximum(m_old, jnp.max(s_ref[st], axis=0, keepdims=True) + c)
            al_ref[st] = jnp.exp2(m_old - m_new)
            p_ref[st] = jnp.exp2(s_ref[st] - (m_new - c)).astype(BF16)
            m_ref[st] = m_new

    def pv(n, p_ref, al_ref):
        vta = values(sub_index(n))
        for st in range(2):
            acc_ref[st] = al_ref[st] * acc_ref[st] + jnp.dot(vta, p_ref[st], preferred_element_type=F32)

    @pl.when(n_off > 0)
    def _():
        for p_ref, al_ref in ((pa_ref, ala_ref), (pb_ref, alb_ref)):
            p_ref[...] = jnp.zeros_like(p_ref)
            al_ref[...] = jnp.ones_like(al_ref)
        qk(0, sa_ref)

        def pair(g, carry):
            n = 2 * g
            pv(n - 2, pa_ref, ala_ref)
            qk(n + 1, sb_ref)
            softmax(n, sa_ref, pa_ref, ala_ref)
            pv(n - 1, pb_ref, alb_ref)
            qk(n + 2, sa_ref)
            softmax(n + 1, sb_ref, pb_ref, alb_ref)
            return carry

        n_pairs = n_off // 2
        odd = n_off % 2 == 1
        lax.fori_loop(0, n_pairs, pair, 0)
        pv(2 * n_pairs - 2, pa_ref, ala_ref)

        @pl.when(odd)
        def _():
            softmax(n_off - 1, sa_ref, pa_ref, ala_ref)

        pv(2 * n_pairs - 1, pb_ref, alb_ref)

        @pl.when(odd)
        def _():
            pv(n_off - 1, pa_ref, ala_ref)

    lam = sc_ref[0]
    a1 = acc_ref[0]
    a2 = acc_ref[1]
    out_t = (a1[:HEAD_W] / a1[HEAD_W:HEAD_W + 1] - lam * (a2[:HEAD_W] / a2[HEAD_W:HEAD_W + 1]))
    ms = jnp.mean(out_t * out_t, axis=0, keepdims=True)
    out_t = out_t * lax.rsqrt(ms + EPS) * (g_ref[...] * out_scale)
    o_ref[0] = out_t.T.astype(o_ref.dtype)


def _attn_features(bparts, ts):
    pos = jnp.arange(ts, dtype=jnp.int32)
    hi = ((pos // POS_SPLIT) * POS_SPLIT).astype(F32)[None, :, None]
    lo = (pos % POS_SPLIT).astype(F32)[None, :, None]
    bp = bparts[:, None, :]
    one = jnp.ones_like(bp)
    qf = jnp.stack([bp + 0 * hi, bp + 0 * hi, hi * one, lo * one], axis=-1)
    kf = jnp.stack([hi * one, lo * one, -bp + 0 * hi, -bp + 0 * hi], axis=-1)
    pad = lambda f: jnp.pad(f.reshape(N_HEADS, ts, 4 * N_BIAS_PARTS),
                            ((0, 0), (0, 0), (0, HEAD_W - 4 * N_BIAS_PARTS)))
    qf, kf = pad(qf), pad(kf)
    kf = jnp.stack([kf, -kf, jnp.zeros_like(kf)], axis=1)
    return qf.astype(BF16), kf.astype(BF16)


def _score_bounds(nrm, b, nsub):
    nrm = jnp.max(nrm, axis=1)[:, :2 * N_HEADS * 2].reshape(b, nsub, 2, N_HEADS, 2)
    q2 = nrm[:, :, 0]
    k2 = jnp.max(nrm[:, :, 1], axis=1)
    ub = jnp.max(jnp.sqrt(q2 * k2[:, None]), axis=-1) * NORM_INFLATE
    return jnp.transpose(ub, (0, 2, 1)).reshape(-1)


def _attention(q, k, vt, scal, ub, qf, kf, g_subln, lambda_init, ts):
    b, l, _ = q.shape
    nsub = l // ts
    kernel = functools.partial(_attn_kernel, ts=ts, out_scale=1.0 - lambda_init)
    return pl.pallas_call(
        kernel,
        grid=(b, N_HEADS, nsub),
        in_specs=[pl.BlockSpec(memory_space=pltpu.SMEM),
                  pl.BlockSpec(memory_space=pltpu.SMEM),
                  pl.BlockSpec((1, ts, HEAD_W), lambda bi, h, qi: (bi, qi, h)),
                  pl.BlockSpec((1, l, HEAD_W), lambda bi, h, qi: (bi, 0, h)),
                  pl.BlockSpec((nsub, HEAD_W, ts), lambda bi, h, qi: (bi, h, 0)),
                  pl.BlockSpec((1, ts, HEAD_W), lambda bi, h, qi: (h, 0, 0)),
                  pl.BlockSpec((1, 3, ts, HEAD_W), lambda bi, h, qi: (h, 0, 0, 0)),
                  pl.BlockSpec((HEAD_W, 1), lambda bi, h, qi: (0, 0))],
        out_specs=pl.BlockSpec((1, ts, HEAD_W), lambda bi, h, qi: (bi, qi, h)),
        out_shape=jax.ShapeDtypeStruct((b, l, ATTN_WIDTH), BF16),
        scratch_shapes=[pltpu.VMEM((2, ts, 2 * HEAD_W), BF16),
                        pltpu.VMEM((2, 1, ts), F32),
                        pltpu.VMEM((2, ACC_ROWS, ts), F32),
                        pltpu.VMEM((2, ts, ts), F32), pltpu.VMEM((2, ts, ts), F32),
                        pltpu.VMEM((2, ts, ts), BF16), pltpu.VMEM((2, ts, ts), BF16),
                        pltpu.VMEM((2, 1, ts), F32), pltpu.VMEM((2, 1, ts), F32)],
        compiler_params=_cparams(("parallel", "parallel", "arbitrary")),
        name="attn",
    )(scal, ub, q, k, vt, qf, kf, g_subln.reshape(HEAD_W, 1))


SCAN_LANES = 512
S5_BLOCKS = 2


def _s5_kernel(u_ref, pm_ref, pmt_ref, bb_ref, cc_ref, a_ref, at_ref, apow_ref, y_ref,
               xs_ref, xb_ref, carry_ref, *, tc, reverse):
    tc8 = tc // 8
    c = pl.program_id(1)

    @pl.when(c == 0)
    def _():
        carry_ref[...] = jnp.zeros_like(carry_ref)

    up = jnp.dot(pm_ref[...], u_ref[0].astype(BF16), preferred_element_type=F32).astype(BF16)
    cw = SSM_WIDTH // S5_BLOCKS
    sw = N_STATE // S5_BLOCKS
    for hb in range(S5_BLOCKS):
        bu = jnp.dot(up[:, hb * cw:(hb + 1) * cw], bb_ref[hb], preferred_element_type=F32)
        xs_ref[:, hb * sw:(hb + 1) * sw] = bu[:, :sw]
        xs_ref[:, N_STATE + hb * sw:N_STATE + (hb + 1) * sw] = bu[:, sw:]

    for cb in range(N_STATE // SCAN_LANES):
        lo = cb * SCAN_LANES
        re_sl = pl.ds(lo, SCAN_LANES)
        im_sl = pl.ds(N_STATE + lo, SCAN_LANES)
        ar = jnp.broadcast_to(a_ref[0:1, lo:lo + SCAN_LANES], (8, SCAN_LANES))
        ai = jnp.broadcast_to(a_ref[1:2, lo:lo + SCAN_LANES], (8, SCAN_LANES))

        def step(i, carry):
            xr, xi = carry
            r = (tc8 - 1 - i) if reverse else i
            rows = pl.ds(pl.multiple_of(r * 8, 8), 8)
            nr = ar * xr - ai * xi + xs_ref[rows, re_sl]
            ni = ar * xi + ai * xr + xs_ref[rows, im_sl]
            xs_ref[rows, re_sl] = nr
            xs_ref[rows, im_sl] = ni
            return nr, ni

        z = jnp.zeros((8, SCAN_LANES), F32)
        lax.fori_loop(0, tc8, step, (z, z), unroll=4)

    last = 0 if reverse else (tc8 - 1) * 8
    e_re = xs_ref[last:last + 8, 0:N_STATE]
    e_im = xs_ref[last:last + 8, N_STATE:2 * N_STATE]
    atr = at_ref[0:1, :]
    ati = at_ref[1:2, :]
    cr = carry_ref[0:1, :]
    ci = carry_ref[1:2, :]
    cin_r = [None] * 8
    cin_i = [None] * 8
    order = range(7, -1, -1) if reverse else range(8)
    for j in order:
        cin_r[j] = cr
        cin_i[j] = ci
        nr = atr * cr - ati * ci + e_re[j:j + 1]
        ni = atr * ci + ati * cr + e_im[j:j + 1]
        cr, ci = nr, ni
    carry_ref[0:1, :] = cr
    carry_ref[1:2, :] = ci
    cin_r = jnp.concatenate(cin_r + cin_r, axis=0)
    cin_i = jnp.concatenate(cin_i + cin_i, axis=0)

    def fix(i, _):
        rows = pl.ds(pl.multiple_of(i * 16, 16), 16)
        pr = apow_ref[rows, 0:N_STATE]
        pi = apow_ref[rows, N_STATE:2 * N_STATE]
        xr = xs_ref[rows, 0:N_STATE] + pr * cin_r - pi * cin_i
        xi = xs_ref[rows, N_STATE:2 * N_STATE] + pr * cin_i + pi * cin_r
        xb_ref[rows, 0:N_STATE] = xr.astype(BF16)
        xb_ref[rows, N_STATE:2 * N_STATE] = xi.astype(BF16)
        return 0

    lax.fori_loop(0, tc // 16, fix, 0)

    yp = jnp.concatenate(
        [jnp.dot(xb_ref[:, hb * sw:(hb + 1) * sw], cc_ref[hb, 0], preferred_element_type=F32)
         + jnp.dot(xb_ref[:, N_STATE + hb * sw:N_STATE + (hb + 1) * sw], cc_ref[hb, 1],
                   preferred_element_type=F32) for hb in range(S5_BLOCKS)], axis=1)
    hi = yp.astype(BF16)
    lo_part = (yp - hi.astype(F32)).astype(BF16)
    y_ref[0] = (jnp.dot(pmt_ref[...], hi, preferred_element_type=F32)
                + jnp.dot(pmt_ref[...], lo_part, preferred_element_type=F32))


def _s5_direction(u, pm, pmt, bb, cc, a, at, apow, tc, reverse):
    b, l, _ = u.shape
    nc = l // tc
    cmap = (lambda bi, c: (bi, nc - 1 - c, 0)) if reverse else (lambda bi, c: (bi, c, 0))
    full = lambda arr: pl.BlockSpec(arr.shape, lambda bi, c: (0,) * arr.ndim)
    kernel = functools.partial(_s5_kernel, tc=tc, reverse=reverse)
    return pl.pallas_call(
        kernel,
        grid=(b, nc),
        in_specs=[pl.BlockSpec((1, tc, SSM_WIDTH), cmap), full(pm), full(pmt), full(bb), full(cc),
                  full(a), full(at), full(apow)],
        out_specs=pl.BlockSpec((1, tc, SSM_WIDTH), cmap),
        out_shape=jax.ShapeDtypeStruct((b, l, SSM_WIDTH), F32),
        scratch_shapes=[pltpu.VMEM((tc, 2 * N_STATE), F32), pltpu.VMEM((tc, 2 * N_STATE), BF16),
                        pltpu.VMEM((2, N_STATE), F32)],
        compiler_params=_cparams(("parallel", "arbitrary")),
        name="s5_bwd" if reverse else "s5_fwd",
    )(u, pm, pmt, bb, cc, a, at, apow)


def _s5_params(lam_re, lam_im, log_dt, b_re, b_im, c_re, c_im, tc, reverse):
    tc8 = tc // 8
    dt = jnp.exp(log_dt)[:, None]
    mag = jnp.exp(lam_re * dt)
    ab_re = mag * jnp.cos(lam_im * dt)
    ab_im = mag * jnp.sin(lam_im * dt)
    den = lam_re * lam_re + lam_im * lam_im
    nr, ni = ab_re - 1.0, ab_im
    co_re = (nr * lam_re + ni * lam_im) / den
    co_im = (ni * lam_re - nr * lam_im) / den
    bb_re = co_re[..., None] * b_re - co_im[..., None] * b_im
    bb_im = co_re[..., None] * b_im + co_im[..., None] * b_re
    gb = N_GROUPS // S5_BLOCKS
    cw, sw = SSM_WIDTH // S5_BLOCKS, N_STATE // S5_BLOCKS
    eye = jnp.eye(gb, dtype=F32)
    blk_in = lambda w: jnp.einsum('bgph,gk->bghkp', w.reshape(S5_BLOCKS, gb, SSM_STATE, SSM_GROUP),
                                  eye).reshape(S5_BLOCKS, cw, sw)
    bb = jnp.concatenate([blk_in(bb_re), blk_in(bb_im)], axis=2).astype(BF16)
    blk_out = lambda w: jnp.einsum('bghp,gk->bgpkh', w.reshape(S5_BLOCKS, gb, SSM_GROUP, SSM_STATE),
                                   eye).reshape(S5_BLOCKS, sw, cw)
    cc = jnp.stack([blk_out(c_re), -blk_out(c_im)], axis=1).astype(BF16)
    a = jnp.stack([ab_re.reshape(-1), ab_im.reshape(-1)])
    n = jnp.arange(1, tc8 + 1, dtype=F32)[:, None, None]
    pmag = jnp.exp(n * (lam_re * dt)[None])
    ang = n * (lam_im * dt)[None]
    pw_re = (pmag * jnp.cos(ang)).reshape(tc8, N_STATE)
    pw_im = (pmag * jnp.sin(ang)).reshape(tc8, N_STATE)
    at = jnp.stack([pw_re[-1], pw_im[-1]])
    if reverse:
        pw_re, pw_im = pw_re[::-1], pw_im[::-1]
    apow = jnp.concatenate([jnp.repeat(pw_re, 8, axis=0), jnp.repeat(pw_im, 8, axis=0)], axis=1)
    return bb, cc, a, at, apow


def _perm_matrices(tc):
    tc8 = tc // 8
    i = np.arange(tc)
    src = (i % 8) * tc8 + i // 8
    pm = np.zeros((tc, tc), np.float32)
    pm[i, src] = 1.0
    return jnp.asarray(pm, BF16), jnp.asarray(pm.T, BF16)


def _mix_kernel(h_ref, a_ref, u_ref, yf_ref, yb_ref, d_ref, wglu_ref, bglu_ref, wout_ref,
                gffn_ref, wr_ref, br_ref, tri_ref, h1_ref, hn_ref, ti_ref, tg_ref, counts_ref, cnt_ref):
    y = d_ref[...] * u_ref[...] + yf_ref[...] + yb_ref[...]
    y = jax.nn.gelu(y)
    z = jnp.dot(y.astype(BF16), wglu_ref[...], preferred_element_type=F32) + bglu_ref[...]
    s = y * jax.nn.sigmoid(z)
    h1 = (h_ref[...]
          + jnp.dot(a_ref[...], wout_ref[0:ATTN_WIDTH, :], preferred_element_type=F32)
          + jnp.dot(s.astype(BF16), wout_ref[ATTN_WIDTH:, :], preferred_element_type=F32))
    h1_ref[...] = h1
    hn = _rms(h1, gffn_ref[...])
    hn_ref[...] = hn
    logits = jnp.dot(hn, wr_ref[...], preferred_element_type=F32,
                     precision=lax.Precision.HIGHEST) + br_ref[...]
    lane = lax.broadcasted_iota(jnp.int32, logits.shape, 1)
    neg = jnp.float32(-jnp.inf)
    work = logits
    ti = jnp.zeros(logits.shape, jnp.int32)
    tv = jnp.full(logits.shape, neg, F32)
    picked = jnp.zeros(logits.shape, F32)
    idxs = []
    for kk in range(TOP_K):
        mx = jnp.max(work, axis=-1, keepdims=True)
        idx = jnp.min(jnp.where(work == mx, lane, LANES), axis=-1, keepdims=True)
        idxs.append(idx)
        ti = jnp.where(lane == kk, idx, ti)
        tv = jnp.where(lane == kk, mx, tv)
        picked = jnp.where(lane == idx, 1.0, picked)
        work = jnp.where(lane == idx, neg, work)
    ex = jnp.exp(tv - jnp.max(tv, axis=-1, keepdims=True))
    tg_ref[...] = ex / jnp.sum(ex, axis=-1, keepdims=True)

    @pl.when(pl.program_id(0) == 0)
    def _():
        cnt_ref[...] = jnp.zeros_like(cnt_ref)

    before = jnp.dot(tri_ref[...], picked.astype(BF16), preferred_element_type=F32) + cnt_ref[0:1, :]
    for kk in range(TOP_K):
        rank = jnp.sum(jnp.where(lane == idxs[kk], before, 0.0), axis=-1, keepdims=True)
        ti = jnp.where(lane == TOP_K + kk, rank.astype(jnp.int32), ti)
    ti_ref[...] = ti
    cnt_ref[...] = cnt_ref[...] + jnp.sum(picked, axis=0, keepdims=True)
    counts_ref[...] = cnt_ref[...]


def _mix(h, a, u, yf, yb, d, wglu_bf, bglu, wout_bf, g_ffn, wr_pad, br_pad, tm):
    t = h.shape[0]
    row = lambda i: (i, 0)
    const = lambda i: (0, 0)
    full = lambda arr: pl.BlockSpec(arr.shape, const)
    tri = jnp.asarray(np.tril(np.ones((tm, tm), np.float32), -1), BF16)
    return pl.pallas_call(
        _mix_kernel,
        grid=(t // tm,),
        in_specs=[pl.BlockSpec((tm, D_MODEL), row), pl.BlockSpec((tm, ATTN_WIDTH), row),
                  pl.BlockSpec((tm, SSM_WIDTH), row), pl.BlockSpec((tm, SSM_WIDTH), row),
                  pl.BlockSpec((tm, SSM_WIDTH), row),
                  full(d), full(wglu_bf), full(bglu), full(wout_bf), full(g_ffn), full(wr_pad),
                  full(br_pad), full(tri)],
        out_specs=[pl.BlockSpec((tm, D_MODEL), row), pl.BlockSpec((tm, D_MODEL), row),
                   pl.BlockSpec((tm, LANES), row), pl.BlockSpec((tm, LANES), row),
                   pl.BlockSpec((8, LANES), const)],
        out_shape=[jax.ShapeDtypeStruct((t, D_MODEL), F32), jax.ShapeDtypeStruct((t, D_MODEL), F32),
                   jax.ShapeDtypeStruct((t, LANES), jnp.int32), jax.ShapeDtypeStruct((t, LANES), F32),
                   jax.ShapeDtypeStruct((8, LANES), F32)],
        scratch_shapes=[pltpu.VMEM((8, LANES), F32)],
        compiler_params=_cparams(("arbitrary",)),
        name="mix",
    )(h, a, u, yf, yb, d, wglu_bf, bglu, wout_bf, g_ffn, wr_pad, br_pad, tri)


ROWS_PER_ISSUE = 4

def _dispatch_kernel(dest_ref, hn_ref, xs_in_ref, xs_ref, sem, *, tm):
    del xs_in_ref

    def issue(g, _):
        for rr in range(ROWS_PER_ISSUE):
            r = g * ROWS_PER_ISSUE + rr
            for kk in range(TOP_K):
                pltpu.make_async_copy(hn_ref.at[pl.ds(r, 1), :],
                                      xs_ref.at[pl.ds(dest_ref[r * TOP_K + kk], 1), :], sem).start()
        return 0

    lax.fori_loop(0, tm // ROWS_PER_ISSUE, issue, 0)

    def drain(g, _):
        for _unused in range(ROWS_PER_ISSUE * TOP_K):
            pltpu.make_async_copy(hn_ref.at[pl.ds(0, 1), :], xs_ref.at[pl.ds(0, 1), :], sem).wait()
        return 0

    lax.fori_loop(0, tm // ROWS_PER_ISSUE, drain, 0)


def _dispatch(dest_flat, hn, xs_init, tm):
    t = hn.shape[0]
    kernel = functools.partial(_dispatch_kernel, tm=tm)
    return pl.pallas_call(
        kernel,
        grid=(t // tm,),
        in_specs=[pl.BlockSpec((tm * TOP_K,), lambda i: (i,), memory_space=pltpu.SMEM),
                  pl.BlockSpec((tm, D_MODEL), lambda i: (i, 0)),
                  pl.BlockSpec(memory_space=pl.ANY)],
        out_specs=pl.BlockSpec(memory_space=pl.ANY),
        out_shape=jax.ShapeDtypeStruct(xs_init.shape, F32),
        scratch_shapes=[pltpu.SemaphoreType.DMA(())],
        input_output_aliases={2: 0},
        compiler_params=_cparams(("arbitrary",)),
        name="dispatch",
    )(dest_flat, hn, xs_init)


def _experts_kernel(be_ref, x_ref, wg_ref, bg_ref, wu_ref, bu_ref, wd_ref, bd_ref, y_ref,
                    wgb_ref, wub_ref, wdb_ref):
    i = pl.program_id(0)

    @pl.when((i == 0) | (be_ref[i] != be_ref[jnp.maximum(i - 1, 0)]))
    def _():
        wgb_ref[...] = wg_ref[0].astype(BF16)
        wub_ref[...] = wu_ref[0].astype(BF16)
        wdb_ref[...] = wd_ref[0].astype(BF16)

    x = x_ref[...].astype(BF16)
    gt = jnp.minimum(jnp.dot(x, wgb_ref[...], preferred_element_type=F32) + bg_ref[0], SWIGLU_LIMIT)
    up = jnp.clip(jnp.dot(x, wub_ref[...], preferred_element_type=F32) + bu_ref[0],
                  -SWIGLU_LIMIT, SWIGLU_LIMIT)
    hh = (up + 1.0) * (gt * jax.nn.sigmoid(SWIGLU_ALPHA * gt))
    y_ref[...] = jnp.dot(hh.astype(BF16), wdb_ref[...], preferred_element_type=F32) + bd_ref[0]


def _experts(blk_expert, xs, wg, bg, wu, bu, wd, bd, eb):
    n_pad = xs.shape[0]
    row = lambda i, be: (i, 0)
    wmap = lambda i, be: (be[i], 0, 0)
    wspec = pl.BlockSpec((1, D_MODEL, D_MODEL), wmap)
    bspec = pl.BlockSpec((1, 1, D_MODEL), wmap)
    grid_spec = pltpu.PrefetchScalarGridSpec(
        num_scalar_prefetch=1,
        grid=(n_pad // eb,),
        in_specs=[pl.BlockSpec((eb, D_MODEL), row), wspec, bspec, wspec, bspec, wspec, bspec],
        out_specs=pl.BlockSpec((eb, D_MODEL), row),
        scratch_shapes=[pltpu.VMEM((D_MODEL, D_MODEL), BF16)] * 3,
    )
    return pl.pallas_call(
        _experts_kernel,
        grid_spec=grid_spec,
        out_shape=jax.ShapeDtypeStruct((n_pad, D_MODEL), F32),
        compiler_params=_cparams(("arbitrary",)),
        name="experts",
    )(blk_expert, xs, wg, bg, wu, bu, wd, bd)


def _tail_kernel(dest_ref, h1_ref, tg_ref, p_ref, ys_ref, gple_ref, wpg_ref, wpp_ref, gfin_ref,
                 o_ref, buf_ref, sem, *, tm):
    def issue(g, _):
        for rr in range(ROWS_PER_ISSUE):
            r = g * ROWS_PER_ISSUE + rr
            for kk in range(TOP_K):
                pltpu.make_async_copy(ys_ref.at[pl.ds(dest_ref[r * TOP_K + kk], 1), :],
                                      buf_ref.at[kk, pl.ds(r, 1), :], sem).start()
        return 0

    lax.fori_loop(0, tm // ROWS_PER_ISSUE, issue, 0)

    def drain(g, _):
        for _unused in range(ROWS_PER_ISSUE * TOP_K):
            pltpu.make_async_copy(ys_ref.at[pl.ds(0, 1), :], buf_ref.at[0, pl.ds(0, 1), :], sem).wait()
        return 0

    lax.fori_loop(0, tm // ROWS_PER_ISSUE, drain, 0)

    tg = tg_ref[...]
    h2 = h1_ref[...]
    for kk in range(TOP_K):
        h2 = h2 + tg[:, kk:kk + 1] * buf_ref[kk]
    gate = jax.nn.sigmoid(jnp.dot(_rms(h2, gple_ref[...]).astype(BF16), wpg_ref[...],
                                  preferred_element_type=F32))
    proj = jnp.dot(p_ref[...].astype(BF16), wpp_ref[...], preferred_element_type=F32)
    o_ref[...] = _rms(h2 + gate * proj, gfin_ref[...])


def _tail(dest_flat, h1, tg, p, ys, g_ple, wpg_bf, wpp_bf, g_final, tm):
    t = h1.shape[0]
    row = lambda i: (i, 0)
    const = lambda i: (0, 0)
    full = lambda arr: pl.BlockSpec(arr.shape, const)
    kernel = functools.partial(_tail_kernel, tm=tm)
    return pl.pallas_call(
        kernel,
        grid=(t // tm,),
        in_specs=[pl.BlockSpec((tm * TOP_K,), lambda i: (i,), memory_space=pltpu.SMEM),
                  pl.BlockSpec((tm, D_MODEL), row), pl.BlockSpec((tm, LANES), row),
                  pl.BlockSpec((tm, PLE_DIM), row),
                  pl.BlockSpec(memory_space=pl.ANY),
                  full(g_ple), full(wpg_bf), full(wpp_bf), full(g_final)],
        out_specs=pl.BlockSpec((tm, D_MODEL), row),
        out_shape=jax.ShapeDtypeStruct((t, D_MODEL), F32),
        scratch_shapes=[pltpu.VMEM((TOP_K, tm, D_MODEL), F32), pltpu.SemaphoreType.DMA(())],
        compiler_params=_cparams(("arbitrary",)),
        name="tail",
    )(dest_flat, h1, tg, p, ys, g_ple, wpg_bf, wpp_bf, g_final)


def _routing(ti, counts, eb):
    t = ti.shape[0]
    top_i, rank = ti[:, :TOP_K], ti[:, TOP_K:2 * TOP_K]
    counts = counts[0, :N_EXPERTS].astype(jnp.int32)
    padded = ((counts + eb - 1) // eb) * eb
    pends = jnp.cumsum(padded)
    pstarts = pends - padded
    onehot = top_i[:, :, None] == jnp.arange(N_EXPERTS, dtype=jnp.int32)
    dest = rank + jnp.sum(jnp.where(onehot, pstarts, 0), axis=-1)
    n_blocks = -(-(t * TOP_K + N_EXPERTS * (eb - 1)) // eb)
    blk_start = jnp.arange(n_blocks, dtype=jnp.int32) * eb
    blk_expert = jnp.minimum(jnp.searchsorted(pends, blk_start, side='right'), N_EXPERTS - 1)
    return dest.reshape(-1).astype(jnp.int32), blk_expert.astype(jnp.int32), n_blocks * eb


def _pick(n, prefs):
    for c in prefs:
        if n % c == 0:
            return c
    raise ValueError(f"no tile for {n}")


def _trunk(x, p, w, lambda_init):
    b, l, _ = x.shape
    t = b * l
    tm = TOKEN_TILE
    assert l % tm == 0
    tc = _pick(l, (256, 128))
    eb = 256

    q, k, vt, u, nrm = _inproj(x.reshape(t, D_MODEL), w['g_mix'], w['w_qku'], w['w_vt'], tm)
    a = _attention(q.reshape(b, l, -1), k.reshape(b, l, -1), vt, w['attn_scal'],
                   _score_bounds(nrm, b, l // tm), w['attn_qf'], w['attn_kf'], w['g_subln'],
                   lambda_init, tm)
    u3 = u.reshape(b, l, SSM_WIDTH)
    pm, pmt = _perm_matrices(tc)
    ys = [_s5_direction(u3, pm, pmt, *w['s5'][dr], tc, reverse=(dr == 1)) for dr in range(2)]
    h1, hn, ti, tg, counts = _mix(x.reshape(t, D_MODEL), a.reshape(t, ATTN_WIDTH), u,
                                  ys[0].reshape(t, SSM_WIDTH), ys[1].reshape(t, SSM_WIDTH),
                                  w['ssm_d'], w['w_glu'], w['b_glu'], w['w_out'], w['g_ffn'],
                                  w['w_router'], w['b_router'], tm)
    dest, blk_expert, n_pad = _routing(ti, counts, eb)
    xs = _dispatch(dest, hn, jnp.zeros((n_pad, D_MODEL), F32), tm // 2)
    ye = _experts(blk_expert, xs, w['w_gate'], w['b_gate'], w['w_up'], w['b_up'],
                  w['w_down'], w['b_down'], eb)
    out = _tail(dest, h1, tg, p.reshape(t, PLE_DIM), ye, w['g_ple'], w['w_ple_gate'],
                w['w_ple_proj'], w['g_final'], tm // 2)
    return out.reshape(b, l, D_MODEL)


def _prepare(i, g_mix, w_in, lambda_q1, lambda_k1, lambda_q2, lambda_k2, g_subln, ssm_lambda_re,
             ssm_lambda_im, ssm_log_dt, ssm_b_re, ssm_b_im, ssm_c_re, ssm_c_im, ssm_d, w_glu, b_glu,
             w_out, g_ffn, w_router, b_router, w_gate, b_gate, w_up, b_up, w_down, b_down, g_ple,
             w_ple_gate, w_ple_proj, g_final, tcs):
    lambda_init = 0.8 - 0.6 * math.exp(-0.3 * i)
    lam = (jnp.exp(jnp.sum(lambda_q1[i] * lambda_k1[i]))
           - jnp.exp(jnp.sum(lambda_q2[i] * lambda_k2[i])) + lambda_init)
    slopes = jnp.exp2(-8.0 * jnp.arange(1, N_HEADS + 1, dtype=F32) / N_HEADS)
    bslopes = slopes * LOG2E
    parts, rest = [], bslopes
    for _ in range(N_BIAS_PARTS):
        part = rest.astype(BF16).astype(F32)
        parts.append(part)
        rest = rest - part
    attn_qf, attn_kf = _attn_features(jnp.stack(parts, axis=1), TOKEN_TILE)
    w_qku = jnp.concatenate([w_in[i][:, :2 * ATTN_WIDTH], w_in[i][:, 3 * ATTN_WIDTH:]], axis=1)
    row = lambda vec: vec.reshape(1, -1).astype(F32)
    w = {
        'g_mix': g_mix[i], 'w_qku': w_qku.astype(BF16),
        'w_vt': w_in[i][:, 2 * ATTN_WIDTH:3 * ATTN_WIDTH].T.astype(BF16),
        'attn_scal': jnp.concatenate([lam.reshape(1), bslopes, 1.0 / bslopes]).astype(F32),
        'attn_qf': attn_qf, 'attn_kf': attn_kf,
        'g_subln': g_subln[i],
        'ssm_d': row(ssm_d[i]), 'w_glu': w_glu[i].astype(BF16), 'b_glu': row(b_glu[i]),
        'w_out': w_out[i].astype(BF16), 'g_ffn': row(g_ffn[i]),
        'w_router': jnp.pad(w_router[i], ((0, 0), (0, LANES - N_EXPERTS))),
        'b_router': jnp.pad(row(b_router[i]), ((0, 0), (0, LANES - N_EXPERTS)),
                            constant_values=-jnp.inf),
        'w_gate': w_gate[i], 'b_gate': b_gate[i].reshape(N_EXPERTS, 1, D_MODEL),
        'w_up': w_up[i], 'b_up': b_up[i].reshape(N_EXPERTS, 1, D_MODEL),
        'w_down': w_down[i], 'b_down': b_down[i].reshape(N_EXPERTS, 1, D_MODEL),
        'g_ple': row(g_ple[i]), 'w_ple_gate': w_ple_gate[i].astype(BF16),
        'w_ple_proj': w_ple_proj[i].astype(BF16), 'g_final': row(g_final),
    }
    w['s5'] = {tc: [_s5_params(ssm_lambda_re[i, dr], ssm_lambda_im[i, dr], ssm_log_dt[i, dr],
                               ssm_b_re[i, dr], ssm_b_im[i, dr], ssm_c_re[i, dr], ssm_c_im[i, dr],
                               tc, reverse=(dr == 1)) for dr in range(2)] for tc in tcs}
    return w, lambda_init


def kernel(x_prompt, x_sample, p_prompt, p_sample, g_mix, w_in, lambda_q1, lambda_k1, lambda_q2, lambda_k2, g_subln, ssm_lambda_re, ssm_lambda_im, ssm_log_dt, ssm_b_re, ssm_b_im, ssm_c_re, ssm_c_im, ssm_d, w_glu, b_glu, w_out, g_ffn, w_router, b_router, w_gate, b_gate, w_up, b_up, w_down, b_down, g_ple, w_ple_gate, w_ple_proj, g_final):
    assert w_in.shape[0] == 1, "single-layer trunk"
    tcs = {_pick(x.shape[1], (256, 128)) for x in (x_prompt, x_sample)}
    w, lambda_init = _prepare(0, g_mix, w_in, lambda_q1, lambda_k1, lambda_q2, lambda_k2, g_subln,
                              ssm_lambda_re, ssm_lambda_im, ssm_log_dt, ssm_b_re, ssm_b_im, ssm_c_re,
                              ssm_c_im, ssm_d, w_glu, b_glu, w_out, g_ffn, w_router, b_router, w_gate,
                              b_gate, w_up, b_up, w_down, b_down, g_ple, w_ple_gate, w_ple_proj,
                              g_final, tcs)
    outs = []
    for x, p in ((x_prompt, p_prompt), (x_sample, p_sample)):
        tc = _pick(x.shape[1], (256, 128))
        wt = dict(w, s5=w['s5'][tc])
        outs.append(_trunk(x, p[0], wt, lambda_init))
    return tuple(outs)
```

```python
import functools
import math

import jax
import jax.numpy as jnp
import numpy as np
from jax import lax
from jax.experimental import pallas as pl
from jax.experimental.pallas import tpu as pltpu

F32 = jnp.float32
BF16 = jnp.bfloat16

D_MODEL = 1024
PLE_DIM = 256
ATTN_WIDTH = 512
SSM_WIDTH = 512
HEAD_W = 128
DIFF_HEAD_DIM = 64
N_HEADS = 4
SSM_GROUP = 16
N_GROUPS = 32
SSM_STATE = 64
N_STATE = N_GROUPS * SSM_STATE
N_EXPERTS = 32
TOP_K = 4
SWIGLU_LIMIT = 7.0
SWIGLU_ALPHA = 1.702
EPS = 1e-6
LANES = 128
TOKEN_TILE = 512

VMEM_LIMIT = 56 * 1024 * 1024


def _cparams(sem):
    return pltpu.CompilerParams(dimension_semantics=sem, vmem_limit_bytes=VMEM_LIMIT)


def _rms(x, g):
    return x * lax.rsqrt(jnp.mean(x * x, axis=-1, keepdims=True) + EPS) * g


LOG2E = 1.4426950408889634
NT_DIMS = (((1,), (1,)), ((), ()))


def _inproj_kernel(x_ref, g_ref, wqku_ref, wvt_ref, seg_ref, q_ref, k_ref, vt_ref, u_ref, nrm_ref):
    xn = _rms(x_ref[...], g_ref[...]).astype(BF16)
    proj = jnp.dot(xn, wqku_ref[...], preferred_element_type=F32)
    scale = LOG2E / math.sqrt(DIFF_HEAD_DIM)
    qk = jnp.concatenate([proj[:, :ATTN_WIDTH] * scale, proj[:, ATTN_WIDTH:2 * ATTN_WIDTH]], axis=1)
    q_ref[...] = qk[:, :ATTN_WIDTH].astype(BF16)
    k_ref[...] = qk[:, ATTN_WIDTH:].astype(BF16)
    u_ref[...] = proj[:, 2 * ATTN_WIDTH:]
    nsq = jnp.dot((qk * qk).astype(BF16), seg_ref[...], preferred_element_type=F32)
    nrm_ref[0] = jnp.max(nsq.reshape(nsq.shape[0] // 8, 8, LANES), axis=0)
    vt_ref[0] = lax.dot_general(wvt_ref[...], xn, NT_DIMS, preferred_element_type=F32).astype(BF16)


def _inproj(x, g_mix, w_qku_bf, w_vt_bf, tm):
    t = x.shape[0]
    row = lambda i: (i, 0)
    const = lambda i: (0, 0)
    col = np.arange(2 * ATTN_WIDTH)
    seg = np.zeros((2 * ATTN_WIDTH, LANES), np.float32)
    seg[col, col // DIFF_HEAD_DIM] = 1.0
    seg = jnp.asarray(seg, BF16)
    return pl.pallas_call(
        _inproj_kernel,
        grid=(t // tm,),
        in_specs=[pl.BlockSpec((tm, D_MODEL), row),
                  pl.BlockSpec((1, D_MODEL), const),
                  pl.BlockSpec(w_qku_bf.shape, const),
                  pl.BlockSpec(w_vt_bf.shape, const),
                  pl.BlockSpec(seg.shape, const)],
        out_specs=[pl.BlockSpec((tm, ATTN_WIDTH), row), pl.BlockSpec((tm, ATTN_WIDTH), row),
                   pl.BlockSpec((1, ATTN_WIDTH, tm), lambda i: (i, 0, 0)),
                   pl.BlockSpec((tm, SSM_WIDTH), row),
                   pl.BlockSpec((1, 8, LANES), lambda i: (i, 0, 0))],
        out_shape=[jax.ShapeDtypeStruct((t, ATTN_WIDTH), BF16), jax.ShapeDtypeStruct((t, ATTN_WIDTH), BF16),
                   jax.ShapeDtypeStruct((t // tm, ATTN_WIDTH, tm), BF16),
                   jax.ShapeDtypeStruct((t, SSM_WIDTH), F32),
                   jax.ShapeDtypeStruct((t // tm, 8, LANES), F32)],
        compiler_params=_cparams(("parallel",)),
        name="inproj",
    )(x, g_mix.reshape(1, D_MODEL), w_qku_bf, w_vt_bf, seg)


ONES_ROWS = 16
ACC_ROWS = HEAD_W + ONES_ROWS
POS_SPLIT = 16
N_BIAS_PARTS = 3
SKIP_BITS = 130.0
NORM_INFLATE = 1.0 + 2.0 ** -5


def _attn_kernel(sc_ref, ub_ref, q_ref, k_ref, vt_ref, qf_ref, kf_ref, g_ref, o_ref,
                 qa_ref, m_ref, acc_ref, sa_ref, sb_ref, pa_ref, pb_ref, ala_ref, alb_ref,
                 *, ts, out_scale):
    bi = pl.program_id(0)
    h = pl.program_id(1)
    qi = pl.program_id(2)
    nsub = k_ref.shape[1] // ts
    bslope = sc_ref[1 + h]

    q = q_ref[0]
    lane = lax.broadcasted_iota(jnp.int32, q.shape, 1)
    zero = jnp.zeros_like(q)
    qf = qf_ref[0]
    qa_ref[0] = jnp.concatenate([jnp.where(lane < DIFF_HEAD_DIM, q, zero), qf], axis=1)
    qa_ref[1] = jnp.concatenate([jnp.where(lane >= DIFF_HEAD_DIM, q, zero), qf], axis=1)
    ones = jnp.ones((ONES_ROWS, ts), BF16)

    def keys(j, variant):
        rows = pl.ds(pl.multiple_of(j * ts, ts), ts)
        return jnp.concatenate([k_ref[0, rows, :], kf_ref[0, variant]], axis=1)

    def values(j):
        return jnp.concatenate([vt_ref[j], ones], axis=0)

    kpos = lax.broadcasted_iota(jnp.int32, (ts, ts), 0)
    qpos = lax.broadcasted_iota(jnp.int32, (ts, ts), 1)
    bias = jnp.abs(qpos - kpos).astype(F32) * (-bslope)
    ka = keys(qi, 2)
    vta = values(qi)
    s_diag = [lax.dot_general(ka, qa_ref[st], NT_DIMS, preferred_element_type=F32) for st in range(2)]
    for st in range(2):
        s = s_diag[st] + bias
        m_new = jnp.max(s, axis=0, keepdims=True)
        p = jnp.exp2(s - m_new).astype(BF16)
        acc_ref[st] = jnp.dot(vta, p, preferred_element_type=F32)
        m_ref[st] = m_new

    ub = ub_ref[(bi * N_HEADS + h) * nsub + qi]
    slack = ub - jnp.min(m_ref[...]) + SKIP_BITS
    x = (slack * sc_ref[1 + N_HEADS + h] - 1.0) * (1.0 / ts)
    w = jnp.where(x < 0.0, 0, jnp.minimum(x, float(nsub)).astype(jnp.int32) + 1)
    lo = jnp.maximum(qi - w, 0)
    n_off = jnp.minimum(qi + w, nsub - 1) - lo

    def sub_index(n):
        j = lo + jnp.clip(n, 0, jnp.maximum(n_off - 1, 0))
        return j + (j >= qi).astype(jnp.int32)

    def qk(n, s_ref):
        j = sub_index(n)
        ka = keys(j, jnp.where(j < qi, 0, 1))
        for st in range(2):
            s_ref[st] = lax.dot_general(ka, qa_ref[st], NT_DIMS, preferred_element_type=F32)

    def softmax(n, s_ref, p_ref, al_ref, keep=None):
        j = sub_index(n)
        c = -bslope * (jnp.abs(qi - j) * ts).astype(F32)
        for st in range(2):
            m_old = m_ref[st]
            m_new = jnp.maximum(m_old, jnp.max(s_ref[st], axis=0, keepdims=True) + c)
            al_ref[st] = jnp.exp2(m_old - m_new)
            p = jnp.exp2(s_ref[st] - (m_new - c))
            if keep is not None:
                p = p * keep
            p_ref[st] = p.astype(BF16)
            m_ref[st] = m_new

    def pv(n, p_ref, al_ref):
        vta = values(sub_index(n))
        for st in range(2):
            acc_ref[st] = al_ref[st] * acc_ref[st] + jnp.dot(vta, p_ref[st], preferred_element_type=F32)

    @pl.when(n_off > 0)
    def _():
        for p_ref, al_ref in ((pa_ref, ala_ref), (pb_ref, alb_ref)):
            p_ref[...] = jnp.zeros_like(p_ref)
            al_ref[...] = jnp.ones_like(al_ref)
        qk(0, sa_ref)

        def pair(g, carry):
            n = 2 * g
            pv(n - 2, pa_ref, ala_ref)
            qk(n + 1, sb_ref)
            softmax(n, sa_ref, pa_ref, ala_ref)
            pv(n - 1, pb_ref, alb_ref)
            qk(n + 2, sa_ref)
            softmax(n + 1, sb_ref, pb_ref, alb_ref)
            return carry

        n_pairs = n_off // 2
        lax.fori_loop(0, n_pairs, pair, 0)
        pv(2 * n_pairs - 2, pa_ref, ala_ref)
        softmax(n_off - 1, sa_ref, pa_ref, ala_ref, keep=(n_off % 2).astype(F32))
        pv(2 * n_pairs - 1, pb_ref, alb_ref)
        pv(n_off - 1, pa_ref, ala_ref)

    lam = sc_ref[0]
    a1 = acc_ref[0]
    a2 = acc_ref[1]
    out_t = (a1[:HEAD_W] / a1[HEAD_W:HEAD_W + 1] - lam * (a2[:HEAD_W] / a2[HEAD_W:HEAD_W + 1]))
    ms = jnp.mean(out_t * out_t, axis=0, keepdims=True)
    out_t = out_t * lax.rsqrt(ms + EPS) * (g_ref[...] * out_scale)
    o_ref[0] = out_t.T.astype(o_ref.dtype)


def _attn_features(bparts, ts):
    pos = jnp.arange(ts, dtype=jnp.int32)
    hi = ((pos // POS_SPLIT) * POS_SPLIT).astype(F32)[None, :, None]
    lo = (pos % POS_SPLIT).astype(F32)[None, :, None]
    bp = bparts[:, None, :]
    one = jnp.ones_like(bp)
    qf = jnp.stack([bp + 0 * hi, bp + 0 * hi, hi * one, lo * one], axis=-1)
    kf = jnp.stack([hi * one, lo * one, -bp + 0 * hi, -bp + 0 * hi], axis=-1)
    pad = lambda f: jnp.pad(f.reshape(N_HEADS, ts, 4 * N_BIAS_PARTS),
                            ((0, 0), (0, 0), (0, HEAD_W - 4 * N_BIAS_PARTS)))
    qf, kf = pad(qf), pad(kf)
    kf = jnp.stack([kf, -kf, jnp.zeros_like(kf)], axis=1)
    return qf.astype(BF16), kf.astype(BF16)


def _score_bounds(nrm, b, nsub):
    nrm = jnp.max(nrm, axis=1)[:, :2 * N_HEADS * 2].reshape(b, nsub, 2, N_HEADS, 2)
    q2 = nrm[:, :, 0]
    k2 = jnp.max(nrm[:, :, 1], axis=1)
    ub = jnp.max(jnp.sqrt(q2 * k2[:, None]), axis=-1) * NORM_INFLATE
    return jnp.transpose(ub, (0, 2, 1)).reshape(-1)


def _attention(q, k, vt, scal, ub, qf, kf, g_subln, lambda_init, ts):
    b, l, _ = q.shape
    nsub = l // ts
    kernel = functools.partial(_attn_kernel, ts=ts, out_scale=1.0 - lambda_init)
    return pl.pallas_call(
        kernel,
        grid=(b, N_HEADS, nsub),
        in_specs=[pl.BlockSpec(memory_space=pltpu.SMEM),
                  pl.BlockSpec(memory_space=pltpu.SMEM),
                  pl.BlockSpec((1, ts, HEAD_W), lambda bi, h, qi: (bi, qi, h)),
                  pl.BlockSpec((1, l, HEAD_W), lambda bi, h, qi: (bi, 0, h)),
                  pl.BlockSpec((nsub, HEAD_W, ts), lambda bi, h, qi: (bi, h, 0)),
                  pl.BlockSpec((1, ts, HEAD_W), lambda bi, h, qi: (h, 0, 0)),
                  pl.BlockSpec((1, 3, ts, HEAD_W), lambda bi, h, qi: (h, 0, 0, 0)),
                  pl.BlockSpec((HEAD_W, 1), lambda bi, h, qi: (0, 0))],
        out_specs=pl.BlockSpec((1, ts, HEAD_W), lambda bi, h, qi: (bi, qi, h)),
        out_shape=jax.ShapeDtypeStruct((b, l, ATTN_WIDTH), BF16),
        scratch_shapes=[pltpu.VMEM((2, ts, 2 * HEAD_W), BF16),
                        pltpu.VMEM((2, 1, ts), F32),
                        pltpu.VMEM((2, ACC_ROWS, ts), F32),
                        pltpu.VMEM((2, ts, ts), F32), pltpu.VMEM((2, ts, ts), F32),
                        pltpu.VMEM((2, ts, ts), BF16), pltpu.VMEM((2, ts, ts), BF16),
                        pltpu.VMEM((2, 1, ts), F32), pltpu.VMEM((2, 1, ts), F32)],
        compiler_params=_cparams(("parallel", "parallel", "arbitrary")),
        name="attn",
    )(scal, ub, q, k, vt, qf, kf, g_subln.reshape(HEAD_W, 1))


SCAN_LANES = 512
S5_BLOCKS = 2


def _s5_kernel(u_ref, pm_ref, pmt_ref, bb_ref, cc_ref, a_ref, at_ref, apow_ref, y_ref,
               xs_ref, xb_ref, carry_ref, *, tc, reverse):
    tc8 = tc // 8
    c = pl.program_id(1)

    @pl.when(c == 0)
    def _():
        carry_ref[...] = jnp.zeros_like(carry_ref)

    up = jnp.dot(pm_ref[...], u_ref[0].astype(BF16), preferred_element_type=F32).astype(BF16)
    cw = SSM_WIDTH // S5_BLOCKS
    sw = N_STATE // S5_BLOCKS
    for hb in range(S5_BLOCKS):
        bu = jnp.dot(up[:, hb * cw:(hb + 1) * cw], bb_ref[hb], preferred_element_type=F32)
        xs_ref[:, hb * sw:(hb + 1) * sw] = bu[:, :sw]
        xs_ref[:, N_STATE + hb * sw:N_STATE + (hb + 1) * sw] = bu[:, sw:]

    for cb in range(N_STATE // SCAN_LANES):
        lo = cb * SCAN_LANES
        re_sl = pl.ds(lo, SCAN_LANES)
        im_sl = pl.ds(N_STATE + lo, SCAN_LANES)
        ar = jnp.broadcast_to(a_ref[0:1, lo:lo + SCAN_LANES], (8, SCAN_LANES))
        ai = jnp.broadcast_to(a_ref[1:2, lo:lo + SCAN_LANES], (8, SCAN_LANES))

        def step(i, carry):
            xr, xi = carry
            r = (tc8 - 1 - i) if reverse else i
            rows = pl.ds(pl.multiple_of(r * 8, 8), 8)
            nr = ar * xr - ai * xi + xs_ref[rows, re_sl]
            ni = ar * xi + ai * xr + xs_ref[rows, im_sl]
            xs_ref[rows, re_sl] = nr
            xs_ref[rows, im_sl] = ni
            return nr, ni

        z = jnp.zeros((8, SCAN_LANES), F32)
        lax.fori_loop(0, tc8, step, (z, z), unroll=4)

    last = 0 if reverse else (tc8 - 1) * 8
    e_re = xs_ref[last:last + 8, 0:N_STATE]
    e_im = xs_ref[last:last + 8, N_STATE:2 * N_STATE]
    atr = at_ref[0:1, :]
    ati = at_ref[1:2, :]
    cr = carry_ref[0:1, :]
    ci = carry_ref[1:2, :]
    cin_r = [None] * 8
    cin_i = [None] * 8
    order = range(7, -1, -1) if reverse else range(8)
    for j in order:
        cin_r[j] = cr
        cin_i[j] = ci
        nr = atr * cr - ati * ci + e_re[j:j + 1]
        ni = atr * ci + ati * cr + e_im[j:j + 1]
        cr, ci = nr, ni
    carry_ref[0:1, :] = cr
    carry_ref[1:2, :] = ci
    cin_r = jnp.concatenate(cin_r + cin_r, axis=0)
    cin_i = jnp.concatenate(cin_i + cin_i, axis=0)

    def fix(i, _):
        rows = pl.ds(pl.multiple_of(i * 16, 16), 16)
        pr = apow_ref[rows, 0:N_STATE]
        pi = apow_ref[rows, N_STATE:2 * N_STATE]
        xr = xs_ref[rows, 0:N_STATE] + pr * cin_r - pi * cin_i
        xi = xs_ref[rows, N_STATE:2 * N_STATE] + pr * cin_i + pi * cin_r
        xb_ref[rows, 0:N_STATE] = xr.astype(BF16)
        xb_ref[rows, N_STATE:2 * N_STATE] = xi.astype(BF16)
        return 0

    lax.fori_loop(0, tc // 16, fix, 0)

    yp = jnp.concatenate(
        [jnp.dot(xb_ref[:, hb * sw:(hb + 1) * sw], cc_ref[hb, 0], preferred_element_type=F32)
         + jnp.dot(xb_ref[:, N_STATE + hb * sw:N_STATE + (hb + 1) * sw], cc_ref[hb, 1],
                   preferred_element_type=F32) for hb in range(S5_BLOCKS)], axis=1)
    hi = yp.astype(BF16)
    lo_part = (yp - hi.astype(F32)).astype(BF16)
    y_ref[0] = (jnp.dot(pmt_ref[...], hi, preferred_element_type=F32)
                + jnp.dot(pmt_ref[...], lo_part, preferred_element_type=F32))


def _s5_direction(u, pm, pmt, bb, cc, a, at, apow, tc, reverse):
    b, l, _ = u.shape
    nc = l // tc
    cmap = (lambda bi, c: (bi, nc - 1 - c, 0)) if reverse else (lambda bi, c: (bi, c, 0))
    full = lambda arr: pl.BlockSpec(arr.shape, lambda bi, c: (0,) * arr.ndim)
    kernel = functools.partial(_s5_kernel, tc=tc, reverse=reverse)
    return pl.pallas_call(
        kernel,
        grid=(b, nc),
        in_specs=[pl.BlockSpec((1, tc, SSM_WIDTH), cmap), full(pm), full(pmt), full(bb), full(cc),
                  full(a), full(at), full(apow)],
        out_specs=pl.BlockSpec((1, tc, SSM_WIDTH), cmap),
        out_shape=jax.ShapeDtypeStruct((b, l, SSM_WIDTH), F32),
        scratch_shapes=[pltpu.VMEM((tc, 2 * N_STATE), F32), pltpu.VMEM((tc, 2 * N_STATE), BF16),
                        pltpu.VMEM((2, N_STATE), F32)],
        compiler_params=_cparams(("parallel", "arbitrary")),
        name="s5_bwd" if reverse else "s5_fwd",
    )(u, pm, pmt, bb, cc, a, at, apow)


def _s5_params(lam_re, lam_im, log_dt, b_re, b_im, c_re, c_im, tc, reverse):
    tc8 = tc // 8
    dt = jnp.exp(log_dt)[:, None]
    mag = jnp.exp(lam_re * dt)
    ab_re = mag * jnp.cos(lam_im * dt)
    ab_im = mag * jnp.sin(lam_im * dt)
    den = lam_re * lam_re + lam_im * lam_im
    nr, ni = ab_re - 1.0, ab_im
    co_re = (nr * lam_re + ni * lam_im) / den
    co_im = (ni * lam_re - nr * lam_im) / den
    bb_re = co_re[..., None] * b_re - co_im[..., None] * b_im
    bb_im = co_re[..., None] * b_im + co_im[..., None] * b_re
    gb = N_GROUPS // S5_BLOCKS
    cw, sw = SSM_WIDTH // S5_BLOCKS, N_STATE // S5_BLOCKS
    eye = jnp.eye(gb, dtype=F32)
    blk_in = lambda w: jnp.einsum('bgph,gk->bghkp', w.reshape(S5_BLOCKS, gb, SSM_STATE, SSM_GROUP),
                                  eye).reshape(S5_BLOCKS, cw, sw)
    bb = jnp.concatenate([blk_in(bb_re), blk_in(bb_im)], axis=2).astype(BF16)
    blk_out = lambda w: jnp.einsum('bghp,gk->bgpkh', w.reshape(S5_BLOCKS, gb, SSM_GROUP, SSM_STATE),
                                   eye).reshape(S5_BLOCKS, sw, cw)
    cc = jnp.stack([blk_out(c_re), -blk_out(c_im)], axis=1).astype(BF16)
    a = jnp.stack([ab_re.reshape(-1), ab_im.reshape(-1)])
    n = jnp.arange(1, tc8 + 1, dtype=F32)[:, None, None]
    pmag = jnp.exp(n * (lam_re * dt)[None])
    ang = n * (lam_im * dt)[None]
    pw_re = (pmag * jnp.cos(ang)).reshape(tc8, N_STATE)
    pw_im = (pmag * jnp.sin(ang)).reshape(tc8, N_STATE)
    at = jnp.stack([pw_re[-1], pw_im[-1]])
    if reverse:
        pw_re, pw_im = pw_re[::-1], pw_im[::-1]
    apow = jnp.concatenate([jnp.repeat(pw_re, 8, axis=0), jnp.repeat(pw_im, 8, axis=0)], axis=1)
    return bb, cc, a, at, apow


def _perm_matrices(tc):
    tc8 = tc // 8
    i = np.arange(tc)
    src = (i % 8) * tc8 + i // 8
    pm = np.zeros((tc, tc), np.float32)
    pm[i, src] = 1.0
    return jnp.asarray(pm, BF16), jnp.asarray(pm.T, BF16)


def _mix_kernel(h_ref, a_ref, u_ref, yf_ref, yb_ref, d_ref, wglu_ref, bglu_ref, wout_ref,
                gffn_ref, wr_ref, br_ref, tri_ref, h1_ref, hn_ref, ti_ref, tg_ref, counts_ref, cnt_ref):
    y = d_ref[...] * u_ref[...] + yf_ref[...] + yb_ref[...]
    y = jax.nn.gelu(y)
    z = jnp.dot(y.astype(BF16), wglu_ref[...], preferred_element_type=F32) + bglu_ref[...]
    s = y * jax.nn.sigmoid(z)
    h1 = (h_ref[...]
          + jnp.dot(a_ref[...], wout_ref[0:ATTN_WIDTH, :], preferred_element_type=F32)
          + jnp.dot(s.astype(BF16), wout_ref[ATTN_WIDTH:, :], preferred_element_type=F32))
    h1_ref[...] = h1
    hn = _rms(h1, gffn_ref[...])
    hn_ref[...] = hn
    logits = jnp.dot(hn, wr_ref[...], preferred_element_type=F32,
                     precision=lax.Precision.HIGHEST) + br_ref[...]
    lane = lax.broadcasted_iota(jnp.int32, logits.shape, 1)
    neg = jnp.float32(-jnp.inf)
    work = logits
    ti = jnp.zeros(logits.shape, jnp.int32)
    tv = jnp.full(logits.shape, neg, F32)
    picked = jnp.zeros(logits.shape, F32)
    idxs = []
    for kk in range(TOP_K):
        mx = jnp.max(work, axis=-1, keepdims=True)
        idx = jnp.min(jnp.where(work == mx, lane, LANES), axis=-1, keepdims=True)
        idxs.append(idx)
        ti = jnp.where(lane == kk, idx, ti)
        tv = jnp.where(lane == kk, mx, tv)
        picked = jnp.where(lane == idx, 1.0, picked)
        work = jnp.where(lane == idx, neg, work)
    ex = jnp.exp(tv - jnp.max(tv, axis=-1, keepdims=True))
    tg_ref[...] = ex / jnp.sum(ex, axis=-1, keepdims=True)

    @pl.when(pl.program_id(0) == 0)
    def _():
        cnt_ref[...] = jnp.zeros_like(cnt_ref)

    before = jnp.dot(tri_ref[...], picked.astype(BF16), preferred_element_type=F32) + cnt_ref[0:1, :]
    for kk in range(TOP_K):
        rank = jnp.sum(jnp.where(lane == idxs[kk], before, 0.0), axis=-1, keepdims=True)
        ti = jnp.where(lane == TOP_K + kk, rank.astype(jnp.int32), ti)
    ti_ref[...] = ti
    cnt_ref[...] = cnt_ref[...] + jnp.sum(picked, axis=0, keepdims=True)
    counts_ref[...] = cnt_ref[...]


def _mix(h, a, u, yf, yb, d, wglu_bf, bglu, wout_bf, g_ffn, wr_pad, br_pad, tm):
    t = h.shape[0]
    row = lambda i: (i, 0)
    const = lambda i: (0, 0)
    full = lambda arr: pl.BlockSpec(arr.shape, const)
    tri = jnp.asarray(np.tril(np.ones((tm, tm), np.float32), -1), BF16)
    return pl.pallas_call(
        _mix_kernel,
        grid=(t // tm,),
        in_specs=[pl.BlockSpec((tm, D_MODEL), row), pl.BlockSpec((tm, ATTN_WIDTH), row),
                  pl.BlockSpec((tm, SSM_WIDTH), row), pl.BlockSpec((tm, SSM_WIDTH), row),
                  pl.BlockSpec((tm, SSM_WIDTH), row),
                  full(d), full(wglu_bf), full(bglu), full(wout_bf), full(g_ffn), full(wr_pad),
                  full(br_pad), full(tri)],
        out_specs=[pl.BlockSpec((tm, D_MODEL), row), pl.BlockSpec((tm, D_MODEL), row),
                   pl.BlockSpec((tm, LANES), row), pl.BlockSpec((tm, LANES), row),
                   pl.BlockSpec((8, LANES), const)],
        out_shape=[jax.ShapeDtypeStruct((t, D_MODEL), F32), jax.ShapeDtypeStruct((t, D_MODEL), F32),
                   jax.ShapeDtypeStruct((t, LANES), jnp.int32), jax.ShapeDtypeStruct((t, LANES), F32),
                   jax.ShapeDtypeStruct((8, LANES), F32)],
        scratch_shapes=[pltpu.VMEM((8, LANES), F32)],
        compiler_params=_cparams(("arbitrary",)),
        name="mix",
    )(h, a, u, yf, yb, d, wglu_bf, bglu, wout_bf, g_ffn, wr_pad, br_pad, tri)


ROWS_PER_ISSUE = 4

def _dispatch_kernel(dest_ref, pad_start_ref, pad_len_ref, hn_ref, xs_ref, zrow_ref, sem, zsem, *, tm):
    @pl.when(pl.program_id(0) == 0)
    def _():
        zrow_ref[...] = jnp.zeros_like(zrow_ref)

        def zero_copy(row):
            return pltpu.make_async_copy(zrow_ref.at[pl.ds(0, 1), :], xs_ref.at[pl.ds(row, 1), :], zsem)

        def start_span(e, _):
            base = pad_start_ref[e]
            lax.fori_loop(0, pad_len_ref[e], lambda r, c: (zero_copy(base + r).start(), c)[1], 0)
            return 0

        def wait_span(e, _):
            lax.fori_loop(0, pad_len_ref[e], lambda r, c: (zero_copy(0).wait(), c)[1], 0)
            return 0

        lax.fori_loop(0, N_EXPERTS + 1, start_span, 0)
        lax.fori_loop(0, N_EXPERTS + 1, wait_span, 0)

    def issue(g, _):
        for rr in range(ROWS_PER_ISSUE):
            r = g * ROWS_PER_ISSUE + rr
            for kk in range(TOP_K):
                pltpu.make_async_copy(hn_ref.at[pl.ds(r, 1), :],
                                      xs_ref.at[pl.ds(dest_ref[r * TOP_K + kk], 1), :], sem).start()
        return 0

    lax.fori_loop(0, tm // ROWS_PER_ISSUE, issue, 0)

    def drain(g, _):
        for _unused in range(ROWS_PER_ISSUE * TOP_K):
            pltpu.make_async_copy(hn_ref.at[pl.ds(0, 1), :], xs_ref.at[pl.ds(0, 1), :], sem).wait()
        return 0

    lax.fori_loop(0, tm // ROWS_PER_ISSUE, drain, 0)


def _dispatch(dest_flat, pad_start, pad_len, hn, n_pad, tm):
    t = hn.shape[0]
    kernel = functools.partial(_dispatch_kernel, tm=tm)
    return pl.pallas_call(
        kernel,
        grid=(t // tm,),
        in_specs=[pl.BlockSpec((tm * TOP_K,), lambda i: (i,), memory_space=pltpu.SMEM),
                  pl.BlockSpec(memory_space=pltpu.SMEM), pl.BlockSpec(memory_space=pltpu.SMEM),
                  pl.BlockSpec((tm, D_MODEL), lambda i: (i, 0))],
        out_specs=pl.BlockSpec(memory_space=pl.ANY),
        out_shape=jax.ShapeDtypeStruct((n_pad, D_MODEL), F32),
        scratch_shapes=[pltpu.VMEM((8, D_MODEL), F32), pltpu.SemaphoreType.DMA(()),
                        pltpu.SemaphoreType.DMA(())],
        compiler_params=_cparams(("arbitrary",)),
        name="dispatch",
    )(dest_flat, pad_start, pad_len, hn)


def _experts_kernel(be_ref, x_ref, wg_ref, bg_ref, wu_ref, bu_ref, wd_ref, bd_ref, y_ref,
                    wgb_ref, wub_ref, wdb_ref):
    i = pl.program_id(0)

    @pl.when((i == 0) | (be_ref[i] != be_ref[jnp.maximum(i - 1, 0)]))
    def _():
        wgb_ref[...] = wg_ref[0].astype(BF16)
        wub_ref[...] = wu_ref[0].astype(BF16)
        wdb_ref[...] = wd_ref[0].astype(BF16)

    x = x_ref[...].astype(BF16)
    gt = jnp.minimum(jnp.dot(x, wgb_ref[...], preferred_element_type=F32) + bg_ref[0], SWIGLU_LIMIT)
    up = jnp.clip(jnp.dot(x, wub_ref[...], preferred_element_type=F32) + bu_ref[0],
                  -SWIGLU_LIMIT, SWIGLU_LIMIT)
    hh = (up + 1.0) * (gt * jax.nn.sigmoid(SWIGLU_ALPHA * gt))
    y_ref[...] = jnp.dot(hh.astype(BF16), wdb_ref[...], preferred_element_type=F32) + bd_ref[0]


def _experts(blk_expert, xs, wg, bg, wu, bu, wd, bd, eb):
    n_pad = xs.shape[0]
    row = lambda i, be: (i, 0)
    wmap = lambda i, be: (be[i], 0, 0)
    wspec = pl.BlockSpec((1, D_MODEL, D_MODEL), wmap)
    bspec = pl.BlockSpec((1, 1, D_MODEL), wmap)
    grid_spec = pltpu.PrefetchScalarGridSpec(
        num_scalar_prefetch=1,
        grid=(n_pad // eb,),
        in_specs=[pl.BlockSpec((eb, D_MODEL), row), wspec, bspec, wspec, bspec, wspec, bspec],
        out_specs=pl.BlockSpec((eb, D_MODEL), row),
        scratch_shapes=[pltpu.VMEM((D_MODEL, D_MODEL), BF16)] * 3,
    )
    return pl.pallas_call(
        _experts_kernel,
        grid_spec=grid_spec,
        out_shape=jax.ShapeDtypeStruct((n_pad, D_MODEL), F32),
        compiler_params=_cparams(("arbitrary",)),
        name="experts",
    )(blk_expert, xs, wg, bg, wu, bu, wd, bd)


def _tail_kernel(dest_ref, h1_ref, tg_ref, p_ref, ys_ref, gple_ref, wpg_ref, wpp_ref, gfin_ref,
                 o_ref, buf_ref, sem, *, tm):
    def issue(g, _):
        for rr in range(ROWS_PER_ISSUE):
            r = g * ROWS_PER_ISSUE + rr
            for kk in range(TOP_K):
                pltpu.make_async_copy(ys_ref.at[pl.ds(dest_ref[r * TOP_K + kk], 1), :],
                                      buf_ref.at[kk, pl.ds(r, 1), :], sem).start()
        return 0

    lax.fori_loop(0, tm // ROWS_PER_ISSUE, issue, 0)

    def drain(g, _):
        for _unused in range(ROWS_PER_ISSUE * TOP_K):
            pltpu.make_async_copy(ys_ref.at[pl.ds(0, 1), :], buf_ref.at[0, pl.ds(0, 1), :], sem).wait()
        return 0

    lax.fori_loop(0, tm // ROWS_PER_ISSUE, drain, 0)

    tg = tg_ref[...]
    h2 = h1_ref[...]
    for kk in range(TOP_K):
        h2 = h2 + tg[:, kk:kk + 1] * buf_ref[kk]
    gate = jax.nn.sigmoid(jnp.dot(_rms(h2, gple_ref[...]).astype(BF16), wpg_ref[...],
                                  preferred_element_type=F32))
    proj = jnp.dot(p_ref[...].astype(BF16), wpp_ref[...], preferred_element_type=F32)
    o_ref[...] = _rms(h2 + gate * proj, gfin_ref[...])


def _tail(dest_flat, h1, tg, p, ys, g_ple, wpg_bf, wpp_bf, g_final, tm):
    t = h1.shape[0]
    row = lambda i: (i, 0)
    const = lambda i: (0, 0)
    full = lambda arr: pl.BlockSpec(arr.shape, const)
    kernel = functools.partial(_tail_kernel, tm=tm)
    return pl.pallas_call(
        kernel,
        grid=(t // tm,),
        in_specs=[pl.BlockSpec((tm * TOP_K,), lambda i: (i,), memory_space=pltpu.SMEM),
                  pl.BlockSpec((tm, D_MODEL), row), pl.BlockSpec((tm, LANES), row),
                  pl.BlockSpec((tm, PLE_DIM), row),
                  pl.BlockSpec(memory_space=pl.ANY),
                  full(g_ple), full(wpg_bf), full(wpp_bf), full(g_final)],
        out_specs=pl.BlockSpec((tm, D_MODEL), row),
        out_shape=jax.ShapeDtypeStruct((t, D_MODEL), F32),
        scratch_shapes=[pltpu.VMEM((TOP_K, tm, D_MODEL), F32), pltpu.SemaphoreType.DMA(())],
        compiler_params=_cparams(("arbitrary",)),
        name="tail",
    )(dest_flat, h1, tg, p, ys, g_ple, wpg_bf, wpp_bf, g_final)


def _routing(ti, counts, eb):
    t = ti.shape[0]
    top_i, rank = ti[:, :TOP_K], ti[:, TOP_K:2 * TOP_K]
    counts = counts[0, :N_EXPERTS].astype(jnp.int32)
    padded = ((counts + eb - 1) // eb) * eb
    pends = jnp.cumsum(padded)
    pstarts = pends - padded
    onehot = top_i[:, :, None] == jnp.arange(N_EXPERTS, dtype=jnp.int32)
    dest = rank + jnp.sum(jnp.where(onehot, pstarts, 0), axis=-1)
    n_blocks = -(-(t * TOP_K + N_EXPERTS * (eb - 1)) // eb)
    blk_start = jnp.arange(n_blocks, dtype=jnp.int32) * eb
    blk_expert = jnp.minimum(jnp.sum(blk_start[:, None] >= pends[None, :], axis=1), N_EXPERTS - 1)
    n_pad = n_blocks * eb
    pad_start = jnp.concatenate([pstarts + counts, pends[-1:]]).astype(jnp.int32)
    pad_len = jnp.concatenate([padded - counts, n_pad - pends[-1:]]).astype(jnp.int32)
    return (dest.reshape(-1).astype(jnp.int32), blk_expert.astype(jnp.int32), pad_start, pad_len, n_pad)


def _pick(n, prefs):
    for c in prefs:
        if n % c == 0:
            return c
    raise ValueError(f"no tile for {n}")


def _trunk(x, p, w, lambda_init):
    b, l, _ = x.shape
    t = b * l
    tm = TOKEN_TILE
    assert l % tm == 0
    tc = _pick(l, (256, 128))
    eb = 256

    q, k, vt, u, nrm = _inproj(x.reshape(t, D_MODEL), w['g_mix'], w['w_qku'], w['w_vt'], tm)
    a = _attention(q.reshape(b, l, -1), k.reshape(b, l, -1), vt, w['attn_scal'],
                   _score_bounds(nrm, b, l // tm), w['attn_qf'], w['attn_kf'], w['g_subln'],
                   lambda_init, tm)
    u3 = u.reshape(b, l, SSM_WIDTH)
    pm, pmt = _perm_matrices(tc)
    ys = [_s5_direction(u3, pm, pmt, *w['s5'][dr], tc, reverse=(dr == 1)) for dr in range(2)]
    h1, hn, ti, tg, counts = _mix(x.reshape(t, D_MODEL), a.reshape(t, ATTN_WIDTH), u,
                                  ys[0].reshape(t, SSM_WIDTH), ys[1].reshape(t, SSM_WIDTH),
                                  w['ssm_d'], w['w_glu'], w['b_glu'], w['w_out'], w['g_ffn'],
                                  w['w_router'], w['b_router'], tm)
    dest, blk_expert, pad_start, pad_len, n_pad = _routing(ti, counts, eb)
    xs = _dispatch(dest, pad_start, pad_len, hn, n_pad, tm // 2)
    ye = _experts(blk_expert, xs, w['w_gate'], w['b_gate'], w['w_up'], w['b_up'],
                  w['w_down'], w['b_down'], eb)
    out = _tail(dest, h1, tg, p.reshape(t, PLE_DIM), ye, w['g_ple'], w['w_ple_gate'],
                w['w_ple_proj'], w['g_final'], tm // 2)
    return out.reshape(b, l, D_MODEL)


def _prepare(i, g_mix, w_in, lambda_q1, lambda_k1, lambda_q2, lambda_k2, g_subln, ssm_lambda_re,
             ssm_lambda_im, ssm_log_dt, ssm_b_re, ssm_b_im, ssm_c_re, ssm_c_im, ssm_d, w_glu, b_glu,
             w_out, g_ffn, w_router, b_router, w_gate, b_gate, w_up, b_up, w_down, b_down, g_ple,
             w_ple_gate, w_ple_proj, g_final, tcs):
    lambda_init = 0.8 - 0.6 * math.exp(-0.3 * i)
    lam = (jnp.exp(jnp.sum(lambda_q1[i] * lambda_k1[i]))
           - jnp.exp(jnp.sum(lambda_q2[i] * lambda_k2[i])) + lambda_init)
    slopes = jnp.exp2(-8.0 * jnp.arange(1, N_HEADS + 1, dtype=F32) / N_HEADS)
    bslopes = slopes * LOG2E
    parts, rest = [], bslopes
    for _ in range(N_BIAS_PARTS):
        part = rest.astype(BF16).astype(F32)
        parts.append(part)
        rest = rest - part
    attn_qf, attn_kf = _attn_features(jnp.stack(parts, axis=1), TOKEN_TILE)
    w_qku = jnp.concatenate([w_in[i][:, :2 * ATTN_WIDTH], w_in[i][:, 3 * ATTN_WIDTH:]], axis=1)
    row = lambda vec: vec.reshape(1, -1).astype(F32)
    w = {
        'g_mix': g_mix[i], 'w_qku': w_qku.astype(BF16),
        'w_vt': w_in[i][:, 2 * ATTN_WIDTH:3 * ATTN_WIDTH].T.astype(BF16),
        'attn_scal': jnp.concatenate([lam.reshape(1), bslopes, 1.0 / bslopes]).astype(F32),
        'attn_qf': attn_qf, 'attn_kf': attn_kf,
        'g_subln': g_subln[i],
        'ssm_d': row(ssm_d[i]), 'w_glu': w_glu[i].astype(BF16), 'b_glu': row(b_glu[i]),
        'w_out': w_out[i].astype(BF16), 'g_ffn': row(g_ffn[i]),
        'w_router': jnp.pad(w_router[i], ((0, 0), (0, LANES - N_EXPERTS))),
        'b_router': jnp.pad(row(b_router[i]), ((0, 0), (0, LANES - N_EXPERTS)),
                            constant_values=-jnp.inf),
        'w_gate': w_gate[i], 'b_gate': b_gate[i].reshape(N_EXPERTS, 1, D_MODEL),
        'w_up': w_up[i], 'b_up': b_up[i].reshape(N_EXPERTS, 1, D_MODEL),
        'w_down': w_down[i], 'b_down': b_down[i].reshape(N_EXPERTS, 1, D_MODEL),
        'g_ple': row(g_ple[i]), 'w_ple_gate': w_ple_gate[i].astype(BF16),
        'w_ple_proj': w_ple_proj[i].astype(BF16), 'g_final': row(g_final),
    }
    w['s5'] = {tc: [_s5_params(ssm_lambda_re[i, dr], ssm_lambda_im[i, dr], ssm_log_dt[i, dr],
                               ssm_b_re[i, dr], ssm_b_im[i, dr], ssm_c_re[i, dr], ssm_c_im[i, dr],
                               tc, reverse=(dr == 1)) for dr in range(2)] for tc in tcs}
    return w, lambda_init


def kernel(x_prompt, x_sample, p_prompt, p_sample, g_mix, w_in, lambda_q1, lambda_k1, lambda_q2, lambda_k2, g_subln, ssm_lambda_re, ssm_lambda_im, ssm_log_dt, ssm_b_re, ssm_b_im, ssm_c_re, ssm_c_im, ssm_d, w_glu, b_glu, w_out, g_ffn, w_router, b_router, w_gate, b_gate, w_up, b_up, w_down, b_down, g_ple, w_ple_gate, w_ple_proj, g_final):
    assert w_in.shape[0] == 1, "single-layer trunk"
    tcs = {_pick(x.shape[1], (256, 128)) for x in (x_prompt, x_sample)}
    w, lambda_init = _prepare(0, g_mix, w_in, lambda_q1, lambda_k1, lambda_q2, lambda_k2, g_subln,
                              ssm_lambda_re, ssm_lambda_im, ssm_log_dt, ssm_b_re, ssm_b_im, ssm_c_re,
                              ssm_c_im, ssm_d, w_glu, b_glu, w_out, g_ffn, w_router, b_router, w_gate,
                              b_gate, w_up, b_up, w_down, b_down, g_ple, w_ple_gate, w_ple_proj,
                              g_final, tcs)
    outs = []
    for x, p in ((x_prompt, p_prompt), (x_sample, p_sample)):
        tc = _pick(x.shape[1], (256, 128))
        wt = dict(w, s5=w['s5'][tc])
        outs.append(_trunk(x, p[0], wt, lambda_init))
    return tuple(outs)
```

```python
import functools
import math

import jax
import jax.numpy as jnp
import numpy as np
from jax import lax
from jax.experimental import pallas as pl
from jax.experimental.pallas import tpu as pltpu

F32 = jnp.float32
BF16 = jnp.bfloat16

D_MODEL = 1024
PLE_DIM = 256
ATTN_WIDTH = 512
SSM_WIDTH = 512
HEAD_W = 128
DIFF_HEAD_DIM = 64
N_HEADS = 4
SSM_GROUP = 16
N_GROUPS = 32
SSM_STATE = 64
N_STATE = N_GROUPS * SSM_STATE
N_EXPERTS = 32
TOP_K = 4
SWIGLU_LIMIT = 7.0
SWIGLU_ALPHA = 1.702
EPS = 1e-6
LANES = 128
TOKEN_TILE = 512

VMEM_LIMIT = 56 * 1024 * 1024


def _cparams(sem):
    return pltpu.CompilerParams(dimension_semantics=sem, vmem_limit_bytes=VMEM_LIMIT)


def _rms(x, g):
    return x * lax.rsqrt(jnp.mean(x * x, axis=-1, keepdims=True) + EPS) * g


LOG2E = 1.4426950408889634
NT_DIMS = (((1,), (1,)), ((), ()))


def _inproj_kernel(x_ref, g_ref, wqku_ref, wvt_ref, seg_ref, q_ref, k_ref, vt_ref, u_ref, nrm_ref):
    xn = _rms(x_ref[...], g_ref[...]).astype(BF16)
    proj = jnp.dot(xn, wqku_ref[...], preferred_element_type=F32)
    scale = LOG2E / math.sqrt(DIFF_HEAD_DIM)
    qk = jnp.concatenate([proj[:, :ATTN_WIDTH] * scale, proj[:, ATTN_WIDTH:2 * ATTN_WIDTH]], axis=1)
    q_ref[...] = qk[:, :ATTN_WIDTH].astype(BF16)
    k_ref[...] = qk[:, ATTN_WIDTH:].astype(BF16)
    u_ref[...] = proj[:, 2 * ATTN_WIDTH:]
    nsq = jnp.dot((qk * qk).astype(BF16), seg_ref[...], preferred_element_type=F32)
    nrm_ref[0] = jnp.max(nsq.reshape(nsq.shape[0] // 8, 8, LANES), axis=0)
    vt_ref[0] = lax.dot_general(wvt_ref[...], xn, NT_DIMS, preferred_element_type=F32).astype(BF16)


def _inproj(x, g_mix, w_qku_bf, w_vt_bf, tm):
    t = x.shape[0]
    row = lambda i: (i, 0)
    const = lambda i: (0, 0)
    col = np.arange(2 * ATTN_WIDTH)
    seg = np.zeros((2 * ATTN_WIDTH, LANES), np.float32)
    seg[col, col // DIFF_HEAD_DIM] = 1.0
    seg = jnp.asarray(seg, BF16)
    return pl.pallas_call(
        _inproj_kernel,
        grid=(t // tm,),
        in_specs=[pl.BlockSpec((tm, D_MODEL), row),
                  pl.BlockSpec((1, D_MODEL), const),
                  pl.BlockSpec(w_qku_bf.shape, const),
                  pl.BlockSpec(w_vt_bf.shape, const),
                  pl.BlockSpec(seg.shape, const)],
        out_specs=[pl.BlockSpec((tm, ATTN_WIDTH), row), pl.BlockSpec((tm, ATTN_WIDTH), row),
                   pl.BlockSpec((1, ATTN_WIDTH, tm), lambda i: (i, 0, 0)),
                   pl.BlockSpec((tm, SSM_WIDTH), row),
                   pl.BlockSpec((1, 8, LANES), lambda i: (i, 0, 0))],
        out_shape=[jax.ShapeDtypeStruct((t, ATTN_WIDTH), BF16), jax.ShapeDtypeStruct((t, ATTN_WIDTH), BF16),
                   jax.ShapeDtypeStruct((t // tm, ATTN_WIDTH, tm), BF16),
                   jax.ShapeDtypeStruct((t, SSM_WIDTH), F32),
                   jax.ShapeDtypeStruct((t // tm, 8, LANES), F32)],
        compiler_params=_cparams(("parallel",)),
        name="inproj",
    )(x, g_mix.reshape(1, D_MODEL), w_qku_bf, w_vt_bf, seg)


ONES_ROWS = 16
ACC_ROWS = HEAD_W + ONES_ROWS
POS_SPLIT = 16
N_BIAS_PARTS = 3
SKIP_BITS = 130.0
NORM_INFLATE = 1.0 + 2.0 ** -5


def _attn_kernel(sc_ref, ub_ref, q_ref, k_ref, vt_ref, qf_ref, kf_ref, g_ref, o_ref,
                 qa_ref, m_ref, acc_ref, sa_ref, sb_ref, pa_ref, pb_ref, ala_ref, alb_ref,
                 *, ts, out_scale):
    bi = pl.program_id(0)
    h = pl.program_id(1)
    qi = pl.program_id(2)
    nsub = k_ref.shape[1] // ts
    bslope = sc_ref[1 + h]

    q = q_ref[0]
    lane = lax.broadcasted_iota(jnp.int32, q.shape, 1)
    zero = jnp.zeros_like(q)
    qf = qf_ref[0]
    qa_ref[0] = jnp.concatenate([jnp.where(lane < DIFF_HEAD_DIM, q, zero), qf], axis=1)
    qa_ref[1] = jnp.concatenate([jnp.where(lane >= DIFF_HEAD_DIM, q, zero), qf], axis=1)
    ones = jnp.ones((ONES_ROWS, ts), BF16)

    def keys(j, variant):
        rows = pl.ds(pl.multiple_of(j * ts, ts), ts)
        return jnp.concatenate([k_ref[0, rows, :], kf_ref[0, variant]], axis=1)

    def values(j):
        return jnp.concatenate([vt_ref[j], ones], axis=0)

    kpos = lax.broadcasted_iota(jnp.int32, (ts, ts), 0)
    qpos = lax.broadcasted_iota(jnp.int32, (ts, ts), 1)
    bias = jnp.abs(qpos - kpos).astype(F32) * (-bslope)
    ka = keys(qi, 2)
    vta = values(qi)
    s_diag = [lax.dot_general(ka, qa_ref[st], NT_DIMS, preferred_element_type=F32) for st in range(2)]
    for st in range(2):
        s = s_diag[st] + bias
        m_new = jnp.max(s, axis=0, keepdims=True)
        p = jnp.exp2(s - m_new).astype(BF16)
        acc_ref[st] = jnp.dot(vta, p, preferred_element_type=F32)
        m_ref[st] = m_new

    ub = ub_ref[(bi * N_HEADS + h) * nsub + qi]
    slack = ub - jnp.min(m_ref[...]) + SKIP_BITS
    x = (slack * sc_ref[1 + N_HEADS + h] - 1.0) * (1.0 / ts)
    w = jnp.where(x < 0.0, 0, jnp.minimum(x, float(nsub)).astype(jnp.int32) + 1)
    lo = jnp.maximum(qi - w, 0)
    n_off = jnp.minimum(qi + w, nsub - 1) - lo

    def sub_index(n):
        j = lo + jnp.clip(n, 0, jnp.maximum(n_off - 1, 0))
        return j + (j >= qi).astype(jnp.int32)

    def qk(n, s_ref):
        j = sub_index(n)
        ka = keys(j, jnp.where(j < qi, 0, 1))
        for st in range(2):
            s_ref[st] = lax.dot_general(ka, qa_ref[st], NT_DIMS, preferred_element_type=F32)

    def softmax(n, s_ref, p_ref, al_ref, keep=None):
        j = sub_index(n)
        c = -bslope * (jnp.abs(qi - j) * ts).astype(F32)
        for st in range(2):
            m_old = m_ref[st]
            m_new = jnp.maximum(m_old, jnp.max(s_ref[st], axis=0, keepdims=True) + c)
            al_ref[st] = jnp.exp2(m_old - m_new)
            p = jnp.exp2(s_ref[st] - (m_new - c))
            if keep is not None:
                p = p * keep
            p_ref[st] = p.astype(BF16)
            m_ref[st] = m_new

    def pv(n, p_ref, al_ref):
        vta = values(sub_index(n))
        for st in range(2):
            acc_ref[st] = al_ref[st] * acc_ref[st] + jnp.dot(vta, p_ref[st], preferred_element_type=F32)

    @pl.when(n_off > 0)
    def _():
        for p_ref, al_ref in ((pa_ref, ala_ref), (pb_ref, alb_ref)):
            p_ref[...] = jnp.zeros_like(p_ref)
            al_ref[...] = jnp.ones_like(al_ref)
        qk(0, sa_ref)

        def pair(g, carry):
            n = 2 * g
            pv(n - 2, pa_ref, ala_ref)
            qk(n + 1, sb_ref)
            softmax(n, sa_ref, pa_ref, ala_ref)
            pv(n - 1, pb_ref, alb_ref)
            qk(n + 2, sa_ref)
            softmax(n + 1, sb_ref, pb_ref, alb_ref)
            return carry

        n_pairs = n_off // 2
        lax.fori_loop(0, n_pairs, pair, 0)
        pv(2 * n_pairs - 2, pa_ref, ala_ref)
        softmax(n_off - 1, sa_ref, pa_ref, ala_ref, keep=(n_off % 2).astype(F32))
        pv(2 * n_pairs - 1, pb_ref, alb_ref)
        pv(n_off - 1, pa_ref, ala_ref)

    lam = sc_ref[0]
    a1 = acc_ref[0]
    a2 = acc_ref[1]
    out_t = (a1[:HEAD_W] / a1[HEAD_W:HEAD_W + 1] - lam * (a2[:HEAD_W] / a2[HEAD_W:HEAD_W + 1]))
    ms = jnp.mean(out_t * out_t, axis=0, keepdims=True)
    out_t = out_t * lax.rsqrt(ms + EPS) * (g_ref[...] * out_scale)
    o_ref[0] = out_t.T.astype(o_ref.dtype)


def _attn_features(bparts, ts):
    pos = jnp.arange(ts, dtype=jnp.int32)
    hi = ((pos // POS_SPLIT) * POS_SPLIT).astype(F32)[None, :, None]
    lo = (pos % POS_SPLIT).astype(F32)[None, :, None]
    bp = bparts[:, None, :]
    one = jnp.ones_like(bp)
    qf = jnp.stack([bp + 0 * hi, bp + 0 * hi, hi * one, lo * one], axis=-1)
    kf = jnp.stack([hi * one, lo * one, -bp + 0 * hi, -bp + 0 * hi], axis=-1)
    pad = lambda f: jnp.pad(f.reshape(N_HEADS, ts, 4 * N_BIAS_PARTS),
                            ((0, 0), (0, 0), (0, HEAD_W - 4 * N_BIAS_PARTS)))
    qf, kf = pad(qf), pad(kf)
    kf = jnp.stack([kf, -kf, jnp.zeros_like(kf)], axis=1)
    return qf.astype(BF16), kf.astype(BF16)


def _score_bounds(nrm, b, nsub):
    nrm = jnp.max(nrm, axis=1)[:, :2 * N_HEADS * 2].reshape(b, nsub, 2, N_HEADS, 2)
    q2 = nrm[:, :, 0]
    k2 = jnp.max(nrm[:, :, 1], axis=1)
    ub = jnp.max(jnp.sqrt(q2 * k2[:, None]), axis=-1) * NORM_INFLATE
    return jnp.transpose(ub, (0, 2, 1)).reshape(-1)


def _attention(q, k, vt, scal, ub, qf, kf, g_subln, lambda_init, ts):
    b, l, _ = q.shape
    nsub = l // ts
    kernel = functools.partial(_attn_kernel, ts=ts, out_scale=1.0 - lambda_init)
    return pl.pallas_call(
        kernel,
        grid=(b, N_HEADS, nsub),
        in_specs=[pl.BlockSpec(memory_space=pltpu.SMEM),
                  pl.BlockSpec(memory_space=pltpu.SMEM),
                  pl.BlockSpec((1, ts, HEAD_W), lambda bi, h, qi: (bi, qi, h)),
                  pl.BlockSpec((1, l, HEAD_W), lambda bi, h, qi: (bi, 0, h)),
                  pl.BlockSpec((nsub, HEAD_W, ts), lambda bi, h, qi: (bi, h, 0)),
                  pl.BlockSpec((1, ts, HEAD_W), lambda bi, h, qi: (h, 0, 0)),
                  pl.BlockSpec((1, 3, ts, HEAD_W), lambda bi, h, qi: (h, 0, 0, 0)),
                  pl.BlockSpec((HEAD_W, 1), lambda bi, h, qi: (0, 0))],
        out_specs=pl.BlockSpec((1, ts, HEAD_W), lambda bi, h, qi: (bi, qi, h)),
        out_shape=jax.ShapeDtypeStruct((b, l, ATTN_WIDTH), BF16),
        scratch_shapes=[pltpu.VMEM((2, ts, 2 * HEAD_W), BF16),
                        pltpu.VMEM((2, 1, ts), F32),
                        pltpu.VMEM((2, ACC_ROWS, ts), F32),
                        pltpu.VMEM((2, ts, ts), F32), pltpu.VMEM((2, ts, ts), F32),
                        pltpu.VMEM((2, ts, ts), BF16), pltpu.VMEM((2, ts, ts), BF16),
                        pltpu.VMEM((2, 1, ts), F32), pltpu.VMEM((2, 1, ts), F32)],
        compiler_params=_cparams(("parallel", "parallel", "arbitrary")),
        name="attn",
    )(scal, ub, q, k, vt, qf, kf, g_subln.reshape(HEAD_W, 1))


SCAN_LANES = 512
S5_BLOCKS = 2


def _s5_kernel(uf_ref, ub_ref, pm_ref, pmt_ref,
               bbf_ref, ccf_ref, af_ref, atf_ref, apf_ref,
               bbb_ref, ccb_ref, ab_ref, atb_ref, apb_ref,
               yf_ref, yb_ref, xsf_ref, xsb_ref, xbf_ref, xbb_ref, carf_ref, carb_ref, *, tc):
    tc8 = tc // 8
    cw = SSM_WIDTH // S5_BLOCKS
    sw = N_STATE // S5_BLOCKS

    @pl.when(pl.program_id(1) == 0)
    def _():
        carf_ref[...] = jnp.zeros_like(carf_ref)
        carb_ref[...] = jnp.zeros_like(carb_ref)

    def direction(u_ref, bb_ref, cc_ref, a_ref, at_ref, apow_ref, y_ref, xs_ref, xb_ref, carry_ref,
                  reverse):
        def bproj():
            up = jnp.dot(pm_ref[...], u_ref[0].astype(BF16), preferred_element_type=F32).astype(BF16)
            for hb in range(S5_BLOCKS):
                bu = jnp.dot(up[:, hb * cw:(hb + 1) * cw], bb_ref[hb], preferred_element_type=F32)
                xs_ref[:, hb * sw:(hb + 1) * sw] = bu[:, :sw]
                xs_ref[:, N_STATE + hb * sw:N_STATE + (hb + 1) * sw] = bu[:, sw:]

        def scan():
            for cb in range(N_STATE // SCAN_LANES):
                lo = cb * SCAN_LANES
                ar = jnp.broadcast_to(a_ref[0:1, lo:lo + SCAN_LANES], (8, SCAN_LANES))
                ai = jnp.broadcast_to(a_ref[1:2, lo:lo + SCAN_LANES], (8, SCAN_LANES))
                xr = xi = None
                for i in range(tc8):
                    r = (tc8 - 1 - i) if reverse else i
                    br = xs_ref[r * 8:r * 8 + 8, lo:lo + SCAN_LANES]
                    bi = xs_ref[r * 8:r * 8 + 8, N_STATE + lo:N_STATE + lo + SCAN_LANES]
                    if i == 0:
                        xr, xi = br, bi
                    else:
                        xr, xi = ar * xr - ai * xi + br, ar * xi + ai * xr + bi
                        xs_ref[r * 8:r * 8 + 8, lo:lo + SCAN_LANES] = xr
                        xs_ref[r * 8:r * 8 + 8, N_STATE + lo:N_STATE + lo + SCAN_LANES] = xi

        def fix():
            last = 0 if reverse else (tc8 - 1) * 8
            e_re = xs_ref[last:last + 8, 0:N_STATE]
            e_im = xs_ref[last:last + 8, N_STATE:2 * N_STATE]
            atr, ati = at_ref[0:1, :], at_ref[1:2, :]
            cr, ci = carry_ref[0:1, :], carry_ref[1:2, :]
            cin_r, cin_i = [None] * 8, [None] * 8
            for j in (range(7, -1, -1) if reverse else range(8)):
                cin_r[j], cin_i[j] = cr, ci
                cr, ci = (atr * cr - ati * ci + e_re[j:j + 1], atr * ci + ati * cr + e_im[j:j + 1])
            carry_ref[0:1, :] = cr
            carry_ref[1:2, :] = ci
            cin_r = jnp.concatenate(cin_r + cin_r, axis=0)
            cin_i = jnp.concatenate(cin_i + cin_i, axis=0)
            for i in range(tc // 16):
                rows = slice(i * 16, (i + 1) * 16)
                pr = apow_ref[rows, 0:N_STATE]
                pi = apow_ref[rows, N_STATE:2 * N_STATE]
                xr = xs_ref[rows, 0:N_STATE] + pr * cin_r - pi * cin_i
                xi = xs_ref[rows, N_STATE:2 * N_STATE] + pr * cin_i + pi * cin_r
                xb_ref[rows, 0:N_STATE] = xr.astype(BF16)
                xb_ref[rows, N_STATE:2 * N_STATE] = xi.astype(BF16)

        def cproj():
            yp = jnp.concatenate(
                [jnp.dot(xb_ref[:, hb * sw:(hb + 1) * sw], cc_ref[hb, 0], preferred_element_type=F32)
                 + jnp.dot(xb_ref[:, N_STATE + hb * sw:N_STATE + (hb + 1) * sw], cc_ref[hb, 1],
                           preferred_element_type=F32) for hb in range(S5_BLOCKS)], axis=1)
            hi = yp.astype(BF16)
            lo_part = (yp - hi.astype(F32)).astype(BF16)
            y_ref[0] = (jnp.dot(pmt_ref[...], hi, preferred_element_type=F32)
                        + jnp.dot(pmt_ref[...], lo_part, preferred_element_type=F32))

        return bproj, scan, fix, cproj

    fwd = direction(uf_ref, bbf_ref, ccf_ref, af_ref, atf_ref, apf_ref, yf_ref, xsf_ref, xbf_ref,
                    carf_ref, False)
    bwd = direction(ub_ref, bbb_ref, ccb_ref, ab_ref, atb_ref, apb_ref, yb_ref, xsb_ref, xbb_ref,
                    carb_ref, True)
    for stage_f, stage_b in zip(fwd, bwd):
        stage_f()
        stage_b()


def _s5(u, pm, pmt, par_f, par_b, tc):
    b, l, _ = u.shape
    nc = l // tc
    fmap = lambda bi, c: (bi, c, 0)
    bmap = lambda bi, c: (bi, nc - 1 - c, 0)
    full = lambda arr: pl.BlockSpec(arr.shape, lambda bi, c: (0,) * arr.ndim)
    kernel = functools.partial(_s5_kernel, tc=tc)
    state = [pltpu.VMEM((tc, 2 * N_STATE), F32)] * 2 + [pltpu.VMEM((tc, 2 * N_STATE), BF16)] * 2
    return pl.pallas_call(
        kernel,
        grid=(b, nc),
        in_specs=[pl.BlockSpec((1, tc, SSM_WIDTH), fmap), pl.BlockSpec((1, tc, SSM_WIDTH), bmap),
                  full(pm), full(pmt)] + [full(x) for x in par_f] + [full(x) for x in par_b],
        out_specs=[pl.BlockSpec((1, tc, SSM_WIDTH), fmap), pl.BlockSpec((1, tc, SSM_WIDTH), bmap)],
        out_shape=[jax.ShapeDtypeStruct((b, l, SSM_WIDTH), F32)] * 2,
        scratch_shapes=state + [pltpu.VMEM((2, N_STATE), F32)] * 2,
        compiler_params=_cparams(("parallel", "arbitrary")),
        name="s5",
    )(u, u, pm, pmt, *par_f, *par_b)


def _s5_params(lam_re, lam_im, log_dt, b_re, b_im, c_re, c_im, tc, reverse):
    tc8 = tc // 8
    dt = jnp.exp(log_dt)[:, None]
    mag = jnp.exp(lam_re * dt)
    ab_re = mag * jnp.cos(lam_im * dt)
    ab_im = mag * jnp.sin(lam_im * dt)
    den = lam_re * lam_re + lam_im * lam_im
    nr, ni = ab_re - 1.0, ab_im
    co_re = (nr * lam_re + ni * lam_im) / den
    co_im = (ni * lam_re - nr * lam_im) / den
    bb_re = co_re[..., None] * b_re - co_im[..., None] * b_im
    bb_im = co_re[..., None] * b_im + co_im[..., None] * b_re
    gb = N_GROUPS // S5_BLOCKS
    cw, sw = SSM_WIDTH // S5_BLOCKS, N_STATE // S5_BLOCKS
    eye = jnp.eye(gb, dtype=F32)
    blk_in = lambda w: jnp.einsum('bgph,gk->bghkp', w.reshape(S5_BLOCKS, gb, SSM_STATE, SSM_GROUP),
                                  eye).reshape(S5_BLOCKS, cw, sw)
    bb = jnp.concatenate([blk_in(bb_re), blk_in(bb_im)], axis=2).astype(BF16)
    blk_out = lambda w: jnp.einsum('bghp,gk->bgpkh', w.reshape(S5_BLOCKS, gb, SSM_GROUP, SSM_STATE),
                                   eye).reshape(S5_BLOCKS, sw, cw)
    cc = jnp.stack([blk_out(c_re), -blk_out(c_im)], axis=1).astype(BF16)
    a = jnp.stack([ab_re.reshape(-1), ab_im.reshape(-1)])
    n = jnp.arange(1, tc8 + 1, dtype=F32)[:, None, None]
    pmag = jnp.exp(n * (lam_re * dt)[None])
    ang = n * (lam_im * dt)[None]
    pw_re = (pmag * jnp.cos(ang)).reshape(tc8, N_STATE)
    pw_im = (pmag * jnp.sin(ang)).reshape(tc8, N_STATE)
    at = jnp.stack([pw_re[-1], pw_im[-1]])
    if reverse:
        pw_re, pw_im = pw_re[::-1], pw_im[::-1]
    apow = jnp.concatenate([jnp.repeat(pw_re, 8, axis=0), jnp.repeat(pw_im, 8, axis=0)], axis=1)
    return bb, cc, a, at, apow


def _perm_matrices(tc):
    tc8 = tc // 8
    i = np.arange(tc)
    src = (i % 8) * tc8 + i // 8
    pm = np.zeros((tc, tc), np.float32)
    pm[i, src] = 1.0
    return jnp.asarray(pm, BF16), jnp.asarray(pm.T, BF16)


def _mix_kernel(h_ref, a_ref, u_ref, yf_ref, yb_ref, d_ref, wglu_ref, bglu_ref, wout_ref,
                gffn_ref, wr_ref, br_ref, tri_ref, h1_ref, hn_ref, ti_ref, tg_ref, counts_ref, cnt_ref):
    y = d_ref[...] * u_ref[...] + yf_ref[...] + yb_ref[...]
    y = jax.nn.gelu(y)
    z = jnp.dot(y.astype(BF16), wglu_ref[...], preferred_element_type=F32) + bglu_ref[...]
    s = y * jax.nn.sigmoid(z)
    h1 = (h_ref[...]
          + jnp.dot(a_ref[...], wout_ref[0:ATTN_WIDTH, :], preferred_element_type=F32)
          + jnp.dot(s.astype(BF16), wout_ref[ATTN_WIDTH:, :], preferred_element_type=F32))
    h1_ref[...] = h1
    hn = _rms(h1, gffn_ref[...])
    hn_ref[...] = hn
    hn_hi = hn.astype(BF16)
    hn_lo = (hn - hn_hi.astype(F32)).astype(BF16)
    logits = (jnp.dot(hn_hi, wr_ref[0], preferred_element_type=F32)
              + jnp.dot(hn_lo, wr_ref[0], preferred_element_type=F32)
              + jnp.dot(hn_hi, wr_ref[1], preferred_element_type=F32)) + br_ref[...]
    lane = lax.broadcasted_iota(jnp.int32, logits.shape, 1)
    neg = jnp.float32(-jnp.inf)
    work = logits
    ti = jnp.zeros(logits.shape, jnp.int32)
    tv = jnp.full(logits.shape, neg, F32)
    picked = jnp.zeros(logits.shape, F32)
    idxs = []
    for kk in range(TOP_K):
        mx = jnp.max(work, axis=-1, keepdims=True)
        idx = jnp.min(jnp.where(work == mx, lane, LANES), axis=-1, keepdims=True)
        idxs.append(idx)
        ti = jnp.where(lane == kk, idx, ti)
        tv = jnp.where(lane == kk, mx, tv)
        picked = jnp.where(lane == idx, 1.0, picked)
        work = jnp.where(lane == idx, neg, work)
    ex = jnp.exp(tv - jnp.max(tv, axis=-1, keepdims=True))
    tg_ref[...] = ex / jnp.sum(ex, axis=-1, keepdims=True)

    @pl.when(pl.program_id(0) == 0)
    def _():
        cnt_ref[...] = jnp.zeros_like(cnt_ref)

    before = jnp.dot(tri_ref[...], picked.astype(BF16), preferred_element_type=F32) + cnt_ref[0:1, :]
    for kk in range(TOP_K):
        rank = jnp.sum(jnp.where(lane == idxs[kk], before, 0.0), axis=-1, keepdims=True)
        ti = jnp.where(lane == TOP_K + kk, rank.astype(jnp.int32), ti)
    ti_ref[...] = ti
    cnt_ref[...] = cnt_ref[...] + jnp.sum(picked, axis=0, keepdims=True)
    counts_ref[...] = cnt_ref[...]


def _mix(h, a, u, yf, yb, d, wglu_bf, bglu, wout_bf, g_ffn, wr_pad, br_pad, tm):
    t = h.shape[0]
    row = lambda i: (i, 0)
    const = lambda i: (0, 0)
    full = lambda arr: pl.BlockSpec(arr.shape, lambda i: (0,) * arr.ndim)
    tri = jnp.asarray(np.tril(np.ones((tm, tm), np.float32), -1), BF16)
    return pl.pallas_call(
        _mix_kernel,
        grid=(t // tm,),
        in_specs=[pl.BlockSpec((tm, D_MODEL), row), pl.BlockSpec((tm, ATTN_WIDTH), row),
                  pl.BlockSpec((tm, SSM_WIDTH), row), pl.BlockSpec((tm, SSM_WIDTH), row),
                  pl.BlockSpec((tm, SSM_WIDTH), row),
                  full(d), full(wglu_bf), full(bglu), full(wout_bf), full(g_ffn), full(wr_pad),
                  full(br_pad), full(tri)],
        out_specs=[pl.BlockSpec((tm, D_MODEL), row), pl.BlockSpec((tm, D_MODEL), row),
                   pl.BlockSpec((tm, LANES), row), pl.BlockSpec((tm, LANES), row),
                   pl.BlockSpec((8, LANES), const)],
        out_shape=[jax.ShapeDtypeStruct((t, D_MODEL), F32), jax.ShapeDtypeStruct((t, D_MODEL), F32),
                   jax.ShapeDtypeStruct((t, LANES), jnp.int32), jax.ShapeDtypeStruct((t, LANES), F32),
                   jax.ShapeDtypeStruct((8, LANES), F32)],
        scratch_shapes=[pltpu.VMEM((8, LANES), F32)],
        compiler_params=_cparams(("arbitrary",)),
        name="mix",
    )(h, a, u, yf, yb, d, wglu_bf, bglu, wout_bf, g_ffn, wr_pad, br_pad, tri)


ROWS_PER_ISSUE = 4

def _dispatch_kernel(dest_ref, pad_start_ref, pad_len_ref, hn_ref, xs_ref, zrow_ref, sem, zsem, *, tm):
    @pl.when(pl.program_id(0) == 0)
    def _():
        zrow_ref[...] = jnp.zeros_like(zrow_ref)

        def zero_copy(row):
            return pltpu.make_async_copy(zrow_ref.at[pl.ds(0, 1), :], xs_ref.at[pl.ds(row, 1), :], zsem)

        def start_span(e, _):
            base = pad_start_ref[e]
            lax.fori_loop(0, pad_len_ref[e], lambda r, c: (zero_copy(base + r).start(), c)[1], 0)
            return 0

        def wait_span(e, _):
            lax.fori_loop(0, pad_len_ref[e], lambda r, c: (zero_copy(0).wait(), c)[1], 0)
            return 0

        lax.fori_loop(0, N_EXPERTS + 1, start_span, 0)
        lax.fori_loop(0, N_EXPERTS + 1, wait_span, 0)

    def issue(g, _):
        for rr in range(ROWS_PER_ISSUE):
            r = g * ROWS_PER_ISSUE + rr
            for kk in range(TOP_K):
                pltpu.make_async_copy(hn_ref.at[pl.ds(r, 1), :],
                                      xs_ref.at[pl.ds(dest_ref[r * TOP_K + kk], 1), :], sem).start()
        return 0

    lax.fori_loop(0, tm // ROWS_PER_ISSUE, issue, 0)

    def drain(g, _):
        for _unused in range(ROWS_PER_ISSUE * TOP_K):
            pltpu.make_async_copy(hn_ref.at[pl.ds(0, 1), :], xs_ref.at[pl.ds(0, 1), :], sem).wait()
        return 0

    lax.fori_loop(0, tm // ROWS_PER_ISSUE, drain, 0)


def _dispatch(dest_flat, pad_start, pad_len, hn, n_pad, tm):
    t = hn.shape[0]
    kernel = functools.partial(_dispatch_kernel, tm=tm)
    return pl.pallas_call(
        kernel,
        grid=(t // tm,),
        in_specs=[pl.BlockSpec((tm * TOP_K,), lambda i: (i,), memory_space=pltpu.SMEM),
                  pl.BlockSpec(memory_space=pltpu.SMEM), pl.BlockSpec(memory_space=pltpu.SMEM),
                  pl.BlockSpec((tm, D_MODEL), lambda i: (i, 0))],
        out_specs=pl.BlockSpec(memory_space=pl.ANY),
        out_shape=jax.ShapeDtypeStruct((n_pad, D_MODEL), F32),
        scratch_shapes=[pltpu.VMEM((8, D_MODEL), F32), pltpu.SemaphoreType.DMA(()),
                        pltpu.SemaphoreType.DMA(())],
        compiler_params=_cparams(("arbitrary",)),
        name="dispatch",
    )(dest_flat, pad_start, pad_len, hn)


def _experts_kernel(be_ref, x_ref, wg_ref, bg_ref, wu_ref, bu_ref, wd_ref, bd_ref, y_ref,
                    wgb_ref, wub_ref, wdb_ref):
    i = pl.program_id(0)

    @pl.when((i == 0) | (be_ref[i] != be_ref[jnp.maximum(i - 1, 0)]))
    def _():
        wgb_ref[...] = wg_ref[0].astype(BF16)
        wub_ref[...] = wu_ref[0].astype(BF16)
        wdb_ref[...] = wd_ref[0].astype(BF16)

    x = x_ref[...].astype(BF16)
    gt = jnp.minimum(jnp.dot(x, wgb_ref[...], preferred_element_type=F32) + bg_ref[0], SWIGLU_LIMIT)
    up = jnp.clip(jnp.dot(x, wub_ref[...], preferred_element_type=F32) + bu_ref[0],
                  -SWIGLU_LIMIT, SWIGLU_LIMIT)
    hh = (up + 1.0) * (gt * jax.nn.sigmoid(SWIGLU_ALPHA * gt))
    y_ref[...] = jnp.dot(hh.astype(BF16), wdb_ref[...], preferred_element_type=F32) + bd_ref[0]


def _experts(blk_expert, xs, wg, bg, wu, bu, wd, bd, eb):
    n_pad = xs.shape[0]
    row = lambda i, be: (i, 0)
    wmap = lambda i, be: (be[i], 0, 0)
    wspec = pl.BlockSpec((1, D_MODEL, D_MODEL), wmap)
    bspec = pl.BlockSpec((1, 1, D_MODEL), wmap)
    grid_spec = pltpu.PrefetchScalarGridSpec(
        num_scalar_prefetch=1,
        grid=(n_pad // eb,),
        in_specs=[pl.BlockSpec((eb, D_MODEL), row), wspec, bspec, wspec, bspec, wspec, bspec],
        out_specs=pl.BlockSpec((eb, D_MODEL), row),
        scratch_shapes=[pltpu.VMEM((D_MODEL, D_MODEL), BF16)] * 3,
    )
    return pl.pallas_call(
        _experts_kernel,
        grid_spec=grid_spec,
        out_shape=jax.ShapeDtypeStruct((n_pad, D_MODEL), F32),
        compiler_params=_cparams(("arbitrary",)),
        name="experts",
    )(blk_expert, xs, wg, bg, wu, bu, wd, bd)


def _tail_kernel(dest_ref, h1_ref, tg_ref, p_ref, ys_ref, gple_ref, wpg_ref, wpp_ref, gfin_ref,
                 o_ref, buf_ref, sem, *, tm):
    def issue(g, _):
        for rr in range(ROWS_PER_ISSUE):
            r = g * ROWS_PER_ISSUE + rr
            for kk in range(TOP_K):
                pltpu.make_async_copy(ys_ref.at[pl.ds(dest_ref[r * TOP_K + kk], 1), :],
                                      buf_ref.at[kk, pl.ds(r, 1), :], sem).start()
        return 0

    lax.fori_loop(0, tm // ROWS_PER_ISSUE, issue, 0)

    def drain(g, _):
        for _unused in range(ROWS_PER_ISSUE * TOP_K):
            pltpu.make_async_copy(ys_ref.at[pl.ds(0, 1), :], buf_ref.at[0, pl.ds(0, 1), :], sem).wait()
        return 0

    lax.fori_loop(0, tm // ROWS_PER_ISSUE, drain, 0)

    tg = tg_ref[...]
    h2 = h1_ref[...]
    for kk in range(TOP_K):
        h2 = h2 + tg[:, kk:kk + 1] * buf_ref[kk]
    gate = jax.nn.sigmoid(jnp.dot(_rms(h2, gple_ref[...]).astype(BF16), wpg_ref[...],
                                  preferred_element_type=F32))
    proj = jnp.dot(p_ref[...].astype(BF16), wpp_ref[...], preferred_element_type=F32)
    o_ref[...] = _rms(h2 + gate * proj, gfin_ref[...])


def _tail(dest_flat, h1, tg, p, ys, g_ple, wpg_bf, wpp_bf, g_final, tm):
    t = h1.shape[0]
    row = lambda i: (i, 0)
    const = lambda i: (0, 0)
    full = lambda arr: pl.BlockSpec(arr.shape, lambda i: (0,) * arr.ndim)
    kernel = functools.partial(_tail_kernel, tm=tm)
    return pl.pallas_call(
        kernel,
        grid=(t // tm,),
        in_specs=[pl.BlockSpec((tm * TOP_K,), lambda i: (i,), memory_space=pltpu.SMEM),
                  pl.BlockSpec((tm, D_MODEL), row), pl.BlockSpec((tm, LANES), row),
                  pl.BlockSpec((tm, PLE_DIM), row),
                  pl.BlockSpec(memory_space=pl.ANY),
                  full(g_ple), full(wpg_bf), full(wpp_bf), full(g_final)],
        out_specs=pl.BlockSpec((tm, D_MODEL), row),
        out_shape=jax.ShapeDtypeStruct((t, D_MODEL), F32),
        scratch_shapes=[pltpu.VMEM((TOP_K, tm, D_MODEL), F32), pltpu.SemaphoreType.DMA(())],
        compiler_params=_cparams(("arbitrary",)),
        name="tail",
    )(dest_flat, h1, tg, p, ys, g_ple, wpg_bf, wpp_bf, g_final)


def _routing(ti, counts, eb):
    t = ti.shape[0]
    top_i, rank = ti[:, :TOP_K], ti[:, TOP_K:2 * TOP_K]
    counts = counts[0, :N_EXPERTS].astype(jnp.int32)
    padded = ((counts + eb - 1) // eb) * eb
    pends = jnp.cumsum(padded)
    pstarts = pends - padded
    onehot = top_i[:, :, None] == jnp.arange(N_EXPERTS, dtype=jnp.int32)
    dest = rank + jnp.sum(jnp.where(onehot, pstarts, 0), axis=-1)
    n_blocks = -(-(t * TOP_K + N_EXPERTS * (eb - 1)) // eb)
    blk_start = jnp.arange(n_blocks, dtype=jnp.int32) * eb
    blk_expert = jnp.minimum(jnp.sum(blk_start[:, None] >= pends[None, :], axis=1), N_EXPERTS - 1)
    n_pad = n_blocks * eb
    pad_start = jnp.concatenate([pstarts + counts, pends[-1:]]).astype(jnp.int32)
    pad_len = jnp.concatenate([padded - counts, n_pad - pends[-1:]]).astype(jnp.int32)
    return (dest.reshape(-1).astype(jnp.int32), blk_expert.astype(jnp.int32), pad_start, pad_len, n_pad)


def _pick(n, prefs):
    for c in prefs:
        if n % c == 0:
            return c
    raise ValueError(f"no tile for {n}")


def _trunk(x, p, w, lambda_init):
    b, l, _ = x.shape
    t = b * l
    tm = TOKEN_TILE
    assert l % tm == 0
    tc = _pick(l, (256, 128))
    eb = 256

    q, k, vt, u, nrm = _inproj(x.reshape(t, D_MODEL), w['g_mix'], w['w_qku'], w['w_vt'], tm)
    a = _attention(q.reshape(b, l, -1), k.reshape(b, l, -1), vt, w['attn_scal'],
                   _score_bounds(nrm, b, l // tm), w['attn_qf'], w['attn_kf'], w['g_subln'],
                   lambda_init, tm)
    u3 = u.reshape(b, l, SSM_WIDTH)
    pm, pmt = _perm_matrices(tc)
    ys = _s5(u3, pm, pmt, w['s5'][0], w['s5'][1], tc)
    h1, hn, ti, tg, counts = _mix(x.reshape(t, D_MODEL), a.reshape(t, ATTN_WIDTH), u,
                                  ys[0].reshape(t, SSM_WIDTH), ys[1].reshape(t, SSM_WIDTH),
                                  w['ssm_d'], w['w_glu'], w['b_glu'], w['w_out'], w['g_ffn'],
                                  w['w_router'], w['b_router'], tm)
    dest, blk_expert, pad_start, pad_len, n_pad = _routing(ti, counts, eb)
    xs = _dispatch(dest, pad_start, pad_len, hn, n_pad, tm // 2)
    ye = _experts(blk_expert, xs, w['w_gate'], w['b_gate'], w['w_up'], w['b_up'],
                  w['w_down'], w['b_down'], eb)
    out = _tail(dest, h1, tg, p.reshape(t, PLE_DIM), ye, w['g_ple'], w['w_ple_gate'],
                w['w_ple_proj'], w['g_final'], tm // 2)
    return out.reshape(b, l, D_MODEL)


def _hi_lo(x):
    hi = x.astype(BF16)
    return jnp.stack([hi, (x - hi.astype(F32)).astype(BF16)])


def _prepare(i, g_mix, w_in, lambda_q1, lambda_k1, lambda_q2, lambda_k2, g_subln, ssm_lambda_re,
             ssm_lambda_im, ssm_log_dt, ssm_b_re, ssm_b_im, ssm_c_re, ssm_c_im, ssm_d, w_glu, b_glu,
             w_out, g_ffn, w_router, b_router, w_gate, b_gate, w_up, b_up, w_down, b_down, g_ple,
             w_ple_gate, w_ple_proj, g_final, tcs):
    lambda_init = 0.8 - 0.6 * math.exp(-0.3 * i)
    lam = (jnp.exp(jnp.sum(lambda_q1[i] * lambda_k1[i]))
           - jnp.exp(jnp.sum(lambda_q2[i] * lambda_k2[i])) + lambda_init)
    slopes = jnp.exp2(-8.0 * jnp.arange(1, N_HEADS + 1, dtype=F32) / N_HEADS)
    bslopes = slopes * LOG2E
    parts, rest = [], bslopes
    for _ in range(N_BIAS_PARTS):
        part = rest.astype(BF16).astype(F32)
        parts.append(part)
        rest = rest - part
    attn_qf, attn_kf = _attn_features(jnp.stack(parts, axis=1), TOKEN_TILE)
    w_qku = jnp.concatenate([w_in[i][:, :2 * ATTN_WIDTH], w_in[i][:, 3 * ATTN_WIDTH:]], axis=1)
    row = lambda vec: vec.reshape(1, -1).astype(F32)
    w = {
        'g_mix': g_mix[i], 'w_qku': w_qku.astype(BF16),
        'w_vt': w_in[i][:, 2 * ATTN_WIDTH:3 * ATTN_WIDTH].T.astype(BF16),
        'attn_scal': jnp.concatenate([lam.reshape(1), bslopes, 1.0 / bslopes]).astype(F32),
        'attn_qf': attn_qf, 'attn_kf': attn_kf,
        'g_subln': g_subln[i],
        'ssm_d': row(ssm_d[i]), 'w_glu': w_glu[i].astype(BF16), 'b_glu': row(b_glu[i]),
        'w_out': w_out[i].astype(BF16), 'g_ffn': row(g_ffn[i]),
        'w_router': _hi_lo(jnp.pad(w_router[i], ((0, 0), (0, LANES - N_EXPERTS)))),
        'b_router': jnp.pad(row(b_router[i]), ((0, 0), (0, LANES - N_EXPERTS)),
                            constant_values=-jnp.inf),
        'w_gate': w_gate[i], 'b_gate': b_gate[i].reshape(N_EXPERTS, 1, D_MODEL),
        'w_up': w_up[i], 'b_up': b_up[i].reshape(N_EXPERTS, 1, D_MODEL),
        'w_down': w_down[i], 'b_down': b_down[i].reshape(N_EXPERTS, 1, D_MODEL),
        'g_ple': row(g_ple[i]), 'w_ple_gate': w_ple_gate[i].astype(BF16),
        'w_ple_proj': w_ple_proj[i].astype(BF16), 'g_final': row(g_final),
    }
    w['s5'] = {tc: [_s5_params(ssm_lambda_re[i, dr], ssm_lambda_im[i, dr], ssm_log_dt[i, dr],
                               ssm_b_re[i, dr], ssm_b_im[i, dr], ssm_c_re[i, dr], ssm_c_im[i, dr],
                               tc, reverse=(dr == 1)) for dr in range(2)] for tc in tcs}
    return w, lambda_init


def kernel(x_prompt, x_sample, p_prompt, p_sample, g_mix, w_in, lambda_q1, lambda_k1, lambda_q2, lambda_k2, g_subln, ssm_lambda_re, ssm_lambda_im, ssm_log_dt, ssm_b_re, ssm_b_im, ssm_c_re, ssm_c_im, ssm_d, w_glu, b_glu, w_out, g_ffn, w_router, b_router, w_gate, b_gate, w_up, b_up, w_down, b_down, g_ple, w_ple_gate, w_ple_proj, g_final):
    assert w_in.shape[0] == 1, "single-layer trunk"
    tcs = {_pick(x.shape[1], (256, 128)) for x in (x_prompt, x_sample)}
    w, lambda_init = _prepare(0, g_mix, w_in, lambda_q1, lambda_k1, lambda_q2, lambda_k2, g_subln,
                              ssm_lambda_re, ssm_lambda_im, ssm_log_dt, ssm_b_re, ssm_b_im, ssm_c_re,
                              ssm_c_im, ssm_d, w_glu, b_glu, w_out, g_ffn, w_router, b_router, w_gate,
                              b_gate, w_up, b_up, w_down, b_down, g_ple, w_ple_gate, w_ple_proj,
                              g_final, tcs)
    outs = []
    for x, p in ((x_prompt, p_prompt), (x_sample, p_sample)):
        tc = _pick(x.shape[1], (256, 128))
        wt = dict(w, s5=w['s5'][tc])
        outs.append(_trunk(x, p[0], wt, lambda_init))
    return tuple(outs)
```

```python
import functools
import math

import jax
import jax.numpy as jnp
import numpy as np
from jax import lax
from jax.experimental import pallas as pl
from jax.experimental.pallas import tpu as pltpu

F32 = jnp.float32
BF16 = jnp.bfloat16

D_MODEL = 1024
PLE_DIM = 256
ATTN_WIDTH = 512
SSM_WIDTH = 512
HEAD_W = 128
DIFF_HEAD_DIM = 64
N_HEADS = 4
SSM_GROUP = 16
N_GROUPS = 32
SSM_STATE = 64
N_STATE = N_GROUPS * SSM_STATE
N_EXPERTS = 32
TOP_K = 4
SWIGLU_LIMIT = 7.0
SWIGLU_ALPHA = 1.702
EPS = 1e-6
LANES = 128
TOKEN_TILE = 512

VMEM_LIMIT = 56 * 1024 * 1024


def _cparams(sem):
    return pltpu.CompilerParams(dimension_semantics=sem, vmem_limit_bytes=VMEM_LIMIT)


def _rms(x, g):
    return x * lax.rsqrt(jnp.mean(x * x, axis=-1, keepdims=True) + EPS) * g


LOG2E = 1.4426950408889634
NT_DIMS = (((1,), (1,)), ((), ()))


def _inproj_kernel(x_ref, g_ref, wqku_ref, wvt_ref, seg_ref, q_ref, k_ref, vt_ref, u_ref, nrm_ref):
    xn = _rms(x_ref[...], g_ref[...]).astype(BF16)
    proj = jnp.dot(xn, wqku_ref[...], preferred_element_type=F32)
    scale = LOG2E / math.sqrt(DIFF_HEAD_DIM)
    qk = jnp.concatenate([proj[:, :ATTN_WIDTH] * scale, proj[:, ATTN_WIDTH:2 * ATTN_WIDTH]], axis=1)
    q_ref[...] = qk[:, :ATTN_WIDTH].astype(BF16)
    k_ref[...] = qk[:, ATTN_WIDTH:].astype(BF16)
    u_ref[...] = proj[:, 2 * ATTN_WIDTH:]
    nsq = jnp.dot((qk * qk).astype(BF16), seg_ref[...], preferred_element_type=F32)
    nrm_ref[0] = jnp.max(nsq.reshape(nsq.shape[0] // 8, 8, LANES), axis=0)
    vt_ref[0] = lax.dot_general(wvt_ref[...], xn, NT_DIMS, preferred_element_type=F32).astype(BF16)


def _inproj(x, g_mix, w_qku_bf, w_vt_bf, tm):
    t = x.shape[0]
    row = lambda i: (i, 0)
    const = lambda i: (0, 0)
    col = np.arange(2 * ATTN_WIDTH)
    seg = np.zeros((2 * ATTN_WIDTH, LANES), np.float32)
    seg[col, col // DIFF_HEAD_DIM] = 1.0
    seg = jnp.asarray(seg, BF16)
    return pl.pallas_call(
        _inproj_kernel,
        grid=(t // tm,),
        in_specs=[pl.BlockSpec((tm, D_MODEL), row),
                  pl.BlockSpec((1, D_MODEL), const),
                  pl.BlockSpec(w_qku_bf.shape, const),
                  pl.BlockSpec(w_vt_bf.shape, const),
                  pl.BlockSpec(seg.shape, const)],
        out_specs=[pl.BlockSpec((tm, ATTN_WIDTH), row), pl.BlockSpec((tm, ATTN_WIDTH), row),
                   pl.BlockSpec((1, ATTN_WIDTH, tm), lambda i: (i, 0, 0)),
                   pl.BlockSpec((tm, SSM_WIDTH), row),
                   pl.BlockSpec((1, 8, LANES), lambda i: (i, 0, 0))],
        out_shape=[jax.ShapeDtypeStruct((t, ATTN_WIDTH), BF16), jax.ShapeDtypeStruct((t, ATTN_WIDTH), BF16),
                   jax.ShapeDtypeStruct((t // tm, ATTN_WIDTH, tm), BF16),
                   jax.ShapeDtypeStruct((t, SSM_WIDTH), F32),
                   jax.ShapeDtypeStruct((t // tm, 8, LANES), F32)],
        compiler_params=_cparams(("parallel",)),
        name="inproj",
    )(x, g_mix.reshape(1, D_MODEL), w_qku_bf, w_vt_bf, seg)


ONES_ROWS = 16
ACC_ROWS = HEAD_W + ONES_ROWS
POS_SPLIT = 16
N_BIAS_PARTS = 3
SKIP_BITS = 130.0
NORM_INFLATE = 1.0 + 2.0 ** -5


def _attn_kernel(sc_ref, ub_ref, q_ref, k_ref, vt_ref, qf_ref, kf_ref, g_ref, o_ref,
                 qa_ref, m_ref, acc_ref, sa_ref, sb_ref, pa_ref, pb_ref, ala_ref, alb_ref,
                 *, ts, out_scale):
    bi = pl.program_id(0)
    h = pl.program_id(1)
    qi = pl.program_id(2)
    nsub = k_ref.shape[1] // ts
    bslope = sc_ref[1 + h]

    q = q_ref[0]
    lane = lax.broadcasted_iota(jnp.int32, q.shape, 1)
    zero = jnp.zeros_like(q)
    qf = qf_ref[0]
    qa_ref[0] = jnp.concatenate([jnp.where(lane < DIFF_HEAD_DIM, q, zero), qf], axis=1)
    qa_ref[1] = jnp.concatenate([jnp.where(lane >= DIFF_HEAD_DIM, q, zero), qf], axis=1)
    ones = jnp.ones((ONES_ROWS, ts), BF16)

    def keys(j, variant):
        rows = pl.ds(pl.multiple_of(j * ts, ts), ts)
        return jnp.concatenate([k_ref[0, rows, :], kf_ref[0, variant]], axis=1)

    def values(j):
        return jnp.concatenate([vt_ref[j], ones], axis=0)

    kpos = lax.broadcasted_iota(jnp.int32, (ts, ts), 0)
    qpos = lax.broadcasted_iota(jnp.int32, (ts, ts), 1)
    bias = jnp.abs(qpos - kpos).astype(F32) * (-bslope)
    ka = keys(qi, 2)
    vta = values(qi)
    s_diag = [lax.dot_general(ka, qa_ref[st], NT_DIMS, preferred_element_type=F32) for st in range(2)]
    for st in range(2):
        s = s_diag[st] + bias
        m_new = jnp.max(s, axis=0, keepdims=True)
        p = jnp.exp2(s - m_new).astype(BF16)
        acc_ref[st] = jnp.dot(vta, p, preferred_element_type=F32)
        m_ref[st] = m_new

    ub = ub_ref[(bi * N_HEADS + h) * nsub + qi]
    slack = ub - jnp.min(m_ref[...]) + SKIP_BITS
    x = (slack * sc_ref[1 + N_HEADS + h] - 1.0) * (1.0 / ts)
    w = jnp.where(x < 0.0, 0, jnp.minimum(x, float(nsub)).astype(jnp.int32) + 1)
    lo = jnp.maximum(qi - w, 0)
    n_off = jnp.minimum(qi + w, nsub - 1) - lo

    def sub_index(n):
        j = lo + jnp.clip(n, 0, jnp.maximum(n_off - 1, 0))
        return j + (j >= qi).astype(jnp.int32)

    def qk(n, s_ref):
        j = sub_index(n)
        ka = keys(j, jnp.where(j < qi, 0, 1))
        for st in range(2):
            s_ref[st] = lax.dot_general(ka, qa_ref[st], NT_DIMS, preferred_element_type=F32)

    def softmax(n, s_ref, p_ref, al_ref, keep=None):
        j = sub_index(n)
        c = -bslope * (jnp.abs(qi - j) * ts).astype(F32)
        for st in range(2):
            m_old = m_ref[st]
            m_new = jnp.maximum(m_old, jnp.max(s_ref[st], axis=0, keepdims=True) + c)
            al_ref[st] = jnp.exp2(m_old - m_new)
            p = jnp.exp2(s_ref[st] - (m_new - c))
            if keep is not None:
                p = p * keep
            p_ref[st] = p.astype(BF16)
            m_ref[st] = m_new

    def pv(n, p_ref, al_ref):
        vta = values(sub_index(n))
        for st in range(2):
            acc_ref[st] = al_ref[st] * acc_ref[st] + jnp.dot(vta, p_ref[st], preferred_element_type=F32)

    @pl.when(n_off > 0)
    def _():
        for p_ref, al_ref in ((pa_ref, ala_ref), (pb_ref, alb_ref)):
            p_ref[...] = jnp.zeros_like(p_ref)
            al_ref[...] = jnp.ones_like(al_ref)
        qk(0, sa_ref)

        def pair(g, carry):
            n = 2 * g
            pv(n - 2, pa_ref, ala_ref)
            qk(n + 1, sb_ref)
            softmax(n, sa_ref, pa_ref, ala_ref)
            pv(n - 1, pb_ref, alb_ref)
            qk(n + 2, sa_ref)
            softmax(n + 1, sb_ref, pb_ref, alb_ref)
            return carry

        n_pairs = n_off // 2
        lax.fori_loop(0, n_pairs, pair, 0)
        pv(2 * n_pairs - 2, pa_ref, ala_ref)
        softmax(n_off - 1, sa_ref, pa_ref, ala_ref, keep=(n_off % 2).astype(F32))
        pv(2 * n_pairs - 1, pb_ref, alb_ref)
        pv(n_off - 1, pa_ref, ala_ref)

    lam = sc_ref[0]
    a1 = acc_ref[0]
    a2 = acc_ref[1]
    out_t = (a1[:HEAD_W] / a1[HEAD_W:HEAD_W + 1] - lam * (a2[:HEAD_W] / a2[HEAD_W:HEAD_W + 1]))
    ms = jnp.mean(out_t * out_t, axis=0, keepdims=True)
    out_t = out_t * lax.rsqrt(ms + EPS) * (g_ref[...] * out_scale)
    o_ref[0] = out_t.T.astype(o_ref.dtype)


def _attn_features(bparts, ts):
    pos = jnp.arange(ts, dtype=jnp.int32)
    hi = ((pos // POS_SPLIT) * POS_SPLIT).astype(F32)[None, :, None]
    lo = (pos % POS_SPLIT).astype(F32)[None, :, None]
    bp = bparts[:, None, :]
    one = jnp.ones_like(bp)
    qf = jnp.stack([bp + 0 * hi, bp + 0 * hi, hi * one, lo * one], axis=-1)
    kf = jnp.stack([hi * one, lo * one, -bp + 0 * hi, -bp + 0 * hi], axis=-1)
    pad = lambda f: jnp.pad(f.reshape(N_HEADS, ts, 4 * N_BIAS_PARTS),
                            ((0, 0), (0, 0), (0, HEAD_W - 4 * N_BIAS_PARTS)))
    qf, kf = pad(qf), pad(kf)
    kf = jnp.stack([kf, -kf, jnp.zeros_like(kf)], axis=1)
    return qf.astype(BF16), kf.astype(BF16)


def _score_bounds(nrm, b, nsub):
    nrm = jnp.max(nrm, axis=1)[:, :2 * N_HEADS * 2].reshape(b, nsub, 2, N_HEADS, 2)
    q2 = nrm[:, :, 0]
    k2 = jnp.max(nrm[:, :, 1], axis=1)
    ub = jnp.max(jnp.sqrt(q2 * k2[:, None]), axis=-1) * NORM_INFLATE
    return jnp.transpose(ub, (0, 2, 1)).reshape(-1)


def _attention(q, k, vt, scal, ub, qf, kf, g_subln, lambda_init, ts):
    b, l, _ = q.shape
    nsub = l // ts
    kernel = functools.partial(_attn_kernel, ts=ts, out_scale=1.0 - lambda_init)
    return pl.pallas_call(
        kernel,
        grid=(b, N_HEADS, nsub),
        in_specs=[pl.BlockSpec(memory_space=pltpu.SMEM),
                  pl.BlockSpec(memory_space=pltpu.SMEM),
                  pl.BlockSpec((1, ts, HEAD_W), lambda bi, h, qi: (bi, qi, h)),
                  pl.BlockSpec((1, l, HEAD_W), lambda bi, h, qi: (bi, 0, h)),
                  pl.BlockSpec((nsub, HEAD_W, ts), lambda bi, h, qi: (bi, h, 0)),
                  pl.BlockSpec((1, ts, HEAD_W), lambda bi, h, qi: (h, 0, 0)),
                  pl.BlockSpec((1, 3, ts, HEAD_W), lambda bi, h, qi: (h, 0, 0, 0)),
                  pl.BlockSpec((HEAD_W, 1), lambda bi, h, qi: (0, 0))],
        out_specs=pl.BlockSpec((1, ts, HEAD_W), lambda bi, h, qi: (bi, qi, h)),
        out_shape=jax.ShapeDtypeStruct((b, l, ATTN_WIDTH), BF16),
        scratch_shapes=[pltpu.VMEM((2, ts, 2 * HEAD_W), BF16),
                        pltpu.VMEM((2, 1, ts), F32),
                        pltpu.VMEM((2, ACC_ROWS, ts), F32),
                        pltpu.VMEM((2, ts, ts), F32), pltpu.VMEM((2, ts, ts), F32),
                        pltpu.VMEM((2, ts, ts), BF16), pltpu.VMEM((2, ts, ts), BF16),
                        pltpu.VMEM((2, 1, ts), F32), pltpu.VMEM((2, 1, ts), F32)],
        compiler_params=_cparams(("parallel", "parallel", "arbitrary")),
        name="attn",
    )(scal, ub, q, k, vt, qf, kf, g_subln.reshape(HEAD_W, 1))


SCAN_LANES = 512
S5_BLOCKS = 2


def _s5_kernel(uf_ref, ub_ref, pm_ref, pmt_ref,
               bbf_ref, ccf_ref, af_ref, atf_ref, apf_ref,
               bbb_ref, ccb_ref, ab_ref, atb_ref, apb_ref,
               yf_ref, yb_ref, xsf_ref, xsb_ref, xbf_ref, xbb_ref, carf_ref, carb_ref, *, tc):
    tc8 = tc // 8
    cw = SSM_WIDTH // S5_BLOCKS
    sw = N_STATE // S5_BLOCKS

    @pl.when(pl.program_id(1) == 0)
    def _():
        carf_ref[...] = jnp.zeros_like(carf_ref)
        carb_ref[...] = jnp.zeros_like(carb_ref)

    def direction(u_ref, bb_ref, cc_ref, a_ref, at_ref, apow_ref, y_ref, xs_ref, xb_ref, carry_ref,
                  reverse):
        def bproj():
            up = jnp.dot(pm_ref[...], u_ref[0].astype(BF16), preferred_element_type=F32).astype(BF16)
            for hb in range(S5_BLOCKS):
                bu = jnp.dot(up[:, hb * cw:(hb + 1) * cw], bb_ref[hb], preferred_element_type=F32)
                xs_ref[:, hb * sw:(hb + 1) * sw] = bu[:, :sw]
                xs_ref[:, N_STATE + hb * sw:N_STATE + (hb + 1) * sw] = bu[:, sw:]

        def scan():
            for cb in range(N_STATE // SCAN_LANES):
                lo = cb * SCAN_LANES
                ar = jnp.broadcast_to(a_ref[0:1, lo:lo + SCAN_LANES], (8, SCAN_LANES))
                ai = jnp.broadcast_to(a_ref[1:2, lo:lo + SCAN_LANES], (8, SCAN_LANES))
                xr = xi = None
                for i in range(tc8):
                    r = (tc8 - 1 - i) if reverse else i
                    br = xs_ref[r * 8:r * 8 + 8, lo:lo + SCAN_LANES]
                    bi = xs_ref[r * 8:r * 8 + 8, N_STATE + lo:N_STATE + lo + SCAN_LANES]
                    if i == 0:
                        xr, xi = br, bi
                    else:
                        xr, xi = ar * xr - ai * xi + br, ar * xi + ai * xr + bi
                        xs_ref[r * 8:r * 8 + 8, lo:lo + SCAN_LANES] = xr
                        xs_ref[r * 8:r * 8 + 8, N_STATE + lo:N_STATE + lo + SCAN_LANES] = xi

        def fix():
            last = 0 if reverse else (tc8 - 1) * 8
            e_re = xs_ref[last:last + 8, 0:N_STATE]
            e_im = xs_ref[last:last + 8, N_STATE:2 * N_STATE]
            atr, ati = at_ref[0:1, :], at_ref[1:2, :]
            cr, ci = carry_ref[0:1, :], carry_ref[1:2, :]
            cin_r, cin_i = [None] * 8, [None] * 8
            for j in (range(7, -1, -1) if reverse else range(8)):
                cin_r[j], cin_i[j] = cr, ci
                cr, ci = (atr * cr - ati * ci + e_re[j:j + 1], atr * ci + ati * cr + e_im[j:j + 1])
            carry_ref[0:1, :] = cr
            carry_ref[1:2, :] = ci
            cin_r = jnp.concatenate(cin_r + cin_r, axis=0)
            cin_i = jnp.concatenate(cin_i + cin_i, axis=0)
            for i in range(tc // 16):
                rows = slice(i * 16, (i + 1) * 16)
                pr = apow_ref[rows, 0:N_STATE]
                pi = apow_ref[rows, N_STATE:2 * N_STATE]
                xr = xs_ref[rows, 0:N_STATE] + pr * cin_r - pi * cin_i
                xi = xs_ref[rows, N_STATE:2 * N_STATE] + pr * cin_i + pi * cin_r
                xb_ref[rows, 0:N_STATE] = xr.astype(BF16)
                xb_ref[rows, N_STATE:2 * N_STATE] = xi.astype(BF16)

        def cproj():
            yp = jnp.concatenate(
                [jnp.dot(xb_ref[:, hb * sw:(hb + 1) * sw], cc_ref[hb, 0], preferred_element_type=F32)
                 + jnp.dot(xb_ref[:, N_STATE + hb * sw:N_STATE + (hb + 1) * sw], cc_ref[hb, 1],
                           preferred_element_type=F32) for hb in range(S5_BLOCKS)], axis=1)
            hi = yp.astype(BF16)
            lo_part = (yp - hi.astype(F32)).astype(BF16)
            y_ref[0] = (jnp.dot(pmt_ref[...], hi, preferred_element_type=F32)
                        + jnp.dot(pmt_ref[...], lo_part, preferred_element_type=F32))

        return bproj, scan, fix, cproj

    fwd = direction(uf_ref, bbf_ref, ccf_ref, af_ref, atf_ref, apf_ref, yf_ref, xsf_ref, xbf_ref,
                    carf_ref, False)
    bwd = direction(ub_ref, bbb_ref, ccb_ref, ab_ref, atb_ref, apb_ref, yb_ref, xsb_ref, xbb_ref,
                    carb_ref, True)
    for stage_f, stage_b in zip(fwd, bwd):
        stage_f()
        stage_b()


def _s5(u, pm, pmt, par_f, par_b, tc):
    b, l, _ = u.shape
    nc = l // tc
    fmap = lambda bi, c: (bi, c, 0)
    bmap = lambda bi, c: (bi, nc - 1 - c, 0)
    full = lambda arr: pl.BlockSpec(arr.shape, lambda bi, c: (0,) * arr.ndim)
    kernel = functools.partial(_s5_kernel, tc=tc)
    state = [pltpu.VMEM((tc, 2 * N_STATE), F32)] * 2 + [pltpu.VMEM((tc, 2 * N_STATE), BF16)] * 2
    return pl.pallas_call(
        kernel,
        grid=(b, nc),
        in_specs=[pl.BlockSpec((1, tc, SSM_WIDTH), fmap), pl.BlockSpec((1, tc, SSM_WIDTH), bmap),
                  full(pm), full(pmt)] + [full(x) for x in par_f] + [full(x) for x in par_b],
        out_specs=[pl.BlockSpec((1, tc, SSM_WIDTH), fmap), pl.BlockSpec((1, tc, SSM_WIDTH), bmap)],
        out_shape=[jax.ShapeDtypeStruct((b, l, SSM_WIDTH), F32)] * 2,
        scratch_shapes=state + [pltpu.VMEM((2, N_STATE), F32)] * 2,
        compiler_params=_cparams(("parallel", "arbitrary")),
        name="s5",
    )(u, u, pm, pmt, *par_f, *par_b)


def _s5_params(lam_re, lam_im, log_dt, b_re, b_im, c_re, c_im, tc, reverse):
    tc8 = tc // 8
    dt = jnp.exp(log_dt)[:, None]
    mag = jnp.exp(lam_re * dt)
    ab_re = mag * jnp.cos(lam_im * dt)
    ab_im = mag * jnp.sin(lam_im * dt)
    den = lam_re * lam_re + lam_im * lam_im
    nr, ni = ab_re - 1.0, ab_im
    co_re = (nr * lam_re + ni * lam_im) / den
    co_im = (ni * lam_re - nr * lam_im) / den
    bb_re = co_re[..., None] * b_re - co_im[..., None] * b_im
    bb_im = co_re[..., None] * b_im + co_im[..., None] * b_re
    gb = N_GROUPS // S5_BLOCKS
    cw, sw = SSM_WIDTH // S5_BLOCKS, N_STATE // S5_BLOCKS
    eye = jnp.eye(gb, dtype=F32)
    blk_in = lambda w: jnp.einsum('bgph,gk->bghkp', w.reshape(S5_BLOCKS, gb, SSM_STATE, SSM_GROUP),
                                  eye).reshape(S5_BLOCKS, cw, sw)
    bb = jnp.concatenate([blk_in(bb_re), blk_in(bb_im)], axis=2).astype(BF16)
    blk_out = lambda w: jnp.einsum('bghp,gk->bgpkh', w.reshape(S5_BLOCKS, gb, SSM_GROUP, SSM_STATE),
                                   eye).reshape(S5_BLOCKS, sw, cw)
    cc = jnp.stack([blk_out(c_re), -blk_out(c_im)], axis=1).astype(BF16)
    a = jnp.stack([ab_re.reshape(-1), ab_im.reshape(-1)])
    n = jnp.arange(1, tc8 + 1, dtype=F32)[:, None, None]
    pmag = jnp.exp(n * (lam_re * dt)[None])
    ang = n * (lam_im * dt)[None]
    pw_re = (pmag * jnp.cos(ang)).reshape(tc8, N_STATE)
    pw_im = (pmag * jnp.sin(ang)).reshape(tc8, N_STATE)
    at = jnp.stack([pw_re[-1], pw_im[-1]])
    if reverse:
        pw_re, pw_im = pw_re[::-1], pw_im[::-1]
    apow = jnp.concatenate([jnp.repeat(pw_re, 8, axis=0), jnp.repeat(pw_im, 8, axis=0)], axis=1)
    return bb, cc, a, at, apow


def _perm_matrices(tc):
    tc8 = tc // 8
    i = np.arange(tc)
    src = (i % 8) * tc8 + i // 8
    pm = np.zeros((tc, tc), np.float32)
    pm[i, src] = 1.0
    return jnp.asarray(pm, BF16), jnp.asarray(pm.T, BF16)


def _mix_kernel(h_ref, a_ref, u_ref, yf_ref, yb_ref, d_ref, wglu_ref, bglu_ref, wout_ref,
                gffn_ref, wr_ref, br_ref, tri_ref, h1_ref, hn_ref, ti_ref, tg_ref, counts_ref, cnt_ref):
    y = d_ref[...] * u_ref[...] + yf_ref[...] + yb_ref[...]
    y = jax.nn.gelu(y)
    z = jnp.dot(y.astype(BF16), wglu_ref[...], preferred_element_type=F32) + bglu_ref[...]
    s = y * jax.nn.sigmoid(z)
    h1 = (h_ref[...]
          + jnp.dot(a_ref[...], wout_ref[0:ATTN_WIDTH, :], preferred_element_type=F32)
          + jnp.dot(s.astype(BF16), wout_ref[ATTN_WIDTH:, :], preferred_element_type=F32))
    h1_ref[...] = h1
    hn = _rms(h1, gffn_ref[...])
    hn_ref[...] = hn
    hn_hi = hn.astype(BF16)
    hn_lo = (hn - hn_hi.astype(F32)).astype(BF16)
    logits = (jnp.dot(hn_hi, wr_ref[0], preferred_element_type=F32)
              + jnp.dot(hn_lo, wr_ref[0], preferred_element_type=F32)
              + jnp.dot(hn_hi, wr_ref[1], preferred_element_type=F32)) + br_ref[...]
    lane = lax.broadcasted_iota(jnp.int32, logits.shape, 1)
    neg = jnp.float32(-jnp.inf)
    work = logits
    ti = jnp.zeros(logits.shape, jnp.int32)
    tv = jnp.full(logits.shape, neg, F32)
    picked = jnp.zeros(logits.shape, F32)
    idxs = []
    for kk in range(TOP_K):
        mx = jnp.max(work, axis=-1, keepdims=True)
        idx = jnp.min(jnp.where(work == mx, lane, LANES), axis=-1, keepdims=True)
        idxs.append(idx)
        ti = jnp.where(lane == kk, idx, ti)
        tv = jnp.where(lane == kk, mx, tv)
        picked = jnp.where(lane == idx, 1.0, picked)
        work = jnp.where(lane == idx, neg, work)
    ex = jnp.exp(tv - jnp.max(tv, axis=-1, keepdims=True))
    tg_ref[...] = ex / jnp.sum(ex, axis=-1, keepdims=True)

    @pl.when(pl.program_id(0) == 0)
    def _():
        cnt_ref[...] = jnp.zeros_like(cnt_ref)

    before = jnp.dot(tri_ref[...], picked.astype(BF16), preferred_element_type=F32) + cnt_ref[0:1, :]
    for kk in range(TOP_K):
        rank = jnp.sum(jnp.where(lane == idxs[kk], before, 0.0), axis=-1, keepdims=True)
        ti = jnp.where(lane == TOP_K + kk, rank.astype(jnp.int32), ti)
    ti_ref[...] = ti
    cnt_ref[...] = cnt_ref[...] + jnp.sum(picked, axis=0, keepdims=True)
    counts_ref[...] = cnt_ref[...]


def _mix(h, a, u, yf, yb, d, wglu_bf, bglu, wout_bf, g_ffn, wr_pad, br_pad, tm):
    t = h.shape[0]
    row = lambda i: (i, 0)
    const = lambda i: (0, 0)
    full = lambda arr: pl.BlockSpec(arr.shape, lambda i: (0,) * arr.ndim)
    tri = jnp.asarray(np.tril(np.ones((tm, tm), np.float32), -1), BF16)
    return pl.pallas_call(
        _mix_kernel,
        grid=(t // tm,),
        in_specs=[pl.BlockSpec((tm, D_MODEL), row), pl.BlockSpec((tm, ATTN_WIDTH), row),
                  pl.BlockSpec((tm, SSM_WIDTH), row), pl.BlockSpec((tm, SSM_WIDTH), row),
                  pl.BlockSpec((tm, SSM_WIDTH), row),
                  full(d), full(wglu_bf), full(bglu), full(wout_bf), full(g_ffn), full(wr_pad),
                  full(br_pad), full(tri)],
        out_specs=[pl.BlockSpec((tm, D_MODEL), row), pl.BlockSpec((tm, D_MODEL), row),
                   pl.BlockSpec((tm, LANES), row), pl.BlockSpec((tm, LANES), row),
                   pl.BlockSpec((8, LANES), const)],
        out_shape=[jax.ShapeDtypeStruct((t, D_MODEL), F32), jax.ShapeDtypeStruct((t, D_MODEL), F32),
                   jax.ShapeDtypeStruct((t, LANES), jnp.int32), jax.ShapeDtypeStruct((t, LANES), F32),
                   jax.ShapeDtypeStruct((8, LANES), F32)],
        scratch_shapes=[pltpu.VMEM((8, LANES), F32)],
        compiler_params=_cparams(("arbitrary",)),
        name="mix",
    )(h, a, u, yf, yb, d, wglu_bf, bglu, wout_bf, g_ffn, wr_pad, br_pad, tri)


ROWS_PER_ISSUE = 4

def _dispatch_kernel(dest_ref, pad_start_ref, pad_len_ref, hn_ref, xs_ref, zrow_ref, sem, zsem, *, tm):
    @pl.when(pl.program_id(0) == 0)
    def _():
        zrow_ref[...] = jnp.zeros_like(zrow_ref)

        def zero_copy(row):
            return pltpu.make_async_copy(zrow_ref.at[pl.ds(0, 1), :], xs_ref.at[pl.ds(row, 1), :], zsem)

        def start_span(e, _):
            base = pad_start_ref[e]
            lax.fori_loop(0, pad_len_ref[e], lambda r, c: (zero_copy(base + r).start(), c)[1], 0)
            return 0

        def wait_span(e, _):
            lax.fori_loop(0, pad_len_ref[e], lambda r, c: (zero_copy(0).wait(), c)[1], 0)
            return 0

        lax.fori_loop(0, N_EXPERTS + 1, start_span, 0)
        lax.fori_loop(0, N_EXPERTS + 1, wait_span, 0)

    def issue(g, _):
        for rr in range(ROWS_PER_ISSUE):
            r = g * ROWS_PER_ISSUE + rr
            for kk in range(TOP_K):
                pltpu.make_async_copy(hn_ref.at[pl.ds(r, 1), :],
                                      xs_ref.at[pl.ds(dest_ref[r * TOP_K + kk], 1), :], sem).start()
        return 0

    lax.fori_loop(0, tm // ROWS_PER_ISSUE, issue, 0)

    def drain(g, _):
        for _unused in range(ROWS_PER_ISSUE * TOP_K):
            pltpu.make_async_copy(hn_ref.at[pl.ds(0, 1), :], xs_ref.at[pl.ds(0, 1), :], sem).wait()
        return 0

    lax.fori_loop(0, tm // ROWS_PER_ISSUE, drain, 0)


def _dispatch(dest_flat, pad_start, pad_len, hn, n_pad, tm):
    t = hn.shape[0]
    kernel = functools.partial(_dispatch_kernel, tm=tm)
    return pl.pallas_call(
        kernel,
        grid=(t // tm,),
        in_specs=[pl.BlockSpec((tm * TOP_K,), lambda i: (i,), memory_space=pltpu.SMEM),
                  pl.BlockSpec(memory_space=pltpu.SMEM), pl.BlockSpec(memory_space=pltpu.SMEM),
                  pl.BlockSpec((tm, D_MODEL), lambda i: (i, 0))],
        out_specs=pl.BlockSpec(memory_space=pl.ANY),
        out_shape=jax.ShapeDtypeStruct((n_pad, D_MODEL), F32),
        scratch_shapes=[pltpu.VMEM((8, D_MODEL), F32), pltpu.SemaphoreType.DMA(()),
                        pltpu.SemaphoreType.DMA(())],
        compiler_params=_cparams(("arbitrary",)),
        name="dispatch",
    )(dest_flat, pad_start, pad_len, hn)


def _experts_kernel(be_ref, x_ref, wg_ref, bg_ref, wu_ref, bu_ref, wd_ref, bd_ref, y_ref,
                    wgb_ref, wub_ref, wdb_ref):
    i = pl.program_id(0)

    @pl.when((i == 0) | (be_ref[i] != be_ref[jnp.maximum(i - 1, 0)]))
    def _():
        wgb_ref[...] = wg_ref[0].astype(BF16)
        wub_ref[...] = wu_ref[0].astype(BF16)
        wdb_ref[...] = wd_ref[0].astype(BF16)

    x = x_ref[...].astype(BF16)
    gt = jnp.minimum(jnp.dot(x, wgb_ref[...], preferred_element_type=F32) + bg_ref[0], SWIGLU_LIMIT)
    up = jnp.clip(jnp.dot(x, wub_ref[...], preferred_element_type=F32) + bu_ref[0],
                  -SWIGLU_LIMIT, SWIGLU_LIMIT)
    hh = (up + 1.0) * (gt * jax.nn.sigmoid(SWIGLU_ALPHA * gt))
    y_ref[...] = jnp.dot(hh.astype(BF16), wdb_ref[...], preferred_element_type=F32) + bd_ref[0]


def _experts(blk_expert, xs, wg, bg, wu, bu, wd, bd, eb):
    n_pad = xs.shape[0]
    row = lambda i, be: (i, 0)
    wmap = lambda i, be: (be[i], 0, 0)
    wspec = pl.BlockSpec((1, D_MODEL, D_MODEL), wmap)
    bspec = pl.BlockSpec((1, 1, D_MODEL), wmap)
    grid_spec = pltpu.PrefetchScalarGridSpec(
        num_scalar_prefetch=1,
        grid=(n_pad // eb,),
        in_specs=[pl.BlockSpec((eb, D_MODEL), row), wspec, bspec, wspec, bspec, wspec, bspec],
        out_specs=pl.BlockSpec((eb, D_MODEL), row),
        scratch_shapes=[pltpu.VMEM((D_MODEL, D_MODEL), BF16)] * 3,
    )
    return pl.pallas_call(
        _experts_kernel,
        grid_spec=grid_spec,
        out_shape=jax.ShapeDtypeStruct((n_pad, D_MODEL), F32),
        compiler_params=_cparams(("arbitrary",)),
        name="experts",
    )(blk_expert, xs, wg, bg, wu, bu, wd, bd)


def _tail_kernel(dcur_ref, dnxt_ref, h1_ref, tg_ref, p_ref, ys_ref, gple_ref, wpg_ref, wpp_ref, gfin_ref,
                 o_ref, bufa_ref, bufb_ref, sema, semb, *, th):
    i = pl.program_id(0)

    def row_copy(idx, buf_ref, kk, r, sem):
        return pltpu.make_async_copy(ys_ref.at[pl.ds(idx, 1), :], buf_ref.at[kk, pl.ds(r, 1), :], sem)

    def gather(d_ref, base, buf_ref, sem):
        for r in range(th):
            for kk in range(TOP_K):
                row_copy(d_ref[base + r * TOP_K + kk], buf_ref, kk, r, sem).start()

    def wait_all(buf_ref, sem):
        for _unused in range(th * TOP_K):
            row_copy(0, buf_ref, 0, 0, sem).wait()

    def compute(rows, buf_ref):
        tg = tg_ref[rows, :]
        h2 = h1_ref[rows, :]
        for kk in range(TOP_K):
            h2 = h2 + tg[:, kk:kk + 1] * buf_ref[kk]
        gate = jax.nn.sigmoid(jnp.dot(_rms(h2, gple_ref[...]).astype(BF16), wpg_ref[...],
                                      preferred_element_type=F32))
        proj = jnp.dot(p_ref[rows, :].astype(BF16), wpp_ref[...], preferred_element_type=F32)
        o_ref[rows, :] = _rms(h2 + gate * proj, gfin_ref[...])

    @pl.when(i == 0)
    def _():
        gather(dcur_ref, 0, bufa_ref, sema)

    wait_all(bufa_ref, sema)
    gather(dcur_ref, th * TOP_K, bufb_ref, semb)
    compute(slice(0, th), bufa_ref)
    wait_all(bufb_ref, semb)
    gather(dnxt_ref, 0, bufa_ref, sema)
    compute(slice(th, 2 * th), bufb_ref)

    @pl.when(i == pl.num_programs(0) - 1)
    def _():
        wait_all(bufa_ref, sema)


def _tail(dest_flat, h1, tg, p, ys, g_ple, wpg_bf, wpp_bf, g_final, th):
    t = h1.shape[0]
    tm = 2 * th
    n = t // tm
    row = lambda i: (i, 0)
    full = lambda arr: pl.BlockSpec(arr.shape, lambda i: (0,) * arr.ndim)
    kernel = functools.partial(_tail_kernel, th=th)
    return pl.pallas_call(
        kernel,
        grid=(n,),
        in_specs=[pl.BlockSpec((tm * TOP_K,), lambda i: (i,), memory_space=pltpu.SMEM),
                  pl.BlockSpec((tm * TOP_K,), lambda i: (jnp.minimum(i + 1, n - 1),),
                               memory_space=pltpu.SMEM),
                  pl.BlockSpec((tm, D_MODEL), row), pl.BlockSpec((tm, LANES), row),
                  pl.BlockSpec((tm, PLE_DIM), row),
                  pl.BlockSpec(memory_space=pl.ANY),
                  full(g_ple), full(wpg_bf), full(wpp_bf), full(g_final)],
        out_specs=pl.BlockSpec((tm, D_MODEL), row),
        out_shape=jax.ShapeDtypeStruct((t, D_MODEL), F32),
        scratch_shapes=[pltpu.VMEM((TOP_K, th, D_MODEL), F32), pltpu.VMEM((TOP_K, th, D_MODEL), F32),
                        pltpu.SemaphoreType.DMA(()), pltpu.SemaphoreType.DMA(())],
        compiler_params=_cparams(("arbitrary",)),
        name="tail",
    )(dest_flat, dest_flat, h1, tg, p, ys, g_ple, wpg_bf, wpp_bf, g_final)


def _routing(ti, counts, eb):
    t = ti.shape[0]
    top_i, rank = ti[:, :TOP_K], ti[:, TOP_K:2 * TOP_K]
    counts = counts[0, :N_EXPERTS].astype(jnp.int32)
    padded = ((counts + eb - 1) // eb) * eb
    pends = jnp.cumsum(padded)
    pstarts = pends - padded
    onehot = top_i[:, :, None] == jnp.arange(N_EXPERTS, dtype=jnp.int32)
    dest = rank + jnp.sum(jnp.where(onehot, pstarts, 0), axis=-1)
    n_blocks = -(-(t * TOP_K + N_EXPERTS * (eb - 1)) // eb)
    blk_start = jnp.arange(n_blocks, dtype=jnp.int32) * eb
    blk_expert = jnp.minimum(jnp.sum(blk_start[:, None] >= pends[None, :], axis=1), N_EXPERTS - 1)
    n_pad = n_blocks * eb
    pad_start = jnp.concatenate([pstarts + counts, pends[-1:]]).astype(jnp.int32)
    pad_len = jnp.concatenate([padded - counts, n_pad - pends[-1:]]).astype(jnp.int32)
    return (dest.reshape(-1).astype(jnp.int32), blk_expert.astype(jnp.int32), pad_start, pad_len, n_pad)


def _pick(n, prefs):
    for c in prefs:
        if n % c == 0:
            return c
    raise ValueError(f"no tile for {n}")


def _trunk(x, p, w, lambda_init):
    b, l, _ = x.shape
    t = b * l
    tm = TOKEN_TILE
    assert l % tm == 0
    tc = _pick(l, (256, 128))
    eb = 256

    q, k, vt, u, nrm = _inproj(x.reshape(t, D_MODEL), w['g_mix'], w['w_qku'], w['w_vt'], tm)
    a = _attention(q.reshape(b, l, -1), k.reshape(b, l, -1), vt, w['attn_scal'],
                   _score_bounds(nrm, b, l // tm), w['attn_qf'], w['attn_kf'], w['g_subln'],
                   lambda_init, tm)
    u3 = u.reshape(b, l, SSM_WIDTH)
    pm, pmt = _perm_matrices(tc)
    ys = _s5(u3, pm, pmt, w['s5'][0], w['s5'][1], tc)
    h1, hn, ti, tg, counts = _mix(x.reshape(t, D_MODEL), a.reshape(t, ATTN_WIDTH), u,
                                  ys[0].reshape(t, SSM_WIDTH), ys[1].reshape(t, SSM_WIDTH),
                                  w['ssm_d'], w['w_glu'], w['b_glu'], w['w_out'], w['g_ffn'],
                                  w['w_router'], w['b_router'], tm)
    dest, blk_expert, pad_start, pad_len, n_pad = _routing(ti, counts, eb)
    xs = _dispatch(dest, pad_start, pad_len, hn, n_pad, tm // 2)
    ye = _experts(blk_expert, xs, w['w_gate'], w['b_gate'], w['w_up'], w['b_up'],
                  w['w_down'], w['b_down'], eb)
    out = _tail(dest, h1, tg, p.reshape(t, PLE_DIM), ye, w['g_ple'], w['w_ple_gate'],
                w['w_ple_proj'], w['g_final'], tm // 2)
    return out.reshape(b, l, D_MODEL)


def _hi_lo(x):
    hi = x.astype(BF16)
    return jnp.stack([hi, (x - hi.astype(F32)).astype(BF16)])


def _prepare(i, g_mix, w_in, lambda_q1, lambda_k1, lambda_q2, lambda_k2, g_subln, ssm_lambda_re,
             ssm_lambda_im, ssm_log_dt, ssm_b_re, ssm_b_im, ssm_c_re, ssm_c_im, ssm_d, w_glu, b_glu,
             w_out, g_ffn, w_router, b_router, w_gate, b_gate, w_up, b_up, w_down, b_down, g_ple,
             w_ple_gate, w_ple_proj, g_final, tcs):
    lambda_init = 0.8 - 0.6 * math.exp(-0.3 * i)
    lam = (jnp.exp(jnp.sum(lambda_q1[i] * lambda_k1[i]))
           - jnp.exp(jnp.sum(lambda_q2[i] * lambda_k2[i])) + lambda_init)
    slopes = jnp.exp2(-8.0 * jnp.arange(1, N_HEADS + 1, dtype=F32) / N_HEADS)
    bslopes = slopes * LOG2E
    parts, rest = [], bslopes
    for _ in range(N_BIAS_PARTS):
        part = rest.astype(BF16).astype(F32)
        parts.append(part)
        rest = rest - part
    attn_qf, attn_kf = _attn_features(jnp.stack(parts, axis=1), TOKEN_TILE)
    w_qku = jnp.concatenate([w_in[i][:, :2 * ATTN_WIDTH], w_in[i][:, 3 * ATTN_WIDTH:]], axis=1)
    row = lambda vec: vec.reshape(1, -1).astype(F32)
    w = {
        'g_mix': g_mix[i], 'w_qku': w_qku.astype(BF16),
        'w_vt': w_in[i][:, 2 * ATTN_WIDTH:3 * ATTN_WIDTH].T.astype(BF16),
        'attn_scal': jnp.concatenate([lam.reshape(1), bslopes, 1.0 / bslopes]).astype(F32),
        'attn_qf': attn_qf, 'attn_kf': attn_kf,
        'g_subln': g_subln[i],
        'ssm_d': row(ssm_d[i]), 'w_glu': w_glu[i].astype(BF16), 'b_glu': row(b_glu[i]),
        'w_out': w_out[i].astype(BF16), 'g_ffn': row(g_ffn[i]),
        'w_router': _hi_lo(jnp.pad(w_router[i], ((0, 0), (0, LANES - N_EXPERTS)))),
        'b_router': jnp.pad(row(b_router[i]), ((0, 0), (0, LANES - N_EXPERTS)),
                            constant_values=-jnp.inf),
        'w_gate': w_gate[i], 'b_gate': b_gate[i].reshape(N_EXPERTS, 1, D_MODEL),
        'w_up': w_up[i], 'b_up': b_up[i].reshape(N_EXPERTS, 1, D_MODEL),
        'w_down': w_down[i], 'b_down': b_down[i].reshape(N_EXPERTS, 1, D_MODEL),
        'g_ple': row(g_ple[i]), 'w_ple_gate': w_ple_gate[i].astype(BF16),
        'w_ple_proj': w_ple_proj[i].astype(BF16), 'g_final': row(g_final),
    }
    w['s5'] = {tc: [_s5_params(ssm_lambda_re[i, dr], ssm_lambda_im[i, dr], ssm_log_dt[i, dr],
                               ssm_b_re[i, dr], ssm_b_im[i, dr], ssm_c_re[i, dr], ssm_c_im[i, dr],
                               tc, reverse=(dr == 1)) for dr in range(2)] for tc in tcs}
    return w, lambda_init


def kernel(x_prompt, x_sample, p_prompt, p_sample, g_mix, w_in, lambda_q1, lambda_k1, lambda_q2, lambda_k2, g_subln, ssm_lambda_re, ssm_lambda_im, ssm_log_dt, ssm_b_re, ssm_b_im, ssm_c_re, ssm_c_im, ssm_d, w_glu, b_glu, w_out, g_ffn, w_router, b_router, w_gate, b_gate, w_up, b_up, w_down, b_down, g_ple, w_ple_gate, w_ple_proj, g_final):
    assert w_in.shape[0] == 1, "single-layer trunk"
    tcs = {_pick(x.shape[1], (256, 128)) for x in (x_prompt, x_sample)}
    w, lambda_init = _prepare(0, g_mix, w_in, lambda_q1, lambda_k1, lambda_q2, lambda_k2, g_subln,
                              ssm_lambda_re, ssm_lambda_im, ssm_log_dt, ssm_b_re, ssm_b_im, ssm_c_re,
                              ssm_c_im, ssm_d, w_glu, b_glu, w_out, g_ffn, w_router, b_router, w_gate,
                              b_gate, w_up, b_up, w_down, b_down, g_ple, w_ple_gate, w_ple_proj,
                              g_final, tcs)
    outs = []
    for x, p in ((x_prompt, p_prompt), (x_sample, p_sample)):
        tc = _pick(x.shape[1], (256, 128))
        wt = dict(w, s5=w['s5'][tc])
        outs.append(_trunk(x, p[0], wt, lambda_init))
    return tuple(outs)
```

```python
import functools
import math

import jax
import jax.numpy as jnp
import numpy as np
from jax import lax
from jax.experimental import pallas as pl
from jax.experimental.pallas import tpu as pltpu

F32 = jnp.float32
BF16 = jnp.bfloat16

D_MODEL = 1024
PLE_DIM = 256
ATTN_WIDTH = 512
SSM_WIDTH = 512
HEAD_W = 128
DIFF_HEAD_DIM = 64
N_HEADS = 4
SSM_GROUP = 16
N_GROUPS = 32
SSM_STATE = 64
N_STATE = N_GROUPS * SSM_STATE
N_EXPERTS = 32
TOP_K = 4
SWIGLU_LIMIT = 7.0
SWIGLU_ALPHA = 1.702
EPS = 1e-6
LANES = 128
TOKEN_TILE = 512

VMEM_LIMIT = 56 * 1024 * 1024


def _cparams(sem):
    return pltpu.CompilerParams(dimension_semantics=sem, vmem_limit_bytes=VMEM_LIMIT)


def _rms(x, g):
    return x * lax.rsqrt(jnp.mean(x * x, axis=-1, keepdims=True) + EPS) * g


LOG2E = 1.4426950408889634
NT_DIMS = (((1,), (1,)), ((), ()))


def _inproj_kernel(x_ref, g_ref, wqku_ref, wvt_ref, seg_ref, q_ref, k_ref, vt_ref, u_ref, nrm_ref):
    xn = _rms(x_ref[...], g_ref[...]).astype(BF16)
    proj = jnp.dot(xn, wqku_ref[...], preferred_element_type=F32)
    scale = LOG2E / math.sqrt(DIFF_HEAD_DIM)
    qk = jnp.concatenate([proj[:, :ATTN_WIDTH] * scale, proj[:, ATTN_WIDTH:2 * ATTN_WIDTH]], axis=1)
    q_ref[...] = qk[:, :ATTN_WIDTH].astype(BF16)
    k_ref[...] = qk[:, ATTN_WIDTH:].astype(BF16)
    u_ref[...] = proj[:, 2 * ATTN_WIDTH:]
    nsq = jnp.dot((qk * qk).astype(BF16), seg_ref[...], preferred_element_type=F32)
    nrm_ref[0] = jnp.max(nsq.reshape(nsq.shape[0] // 8, 8, LANES), axis=0)
    vt_ref[0] = lax.dot_general(wvt_ref[...], xn, NT_DIMS, preferred_element_type=F32).astype(BF16)


def _inproj(x, g_mix, w_qku_bf, w_vt_bf, tm):
    t = x.shape[0]
    row = lambda i: (i, 0)
    const = lambda i: (0, 0)
    col = np.arange(2 * ATTN_WIDTH)
    seg = np.zeros((2 * ATTN_WIDTH, LANES), np.float32)
    seg[col, col // DIFF_HEAD_DIM] = 1.0
    seg = jnp.asarray(seg, BF16)
    return pl.pallas_call(
        _inproj_kernel,
        grid=(t // tm,),
        in_specs=[pl.BlockSpec((tm, D_MODEL), row),
                  pl.BlockSpec((1, D_MODEL), const),
                  pl.BlockSpec(w_qku_bf.shape, const),
                  pl.BlockSpec(w_vt_bf.shape, const),
                  pl.BlockSpec(seg.shape, const)],
        out_specs=[pl.BlockSpec((tm, ATTN_WIDTH), row), pl.BlockSpec((tm, ATTN_WIDTH), row),
                   pl.BlockSpec((1, ATTN_WIDTH, tm), lambda i: (i, 0, 0)),
                   pl.BlockSpec((tm, SSM_WIDTH), row),
                   pl.BlockSpec((1, 8, LANES), lambda i: (i, 0, 0))],
        out_shape=[jax.ShapeDtypeStruct((t, ATTN_WIDTH), BF16), jax.ShapeDtypeStruct((t, ATTN_WIDTH), BF16),
                   jax.ShapeDtypeStruct((t // tm, ATTN_WIDTH, tm), BF16),
                   jax.ShapeDtypeStruct((t, SSM_WIDTH), F32),
                   jax.ShapeDtypeStruct((t // tm, 8, LANES), F32)],
        compiler_params=_cparams(("parallel",)),
        name="inproj",
    )(x, g_mix.reshape(1, D_MODEL), w_qku_bf, w_vt_bf, seg)


ONES_ROWS = 16
ACC_ROWS = HEAD_W + ONES_ROWS
POS_SPLIT = 16
N_BIAS_PARTS = 3
SKIP_BITS = 130.0
NORM_INFLATE = 1.0 + 2.0 ** -5


def _attn_kernel(sc_ref, ub_ref, q_ref, k_ref, vt_ref, qf_ref, kf_ref, g_ref, o_ref,
                 qa_ref, m_ref, acc_ref, sa_ref, sb_ref, pa_ref, pb_ref, ala_ref, alb_ref,
                 *, ts, out_scale):
    bi = pl.program_id(0)
    h = pl.program_id(1)
    qi = pl.program_id(2)
    nsub = k_ref.shape[1] // ts
    bslope = sc_ref[1 + h]

    q = q_ref[0]
    lane = lax.broadcasted_iota(jnp.int32, q.shape, 1)
    zero = jnp.zeros_like(q)
    qf = qf_ref[0]
    qa_ref[0] = jnp.concatenate([jnp.where(lane < DIFF_HEAD_DIM, q, zero), qf], axis=1)
    qa_ref[1] = jnp.concatenate([jnp.where(lane >= DIFF_HEAD_DIM, q, zero), qf], axis=1)
    ones = jnp.ones((ONES_ROWS, ts), BF16)

    def keys(j, variant):
        rows = pl.ds(pl.multiple_of(j * ts, ts), ts)
        return jnp.concatenate([k_ref[0, rows, :], kf_ref[0, variant]], axis=1)

    def values(j):
        return jnp.concatenate([vt_ref[j], ones], axis=0)

    kpos = lax.broadcasted_iota(jnp.int32, (ts, ts), 0)
    qpos = lax.broadcasted_iota(jnp.int32, (ts, ts), 1)
    bias = jnp.abs(qpos - kpos).astype(F32) * (-bslope)
    ka = keys(qi, 2)
    vta = values(qi)
    s_diag = [lax.dot_general(ka, qa_ref[st], NT_DIMS, preferred_element_type=F32) for st in range(2)]
    for st in range(2):
        s = s_diag[st] + bias
        m_new = jnp.max(s, axis=0, keepdims=True)
        p = jnp.exp2(s - m_new).astype(BF16)
        acc_ref[st] = jnp.dot(vta, p, preferred_element_type=F32)
        m_ref[st] = m_new

    ub = ub_ref[(bi * N_HEADS + h) * nsub + qi]
    slack = ub - jnp.min(m_ref[...]) + SKIP_BITS
    x = (slack * sc_ref[1 + N_HEADS + h] - 1.0) * (1.0 / ts)
    w = jnp.where(x < 0.0, 0, jnp.minimum(x, float(nsub)).astype(jnp.int32) + 1)
    lo = jnp.maximum(qi - w, 0)
    n_off = jnp.minimum(qi + w, nsub - 1) - lo

    def sub_index(n):
        j = lo + jnp.clip(n, 0, jnp.maximum(n_off - 1, 0))
        return j + (j >= qi).astype(jnp.int32)

    def qk(n, s_ref):
        j = sub_index(n)
        ka = keys(j, jnp.where(j < qi, 0, 1))
        for st in range(2):
            s_ref[st] = lax.dot_general(ka, qa_ref[st], NT_DIMS, preferred_element_type=F32)

    def softmax(n, s_ref, p_ref, al_ref, keep=None):
        j = sub_index(n)
        c = -bslope * (jnp.abs(qi - j) * ts).astype(F32)
        for st in range(2):
            m_old = m_ref[st]
            m_new = jnp.maximum(m_old, jnp.max(s_ref[st], axis=0, keepdims=True) + c)
            al_ref[st] = jnp.exp2(m_old - m_new)
            p = jnp.exp2(s_ref[st] - (m_new - c))
            if keep is not None:
                p = p * keep
            p_ref[st] = p.astype(BF16)
            m_ref[st] = m_new

    def pv(n, p_ref, al_ref):
        vta = values(sub_index(n))
        for st in range(2):
            acc_ref[st] = al_ref[st] * acc_ref[st] + jnp.dot(vta, p_ref[st], preferred_element_type=F32)

    @pl.when(n_off > 0)
    def _():
        for p_ref, al_ref in ((pa_ref, ala_ref), (pb_ref, alb_ref)):
            p_ref[...] = jnp.zeros_like(p_ref)
            al_ref[...] = jnp.ones_like(al_ref)
        qk(0, sa_ref)

        def pair(g, carry):
            n = 2 * g
            pv(n - 2, pa_ref, ala_ref)
            qk(n + 1, sb_ref)
            softmax(n, sa_ref, pa_ref, ala_ref)
            pv(n - 1, pb_ref, alb_ref)
            qk(n + 2, sa_ref)
            softmax(n + 1, sb_ref, pb_ref, alb_ref)
            return carry

        n_pairs = n_off // 2
        lax.fori_loop(0, n_pairs, pair, 0)
        pv(2 * n_pairs - 2, pa_ref, ala_ref)
        pv(2 * n_pairs - 1, pb_ref, alb_ref)
        softmax(n_off - 1, sa_ref, pa_ref, ala_ref, keep=(n_off % 2).astype(F32))
        pv(n_off - 1, pa_ref, ala_ref)

    lam = sc_ref[0]
    a1 = acc_ref[0]
    a2 = acc_ref[1]
    out_t = (a1[:HEAD_W] / a1[HEAD_W:HEAD_W + 1] - lam * (a2[:HEAD_W] / a2[HEAD_W:HEAD_W + 1]))
    ms = jnp.mean(out_t * out_t, axis=0, keepdims=True)
    out_t = out_t * lax.rsqrt(ms + EPS) * (g_ref[...] * out_scale)
    o_ref[0] = out_t.T.astype(o_ref.dtype)


def _attn_features(bparts, ts):
    pos = jnp.arange(ts, dtype=jnp.int32)
    hi = ((pos // POS_SPLIT) * POS_SPLIT).astype(F32)[None, :, None]
    lo = (pos % POS_SPLIT).astype(F32)[None, :, None]
    bp = bparts[:, None, :]
    one = jnp.ones_like(bp)
    qf = jnp.stack([bp + 0 * hi, bp + 0 * hi, hi * one, lo * one], axis=-1)
    kf = jnp.stack([hi * one, lo * one, -bp + 0 * hi, -bp + 0 * hi], axis=-1)
    pad = lambda f: jnp.pad(f.reshape(N_HEADS, ts, 4 * N_BIAS_PARTS),
                            ((0, 0), (0, 0), (0, HEAD_W - 4 * N_BIAS_PARTS)))
    qf, kf = pad(qf), pad(kf)
    kf = jnp.stack([kf, -kf, jnp.zeros_like(kf)], axis=1)
    return qf.astype(BF16), kf.astype(BF16)


def _score_bounds(nrm, b, nsub):
    nrm = jnp.max(nrm, axis=1)[:, :2 * N_HEADS * 2].reshape(b, nsub, 2, N_HEADS, 2)
    q2 = nrm[:, :, 0]
    k2 = jnp.max(nrm[:, :, 1], axis=1)
    ub = jnp.max(jnp.sqrt(q2 * k2[:, None]), axis=-1) * NORM_INFLATE
    return jnp.transpose(ub, (0, 2, 1)).reshape(-1)


def _attention(q, k, vt, scal, ub, qf, kf, g_subln, lambda_init, ts):
    b, l, _ = q.shape
    nsub = l // ts
    kernel = functools.partial(_attn_kernel, ts=ts, out_scale=1.0 - lambda_init)
    return pl.pallas_call(
        kernel,
        grid=(b, N_HEADS, nsub),
        in_specs=[pl.BlockSpec(memory_space=pltpu.SMEM),
                  pl.BlockSpec(memory_space=pltpu.SMEM),
                  pl.BlockSpec((1, ts, HEAD_W), lambda bi, h, qi: (bi, qi, h)),
                  pl.BlockSpec((1, l, HEAD_W), lambda bi, h, qi: (bi, 0, h)),
                  pl.BlockSpec((nsub, HEAD_W, ts), lambda bi, h, qi: (bi, h, 0)),
                  pl.BlockSpec((1, ts, HEAD_W), lambda bi, h, qi: (h, 0, 0)),
                  pl.BlockSpec((1, 3, ts, HEAD_W), lambda bi, h, qi: (h, 0, 0, 0)),
                  pl.BlockSpec((HEAD_W, 1), lambda bi, h, qi: (0, 0))],
        out_specs=pl.BlockSpec((1, ts, HEAD_W), lambda bi, h, qi: (bi, qi, h)),
        out_shape=jax.ShapeDtypeStruct((b, l, ATTN_WIDTH), BF16),
        scratch_shapes=[pltpu.VMEM((2, ts, 2 * HEAD_W), BF16),
                        pltpu.VMEM((2, 1, ts), F32),
                        pltpu.VMEM((2, ACC_ROWS, ts), F32),
                        pltpu.VMEM((2, ts, ts), F32), pltpu.VMEM((2, ts, ts), F32),
                        pltpu.VMEM((2, ts, ts), BF16), pltpu.VMEM((2, ts, ts), BF16),
                        pltpu.VMEM((2, 1, ts), F32), pltpu.VMEM((2, 1, ts), F32)],
        compiler_params=_cparams(("parallel", "parallel", "arbitrary")),
        name="attn",
    )(scal, ub, q, k, vt, qf, kf, g_subln.reshape(HEAD_W, 1))


SCAN_LANES = 512
S5_BLOCKS = 2


def _s5_kernel(uf_ref, ub_ref, pm_ref, pmt_ref,
               bbf_ref, ccf_ref, af_ref, atf_ref, apf_ref,
               bbb_ref, ccb_ref, ab_ref, atb_ref, apb_ref,
               yf_ref, yb_ref, xsf_ref, xsb_ref, xbf_ref, xbb_ref, carf_ref, carb_ref, *, tc):
    tc8 = tc // 8
    cw = SSM_WIDTH // S5_BLOCKS
    sw = N_STATE // S5_BLOCKS

    @pl.when(pl.program_id(1) == 0)
    def _():
        carf_ref[...] = jnp.zeros_like(carf_ref)
        carb_ref[...] = jnp.zeros_like(carb_ref)

    def direction(u_ref, bb_ref, cc_ref, a_ref, at_ref, apow_ref, y_ref, xs_ref, xb_ref, carry_ref,
                  reverse):
        def bproj():
            up = jnp.dot(pm_ref[...], u_ref[0].astype(BF16), preferred_element_type=F32).astype(BF16)
            for hb in range(S5_BLOCKS):
                bu = jnp.dot(up[:, hb * cw:(hb + 1) * cw], bb_ref[hb], preferred_element_type=F32)
                xs_ref[:, hb * sw:(hb + 1) * sw] = bu[:, :sw]
                xs_ref[:, N_STATE + hb * sw:N_STATE + (hb + 1) * sw] = bu[:, sw:]

        def scan():
            for cb in range(N_STATE // SCAN_LANES):
                lo = cb * SCAN_LANES
                ar = jnp.broadcast_to(a_ref[0:1, lo:lo + SCAN_LANES], (8, SCAN_LANES))
                ai = jnp.broadcast_to(a_ref[1:2, lo:lo + SCAN_LANES], (8, SCAN_LANES))
                xr = xi = None
                for i in range(tc8):
                    r = (tc8 - 1 - i) if reverse else i
                    br = xs_ref[r * 8:r * 8 + 8, lo:lo + SCAN_LANES]
                    bi = xs_ref[r * 8:r * 8 + 8, N_STATE + lo:N_STATE + lo + SCAN_LANES]
                    if i == 0:
                        xr, xi = br, bi
                    else:
                        xr, xi = ar * xr - ai * xi + br, ar * xi + ai * xr + bi
                        xs_ref[r * 8:r * 8 + 8, lo:lo + SCAN_LANES] = xr
                        xs_ref[r * 8:r * 8 + 8, N_STATE + lo:N_STATE + lo + SCAN_LANES] = xi

        def fix():
            last = 0 if reverse else (tc8 - 1) * 8
            e_re = xs_ref[last:last + 8, 0:N_STATE]
            e_im = xs_ref[last:last + 8, N_STATE:2 * N_STATE]
            atr, ati = at_ref[0:1, :], at_ref[1:2, :]
            cr, ci = carry_ref[0:1, :], carry_ref[1:2, :]
            cin_r, cin_i = [None] * 8, [None] * 8
            for j in (range(7, -1, -1) if reverse else range(8)):
                cin_r[j], cin_i[j] = cr, ci
                cr, ci = (atr * cr - ati * ci + e_re[j:j + 1], atr * ci + ati * cr + e_im[j:j + 1])
            carry_ref[0:1, :] = cr
            carry_ref[1:2, :] = ci
            cin_r = jnp.concatenate(cin_r + cin_r, axis=0)
            cin_i = jnp.concatenate(cin_i + cin_i, axis=0)
            for i in range(tc // 16):
                rows = slice(i * 16, (i + 1) * 16)
                pr = apow_ref[rows, 0:N_STATE]
                pi = apow_ref[rows, N_STATE:2 * N_STATE]
                xr = xs_ref[rows, 0:N_STATE] + pr * cin_r - pi * cin_i
                xi = xs_ref[rows, N_STATE:2 * N_STATE] + pr * cin_i + pi * cin_r
                xb_ref[rows, 0:N_STATE] = xr.astype(BF16)
                xb_ref[rows, N_STATE:2 * N_STATE] = xi.astype(BF16)

        def cproj():
            yp = jnp.concatenate(
                [jnp.dot(xb_ref[:, hb * sw:(hb + 1) * sw], cc_ref[hb, 0], preferred_element_type=F32)
                 + jnp.dot(xb_ref[:, N_STATE + hb * sw:N_STATE + (hb + 1) * sw], cc_ref[hb, 1],
                           preferred_element_type=F32) for hb in range(S5_BLOCKS)], axis=1)
            hi = yp.astype(BF16)
            lo_part = (yp - hi.astype(F32)).astype(BF16)
            y_ref[0] = (jnp.dot(pmt_ref[...], hi, preferred_element_type=F32)
                        + jnp.dot(pmt_ref[...], lo_part, preferred_element_type=F32))

        return bproj, scan, fix, cproj

    fwd = direction(uf_ref, bbf_ref, ccf_ref, af_ref, atf_ref, apf_ref, yf_ref, xsf_ref, xbf_ref,
                    carf_ref, False)
    bwd = direction(ub_ref, bbb_ref, ccb_ref, ab_ref, atb_ref, apb_ref, yb_ref, xsb_ref, xbb_ref,
                    carb_ref, True)
    for stage_f, stage_b in zip(fwd, bwd):
        stage_f()
        stage_b()


def _s5(u, pm, pmt, par_f, par_b, tc):
    b, l, _ = u.shape
    nc = l // tc
    fmap = lambda bi, c: (bi, c, 0)
    bmap = lambda bi, c: (bi, nc - 1 - c, 0)
    full = lambda arr: pl.BlockSpec(arr.shape, lambda bi, c: (0,) * arr.ndim)
    kernel = functools.partial(_s5_kernel, tc=tc)
    state = [pltpu.VMEM((tc, 2 * N_STATE), F32)] * 2 + [pltpu.VMEM((tc, 2 * N_STATE), BF16)] * 2
    return pl.pallas_call(
        kernel,
        grid=(b, nc),
        in_specs=[pl.BlockSpec((1, tc, SSM_WIDTH), fmap), pl.BlockSpec((1, tc, SSM_WIDTH), bmap),
                  full(pm), full(pmt)] + [full(x) for x in par_f] + [full(x) for x in par_b],
        out_specs=[pl.BlockSpec((1, tc, SSM_WIDTH), fmap), pl.BlockSpec((1, tc, SSM_WIDTH), bmap)],
        out_shape=[jax.ShapeDtypeStruct((b, l, SSM_WIDTH), F32)] * 2,
        scratch_shapes=state + [pltpu.VMEM((2, N_STATE), F32)] * 2,
        compiler_params=_cparams(("parallel", "arbitrary")),
        name="s5",
    )(u, u, pm, pmt, *par_f, *par_b)


def _s5_params(lam_re, lam_im, log_dt, b_re, b_im, c_re, c_im, tc, reverse):
    tc8 = tc // 8
    dt = jnp.exp(log_dt)[:, None]
    mag = jnp.exp(lam_re * dt)
    ab_re = mag * jnp.cos(lam_im * dt)
    ab_im = mag * jnp.sin(lam_im * dt)
    den = lam_re * lam_re + lam_im * lam_im
    nr, ni = ab_re - 1.0, ab_im
    co_re = (nr * lam_re + ni * lam_im) / den
    co_im = (ni * lam_re - nr * lam_im) / den
    bb_re = co_re[..., None] * b_re - co_im[..., None] * b_im
    bb_im = co_re[..., None] * b_im + co_im[..., None] * b_re
    gb = N_GROUPS // S5_BLOCKS
    cw, sw = SSM_WIDTH // S5_BLOCKS, N_STATE // S5_BLOCKS
    eye = jnp.eye(gb, dtype=F32)
    blk_in = lambda w: jnp.einsum('bgph,gk->bghkp', w.reshape(S5_BLOCKS, gb, SSM_STATE, SSM_GROUP),
                                  eye).reshape(S5_BLOCKS, cw, sw)
    bb = jnp.concatenate([blk_in(bb_re), blk_in(bb_im)], axis=2).astype(BF16)
    blk_out = lambda w: jnp.einsum('bghp,gk->bgpkh', w.reshape(S5_BLOCKS, gb, SSM_GROUP, SSM_STATE),
                                   eye).reshape(S5_BLOCKS, sw, cw)
    cc = jnp.stack([blk_out(c_re), -blk_out(c_im)], axis=1).astype(BF16)
    a = jnp.stack([ab_re.reshape(-1), ab_im.reshape(-1)])
    n = jnp.arange(1, tc8 + 1, dtype=F32)[:, None, None]
    pmag = jnp.exp(n * (lam_re * dt)[None])
    ang = n * (lam_im * dt)[None]
    pw_re = (pmag * jnp.cos(ang)).reshape(tc8, N_STATE)
    pw_im = (pmag * jnp.sin(ang)).reshape(tc8, N_STATE)
    at = jnp.stack([pw_re[-1], pw_im[-1]])
    if reverse:
        pw_re, pw_im = pw_re[::-1], pw_im[::-1]
    apow = jnp.concatenate([jnp.repeat(pw_re, 8, axis=0), jnp.repeat(pw_im, 8, axis=0)], axis=1)
    return bb, cc, a, at, apow


def _perm_matrices(tc):
    tc8 = tc // 8
    i = np.arange(tc)
    src = (i % 8) * tc8 + i // 8
    pm = np.zeros((tc, tc), np.float32)
    pm[i, src] = 1.0
    return jnp.asarray(pm, BF16), jnp.asarray(pm.T, BF16)


def _mix_kernel(h_ref, a_ref, u_ref, yf_ref, yb_ref, d_ref, wglu_ref, bglu_ref, wout_ref,
                gffn_ref, wr_ref, br_ref, tri_ref, h1_ref, hn_ref, ti_ref, tg_ref, counts_ref, cnt_ref):
    h1 = h_ref[...] + jnp.dot(a_ref[...], wout_ref[0:ATTN_WIDTH, :], preferred_element_type=F32)
    y = d_ref[...] * u_ref[...] + yf_ref[...] + yb_ref[...]
    y = jax.nn.gelu(y)
    z = jnp.dot(y.astype(BF16), wglu_ref[...], preferred_element_type=F32) + bglu_ref[...]
    s = y * jax.nn.sigmoid(z)
    h1 = h1 + jnp.dot(s.astype(BF16), wout_ref[ATTN_WIDTH:, :], preferred_element_type=F32)
    h1_ref[...] = h1
    hn = _rms(h1, gffn_ref[...])
    hn_ref[...] = hn
    hn_hi = hn.astype(BF16)
    hn_lo = (hn - hn_hi.astype(F32)).astype(BF16)
    logits = (jnp.dot(hn_hi, wr_ref[0], preferred_element_type=F32)
              + jnp.dot(hn_lo, wr_ref[0], preferred_element_type=F32)
              + jnp.dot(hn_hi, wr_ref[1], preferred_element_type=F32)) + br_ref[...]
    lane = lax.broadcasted_iota(jnp.int32, logits.shape, 1)
    neg = jnp.float32(-jnp.inf)
    work = logits
    ti = jnp.zeros(logits.shape, jnp.int32)
    tv = jnp.full(logits.shape, neg, F32)
    picked = jnp.zeros(logits.shape, F32)
    idxs = []
    for kk in range(TOP_K):
        mx = jnp.max(work, axis=-1, keepdims=True)
        idx = jnp.min(jnp.where(work == mx, lane, LANES), axis=-1, keepdims=True)
        idxs.append(idx)
        ti = jnp.where(lane == kk, idx, ti)
        tv = jnp.where(lane == kk, mx, tv)
        picked = jnp.where(lane == idx, 1.0, picked)
        work = jnp.where(lane == idx, neg, work)
    ex = jnp.exp(tv - jnp.max(tv, axis=-1, keepdims=True))
    tg_ref[...] = ex / jnp.sum(ex, axis=-1, keepdims=True)

    @pl.when(pl.program_id(0) == 0)
    def _():
        cnt_ref[...] = jnp.zeros_like(cnt_ref)

    before = jnp.dot(tri_ref[...], picked.astype(BF16), preferred_element_type=F32) + cnt_ref[0:1, :]
    for kk in range(TOP_K):
        rank = jnp.sum(jnp.where(lane == idxs[kk], before, 0.0), axis=-1, keepdims=True)
        ti = jnp.where(lane == TOP_K + kk, rank.astype(jnp.int32), ti)
    ti_ref[...] = ti
    cnt_ref[...] = cnt_ref[...] + jnp.sum(picked, axis=0, keepdims=True)
    counts_ref[...] = cnt_ref[...]


def _mix(h, a, u, yf, yb, d, wglu_bf, bglu, wout_bf, g_ffn, wr_pad, br_pad, tm):
    t = h.shape[0]
    row = lambda i: (i, 0)
    const = lambda i: (0, 0)
    full = lambda arr: pl.BlockSpec(arr.shape, lambda i: (0,) * arr.ndim)
    tri = jnp.asarray(np.tril(np.ones((tm, tm), np.float32), -1), BF16)
    return pl.pallas_call(
        _mix_kernel,
        grid=(t // tm,),
        in_specs=[pl.BlockSpec((tm, D_MODEL), row), pl.BlockSpec((tm, ATTN_WIDTH), row),
                  pl.BlockSpec((tm, SSM_WIDTH), row), pl.BlockSpec((tm, SSM_WIDTH), row),
                  pl.BlockSpec((tm, SSM_WIDTH), row),
                  full(d), full(wglu_bf), full(bglu), full(wout_bf), full(g_ffn), full(wr_pad),
                  full(br_pad), full(tri)],
        out_specs=[pl.BlockSpec((tm, D_MODEL), row), pl.BlockSpec((tm, D_MODEL), row),
                   pl.BlockSpec((tm, LANES), row), pl.BlockSpec((tm, LANES), row),
                   pl.BlockSpec((8, LANES), const)],
        out_shape=[jax.ShapeDtypeStruct((t, D_MODEL), F32), jax.ShapeDtypeStruct((t, D_MODEL), F32),
                   jax.ShapeDtypeStruct((t, LANES), jnp.int32), jax.ShapeDtypeStruct((t, LANES), F32),
                   jax.ShapeDtypeStruct((8, LANES), F32)],
        scratch_shapes=[pltpu.VMEM((8, LANES), F32)],
        compiler_params=_cparams(("arbitrary",)),
        name="mix",
    )(h, a, u, yf, yb, d, wglu_bf, bglu, wout_bf, g_ffn, wr_pad, br_pad, tri)


ROWS_PER_ISSUE = 4

def _dispatch_kernel(dest_ref, pad_start_ref, pad_len_ref, hn_ref, xs_ref, zrow_ref, sem, zsem, *, tm):
    @pl.when(pl.program_id(0) == 0)
    def _():
        zrow_ref[...] = jnp.zeros_like(zrow_ref)

        def zero_copy(row):
            return pltpu.make_async_copy(zrow_ref.at[pl.ds(0, 1), :], xs_ref.at[pl.ds(row, 1), :], zsem)

        def start_span(e, _):
            base = pad_start_ref[e]
            lax.fori_loop(0, pad_len_ref[e], lambda r, c: (zero_copy(base + r).start(), c)[1], 0)
            return 0

        def wait_span(e, _):
            lax.fori_loop(0, pad_len_ref[e], lambda r, c: (zero_copy(0).wait(), c)[1], 0)
            return 0

        lax.fori_loop(0, N_EXPERTS + 1, start_span, 0)
        lax.fori_loop(0, N_EXPERTS + 1, wait_span, 0)

    def issue(g, _):
        for rr in range(ROWS_PER_ISSUE):
            r = g * ROWS_PER_ISSUE + rr
            for kk in range(TOP_K):
                pltpu.make_async_copy(hn_ref.at[pl.ds(r, 1), :],
                                      xs_ref.at[pl.ds(dest_ref[r * TOP_K + kk], 1), :], sem).start()
        return 0

    lax.fori_loop(0, tm // ROWS_PER_ISSUE, issue, 0)

    def drain(g, _):
        for _unused in range(ROWS_PER_ISSUE * TOP_K):
            pltpu.make_async_copy(hn_ref.at[pl.ds(0, 1), :], xs_ref.at[pl.ds(0, 1), :], sem).wait()
        return 0

    lax.fori_loop(0, tm // ROWS_PER_ISSUE, drain, 0)


def _dispatch(dest_flat, pad_start, pad_len, hn, n_pad, tm):
    t = hn.shape[0]
    kernel = functools.partial(_dispatch_kernel, tm=tm)
    return pl.pallas_call(
        kernel,
        grid=(t // tm,),
        in_specs=[pl.BlockSpec((tm * TOP_K,), lambda i: (i,), memory_space=pltpu.SMEM),
                  pl.BlockSpec(memory_space=pltpu.SMEM), pl.BlockSpec(memory_space=pltpu.SMEM),
                  pl.BlockSpec((tm, D_MODEL), lambda i: (i, 0))],
        out_specs=pl.BlockSpec(memory_space=pl.ANY),
        out_shape=jax.ShapeDtypeStruct((n_pad, D_MODEL), F32),
        scratch_shapes=[pltpu.VMEM((8, D_MODEL), F32), pltpu.SemaphoreType.DMA(()),
                        pltpu.SemaphoreType.DMA(())],
        compiler_params=_cparams(("arbitrary",)),
        name="dispatch",
    )(dest_flat, pad_start, pad_len, hn)


def _experts_kernel(be_ref, x_ref, wg_ref, bg_ref, wu_ref, bu_ref, wd_ref, bd_ref, y_ref,
                    wgb_ref, wub_ref, wdb_ref):
    i = pl.program_id(0)

    @pl.when((i == 0) | (be_ref[i] != be_ref[jnp.maximum(i - 1, 0)]))
    def _():
        wgb_ref[...] = wg_ref[0].astype(BF16)
        wub_ref[...] = wu_ref[0].astype(BF16)
        wdb_ref[...] = wd_ref[0].astype(BF16)

    x = x_ref[...].astype(BF16)
    gt = jnp.minimum(jnp.dot(x, wgb_ref[...], preferred_element_type=F32) + bg_ref[0], SWIGLU_LIMIT)
    up = jnp.clip(jnp.dot(x, wub_ref[...], preferred_element_type=F32) + bu_ref[0],
                  -SWIGLU_LIMIT, SWIGLU_LIMIT)
    hh = (up + 1.0) * (gt * jax.nn.sigmoid(SWIGLU_ALPHA * gt))
    y_ref[...] = jnp.dot(hh.astype(BF16), wdb_ref[...], preferred_element_type=F32) + bd_ref[0]


def _experts(blk_expert, xs, wg, bg, wu, bu, wd, bd, eb):
    n_pad = xs.shape[0]
    row = lambda i, be: (i, 0)
    wmap = lambda i, be: (be[i], 0, 0)
    wspec = pl.BlockSpec((1, D_MODEL, D_MODEL), wmap)
    bspec = pl.BlockSpec((1, 1, D_MODEL), wmap)
    grid_spec = pltpu.PrefetchScalarGridSpec(
        num_scalar_prefetch=1,
        grid=(n_pad // eb,),
        in_specs=[pl.BlockSpec((eb, D_MODEL), row), wspec, bspec, wspec, bspec, wspec, bspec],
        out_specs=pl.BlockSpec((eb, D_MODEL), row),
        scratch_shapes=[pltpu.VMEM((D_MODEL, D_MODEL), BF16)] * 3,
    )
    return pl.pallas_call(
        _experts_kernel,
        grid_spec=grid_spec,
        out_shape=jax.ShapeDtypeStruct((n_pad, D_MODEL), F32),
        compiler_params=_cparams(("arbitrary",)),
        name="experts",
    )(blk_expert, xs, wg, bg, wu, bu, wd, bd)


def _tail_kernel(dcur_ref, dnxt_ref, h1_ref, tg_ref, p_ref, ys_ref, gple_ref, wpg_ref, wpp_ref, gfin_ref,
                 o_ref, bufa_ref, bufb_ref, sema, semb, *, th):
    i = pl.program_id(0)

    def row_copy(idx, buf_ref, kk, r, sem):
        return pltpu.make_async_copy(ys_ref.at[pl.ds(idx, 1), :], buf_ref.at[kk, pl.ds(r, 1), :], sem)

    def gather(d_ref, base, buf_ref, sem):
        for r in range(th):
            for kk in range(TOP_K):
                row_copy(d_ref[base + r * TOP_K + kk], buf_ref, kk, r, sem).start()

    def wait_all(buf_ref, sem):
        for _unused in range(th * TOP_K):
            row_copy(0, buf_ref, 0, 0, sem).wait()

    def compute(rows, buf_ref):
        proj = jnp.dot(p_ref[rows, :].astype(BF16), wpp_ref[...], preferred_element_type=F32)
        tg = tg_ref[rows, :]
        h2 = h1_ref[rows, :]
        for kk in range(TOP_K):
            h2 = h2 + tg[:, kk:kk + 1] * buf_ref[kk]
        gate = jax.nn.sigmoid(jnp.dot(_rms(h2, gple_ref[...]).astype(BF16), wpg_ref[...],
                                      preferred_element_type=F32))
        o_ref[rows, :] = _rms(h2 + gate * proj, gfin_ref[...])

    @pl.when(i == 0)
    def _():
        gather(dcur_ref, 0, bufa_ref, sema)

    wait_all(bufa_ref, sema)
    gather(dcur_ref, th * TOP_K, bufb_ref, semb)
    compute(slice(0, th), bufa_ref)
    wait_all(bufb_ref, semb)
    gather(dnxt_ref, 0, bufa_ref, sema)
    compute(slice(th, 2 * th), bufb_ref)

    @pl.when(i == pl.num_programs(0) - 1)
    def _():
        wait_all(bufa_ref, sema)


def _tail(dest_flat, h1, tg, p, ys, g_ple, wpg_bf, wpp_bf, g_final, th):
    t = h1.shape[0]
    tm = 2 * th
    n = t // tm
    row = lambda i: (i, 0)
    full = lambda arr: pl.BlockSpec(arr.shape, lambda i: (0,) * arr.ndim)
    kernel = functools.partial(_tail_kernel, th=th)
    return pl.pallas_call(
        kernel,
        grid=(n,),
        in_specs=[pl.BlockSpec((tm * TOP_K,), lambda i: (i,), memory_space=pltpu.SMEM),
                  pl.BlockSpec((tm * TOP_K,), lambda i: (jnp.minimum(i + 1, n - 1),),
                               memory_space=pltpu.SMEM),
                  pl.BlockSpec((tm, D_MODEL), row), pl.BlockSpec((tm, LANES), row),
                  pl.BlockSpec((tm, PLE_DIM), row),
                  pl.BlockSpec(memory_space=pl.ANY),
                  full(g_ple), full(wpg_bf), full(wpp_bf), full(g_final)],
        out_specs=pl.BlockSpec((tm, D_MODEL), row),
        out_shape=jax.ShapeDtypeStruct((t, D_MODEL), F32),
        scratch_shapes=[pltpu.VMEM((TOP_K, th, D_MODEL), F32), pltpu.VMEM((TOP_K, th, D_MODEL), F32),
                        pltpu.SemaphoreType.DMA(()), pltpu.SemaphoreType.DMA(())],
        compiler_params=_cparams(("arbitrary",)),
        name="tail",
    )(dest_flat, dest_flat, h1, tg, p, ys, g_ple, wpg_bf, wpp_bf, g_final)


def _routing(ti, counts, eb):
    t = ti.shape[0]
    top_i, rank = ti[:, :TOP_K], ti[:, TOP_K:2 * TOP_K]
    counts = counts[0, :N_EXPERTS].astype(jnp.int32)
    padded = ((counts + eb - 1) // eb) * eb
    pends = jnp.cumsum(padded)
    pstarts = pends - padded
    onehot = top_i[:, :, None] == jnp.arange(N_EXPERTS, dtype=jnp.int32)
    dest = rank + jnp.sum(jnp.where(onehot, pstarts, 0), axis=-1)
    n_blocks = -(-(t * TOP_K + N_EXPERTS * (eb - 1)) // eb)
    blk_start = jnp.arange(n_blocks, dtype=jnp.int32) * eb
    blk_expert = jnp.minimum(jnp.sum(blk_start[:, None] >= pends[None, :], axis=1), N_EXPERTS - 1)
    n_pad = n_blocks * eb
    pad_start = jnp.concatenate([pstarts + counts, pends[-1:]]).astype(jnp.int32)
    pad_len = jnp.concatenate([padded - counts, n_pad - pends[-1:]]).astype(jnp.int32)
    return (dest.reshape(-1).astype(jnp.int32), blk_expert.astype(jnp.int32), pad_start, pad_len, n_pad)


def _pick(n, prefs):
    for c in prefs:
        if n % c == 0:
            return c
    raise ValueError(f"no tile for {n}")


def _trunk(x, p, w, lambda_init):
    b, l, _ = x.shape
    t = b * l
    tm = TOKEN_TILE
    assert l % tm == 0
    tc = _pick(l, (256, 128))
    eb = 256

    q, k, vt, u, nrm = _inproj(x.reshape(t, D_MODEL), w['g_mix'], w['w_qku'], w['w_vt'], tm)
    a = _attention(q.reshape(b, l, -1), k.reshape(b, l, -1), vt, w['attn_scal'],
                   _score_bounds(nrm, b, l // tm), w['attn_qf'], w['attn_kf'], w['g_subln'],
                   lambda_init, tm)
    u3 = u.reshape(b, l, SSM_WIDTH)
    pm, pmt = _perm_matrices(tc)
    ys = _s5(u3, pm, pmt, w['s5'][0], w['s5'][1], tc)
    h1, hn, ti, tg, counts = _mix(x.reshape(t, D_MODEL), a.reshape(t, ATTN_WIDTH), u,
                                  ys[0].reshape(t, SSM_WIDTH), ys[1].reshape(t, SSM_WIDTH),
                                  w['ssm_d'], w['w_glu'], w['b_glu'], w['w_out'], w['g_ffn'],
                                  w['w_router'], w['b_router'], tm)
    dest, blk_expert, pad_start, pad_len, n_pad = _routing(ti, counts, eb)
    xs = _dispatch(dest, pad_start, pad_len, hn, n_pad, tm // 2)
    ye = _experts(blk_expert, xs, w['w_gate'], w['b_gate'], w['w_up'], w['b_up'],
                  w['w_down'], w['b_down'], eb)
    out = _tail(dest, h1, tg, p.reshape(t, PLE_DIM), ye, w['g_ple'], w['w_ple_gate'],
                w['w_ple_proj'], w['g_final'], tm // 2)
    return out.reshape(b, l, D_MODEL)


def _hi_lo(x):
    hi = x.astype(BF16)
    return jnp.stack([hi, (x - hi.astype(F32)).astype(BF16)])


def _prepare(i, g_mix, w_in, lambda_q1, lambda_k1, lambda_q2, lambda_k2, g_subln, ssm_lambda_re,
             ssm_lambda_im, ssm_log_dt, ssm_b_re, ssm_b_im, ssm_c_re, ssm_c_im, ssm_d, w_glu, b_glu,
             w_out, g_ffn, w_router, b_router, w_gate, b_gate, w_up, b_up, w_down, b_down, g_ple,
             w_ple_gate, w_ple_proj, g_final, tcs):
    lambda_init = 0.8 - 0.6 * math.exp(-0.3 * i)
    lam = (jnp.exp(jnp.sum(lambda_q1[i] * lambda_k1[i]))
           - jnp.exp(jnp.sum(lambda_q2[i] * lambda_k2[i])) + lambda_init)
    slopes = jnp.exp2(-8.0 * jnp.arange(1, N_HEADS + 1, dtype=F32) / N_HEADS)
    bslopes = slopes * LOG2E
    parts, rest = [], bslopes
    for _ in range(N_BIAS_PARTS):
        part = rest.astype(BF16).astype(F32)
        parts.append(part)
        rest = rest - part
    attn_qf, attn_kf = _attn_features(jnp.stack(parts, axis=1), TOKEN_TILE)
    w_qku = jnp.concatenate([w_in[i][:, :2 * ATTN_WIDTH], w_in[i][:, 3 * ATTN_WIDTH:]], axis=1)
    row = lambda vec: vec.reshape(1, -1).astype(F32)
    w = {
        'g_mix': g_mix[i], 'w_qku': w_qku.astype(BF16),
        'w_vt': w_in[i][:, 2 * ATTN_WIDTH:3 * ATTN_WIDTH].T.astype(BF16),
        'attn_scal': jnp.concatenate([lam.reshape(1), bslopes, 1.0 / bslopes]).astype(F32),
        'attn_qf': attn_qf, 'attn_kf': attn_kf,
        'g_subln': g_subln[i],
        'ssm_d': row(ssm_d[i]), 'w_glu': w_glu[i].astype(BF16), 'b_glu': row(b_glu[i]),
        'w_out': w_out[i].astype(BF16), 'g_ffn': row(g_ffn[i]),
        'w_router': _hi_lo(jnp.pad(w_router[i], ((0, 0), (0, LANES - N_EXPERTS)))),
        'b_router': jnp.pad(row(b_router[i]), ((0, 0), (0, LANES - N_EXPERTS)),
                            constant_values=-jnp.inf),
        'w_gate': w_gate[i], 'b_gate': b_gate[i].reshape(N_EXPERTS, 1, D_MODEL),
        'w_up': w_up[i], 'b_up': b_up[i].reshape(N_EXPERTS, 1, D_MODEL),
        'w_down': w_down[i], 'b_down': b_down[i].reshape(N_EXPERTS, 1, D_MODEL),
        'g_ple': row(g_ple[i]), 'w_ple_gate': w_ple_gate[i].astype(BF16),
        'w_ple_proj': w_ple_proj[i].astype(BF16), 'g_final': row(g_final),
    }
    w['s5'] = {tc: [_s5_params(ssm_lambda_re[i, dr], ssm_lambda_im[i, dr], ssm_log_dt[i, dr],
                               ssm_b_re[i, dr], ssm_b_im[i, dr], ssm_c_re[i, dr], ssm_c_im[i, dr],
                               tc, reverse=(dr == 1)) for dr in range(2)] for tc in tcs}
    return w, lambda_init


def kernel(x_prompt, x_sample, p_prompt, p_sample, g_mix, w_in, lambda_q1, lambda_k1, lambda_q2, lambda_k2, g_subln, ssm_lambda_re, ssm_lambda_im, ssm_log_dt, ssm_b_re, ssm_b_im, ssm_c_re, ssm_c_im, ssm_d, w_glu, b_glu, w_out, g_ffn, w_router, b_router, w_gate, b_gate, w_up, b_up, w_down, b_down, g_ple, w_ple_gate, w_ple_proj, g_final):
    assert w_in.shape[0] == 1, "single-layer trunk"
    tcs = {_pick(x.shape[1], (256, 128)) for x in (x_prompt, x_sample)}
    w, lambda_init = _prepare(0, g_mix, w_in, lambda_q1, lambda_k1, lambda_q2, lambda_k2, g_subln,
                              ssm_lambda_re, ssm_lambda_im, ssm_log_dt, ssm_b_re, ssm_b_im, ssm_c_re,
                              ssm_c_im, ssm_d, w_glu, b_glu, w_out, g_ffn, w_router, b_router, w_gate,
                              b_gate, w_up, b_up, w_down, b_down, g_ple, w_ple_gate, w_ple_proj,
                              g_final, tcs)
    outs = []
    for x, p in ((x_prompt, p_prompt), (x_sample, p_sample)):
        tc = _pick(x.shape[1], (256, 128))
        wt = dict(w, s5=w['s5'][tc])
        outs.append(_trunk(x, p[0], wt, lambda_init))
    return tuple(outs)
```

```python
import functools
import math

import jax
import jax.numpy as jnp
import numpy as np
from jax import lax
from jax.experimental import pallas as pl
from jax.experimental.pallas import tpu as pltpu

F32 = jnp.float32
BF16 = jnp.bfloat16

D_MODEL = 1024
PLE_DIM = 256
ATTN_WIDTH = 512
SSM_WIDTH = 512
HEAD_W = 128
DIFF_HEAD_DIM = 64
N_HEADS = 4
SSM_GROUP = 16
N_GROUPS = 32
SSM_STATE = 64
N_STATE = N_GROUPS * SSM_STATE
N_EXPERTS = 32
TOP_K = 4
SWIGLU_LIMIT = 7.0
SWIGLU_ALPHA = 1.702
EPS = 1e-6
LANES = 128
TOKEN_TILE = 512

VMEM_LIMIT = 56 * 1024 * 1024


def _cparams(sem):
    return pltpu.CompilerParams(dimension_semantics=sem, vmem_limit_bytes=VMEM_LIMIT)


def _rms(x, g):
    return x * lax.rsqrt(jnp.mean(x * x, axis=-1, keepdims=True) + EPS) * g


LOG2E = 1.4426950408889634
NT_DIMS = (((1,), (1,)), ((), ()))


def _inproj_kernel(x_ref, g_ref, wqku_ref, wvt_ref, seg_ref, q_ref, k_ref, vt_ref, u_ref, nrm_ref):
    xn = _rms(x_ref[...], g_ref[...]).astype(BF16)
    proj = jnp.dot(xn, wqku_ref[...], preferred_element_type=F32)
    scale = LOG2E / math.sqrt(DIFF_HEAD_DIM)
    qk = jnp.concatenate([proj[:, :ATTN_WIDTH] * scale, proj[:, ATTN_WIDTH:2 * ATTN_WIDTH]], axis=1)
    q_ref[...] = qk[:, :ATTN_WIDTH].astype(BF16)
    k_ref[...] = qk[:, ATTN_WIDTH:].astype(BF16)
    u_ref[...] = proj[:, 2 * ATTN_WIDTH:]
    nsq = jnp.dot((qk * qk).astype(BF16), seg_ref[...], preferred_element_type=F32)
    nrm_ref[0] = jnp.max(nsq.reshape(nsq.shape[0] // 8, 8, LANES), axis=0)
    vt_ref[0] = lax.dot_general(wvt_ref[...], xn, NT_DIMS, preferred_element_type=F32).astype(BF16)


def _inproj(x, g_mix, w_qku_bf, w_vt_bf, tm):
    t = x.shape[0]
    row = lambda i: (i, 0)
    const = lambda i: (0, 0)
    col = np.arange(2 * ATTN_WIDTH)
    seg = np.zeros((2 * ATTN_WIDTH, LANES), np.float32)
    seg[col, col // DIFF_HEAD_DIM] = 1.0
    seg = jnp.asarray(seg, BF16)
    return pl.pallas_call(
        _inproj_kernel,
        grid=(t // tm,),
        in_specs=[pl.BlockSpec((tm, D_MODEL), row),
                  pl.BlockSpec((1, D_MODEL), const),
                  pl.BlockSpec(w_qku_bf.shape, const),
                  pl.BlockSpec(w_vt_bf.shape, const),
                  pl.BlockSpec(seg.shape, const)],
        out_specs=[pl.BlockSpec((tm, ATTN_WIDTH), row), pl.BlockSpec((tm, ATTN_WIDTH), row),
                   pl.BlockSpec((1, ATTN_WIDTH, tm), lambda i: (i, 0, 0)),
                   pl.BlockSpec((tm, SSM_WIDTH), row),
                   pl.BlockSpec((1, 8, LANES), lambda i: (i, 0, 0))],
        out_shape=[jax.ShapeDtypeStruct((t, ATTN_WIDTH), BF16), jax.ShapeDtypeStruct((t, ATTN_WIDTH), BF16),
                   jax.ShapeDtypeStruct((t // tm, ATTN_WIDTH, tm), BF16),
                   jax.ShapeDtypeStruct((t, SSM_WIDTH), F32),
                   jax.ShapeDtypeStruct((t // tm, 8, LANES), F32)],
        compiler_params=_cparams(("parallel",)),
        name="inproj",
    )(x, g_mix.reshape(1, D_MODEL), w_qku_bf, w_vt_bf, seg)


ONES_ROWS = 16
ACC_ROWS = HEAD_W + ONES_ROWS
POS_SPLIT = 16
N_BIAS_PARTS = 3
SKIP_BITS = 130.0
NORM_INFLATE = 1.0 + 2.0 ** -5


def _attn_kernel(sc_ref, ub_ref, q_ref, k_ref, vt_ref, qf_ref, kf_ref, g_ref, o_ref,
                 qa_ref, m_ref, acc_ref, *stage_refs, ts, out_scale):
    s_refs, p_refs, al_refs = stage_refs[0:2], stage_refs[2:6], stage_refs[6:10]
    bi = pl.program_id(0)
    h = pl.program_id(1)
    qi = pl.program_id(2)
    nsub = k_ref.shape[1] // ts
    bslope = sc_ref[1 + h]

    q = q_ref[0]
    lane = lax.broadcasted_iota(jnp.int32, q.shape, 1)
    zero = jnp.zeros_like(q)
    qf = qf_ref[0]
    qa_ref[0] = jnp.concatenate([jnp.where(lane < DIFF_HEAD_DIM, q, zero), qf], axis=1)
    qa_ref[1] = jnp.concatenate([jnp.where(lane >= DIFF_HEAD_DIM, q, zero), qf], axis=1)
    ones = jnp.ones((ONES_ROWS, ts), BF16)

    def keys(j, variant):
        rows = pl.ds(pl.multiple_of(j * ts, ts), ts)
        return jnp.concatenate([k_ref[0, rows, :], kf_ref[0, variant]], axis=1)

    def values(j):
        return jnp.concatenate([vt_ref[j], ones], axis=0)

    kpos = lax.broadcasted_iota(jnp.int32, (ts, ts), 0)
    qpos = lax.broadcasted_iota(jnp.int32, (ts, ts), 1)
    bias = jnp.abs(qpos - kpos).astype(F32) * (-bslope)
    ka = keys(qi, 2)
    vta = values(qi)
    s_diag = [lax.dot_general(ka, qa_ref[st], NT_DIMS, preferred_element_type=F32) for st in range(2)]
    for st in range(2):
        s = s_diag[st] + bias
        m_new = jnp.max(s, axis=0, keepdims=True)
        p = jnp.exp2(s - m_new).astype(BF16)
        acc_ref[st] = jnp.dot(vta, p, preferred_element_type=F32)
        m_ref[st] = m_new

    ub = ub_ref[(bi * N_HEADS + h) * nsub + qi]
    slack = ub - jnp.min(m_ref[...]) + SKIP_BITS
    x = (slack * sc_ref[1 + N_HEADS + h] - 1.0) * (1.0 / ts)
    w = jnp.where(x < 0.0, 0, jnp.minimum(x, float(nsub)).astype(jnp.int32) + 1)
    lo = jnp.maximum(qi - w, 0)
    n_off = jnp.minimum(qi + w, nsub - 1) - lo

    def sub_index(n):
        j = lo + jnp.clip(n, 0, jnp.maximum(n_off - 1, 0))
        return j + (j >= qi).astype(jnp.int32)

    def qk(n, s_ref):
        j = sub_index(n)
        ka = keys(j, jnp.where(j < qi, 0, 1))
        for st in range(2):
            s_ref[st] = lax.dot_general(ka, qa_ref[st], NT_DIMS, preferred_element_type=F32)

    def softmax(n, s_ref, p_ref, al_ref, keep=None):
        j = sub_index(n)
        c = -bslope * (jnp.abs(qi - j) * ts).astype(F32)
        for st in range(2):
            m_old = m_ref[st]
            m_new = jnp.maximum(m_old, jnp.max(s_ref[st], axis=0, keepdims=True) + c)
            al_ref[st] = jnp.exp2(m_old - m_new)
            p = jnp.exp2(s_ref[st] - (m_new - c))
            if keep is not None:
                p = p * keep
            p_ref[st] = p.astype(BF16)
            m_ref[st] = m_new

    def pv(n, p_ref, al_ref):
        vta = values(sub_index(n))
        for st in range(2):
            acc_ref[st] = al_ref[st] * acc_ref[st] + jnp.dot(vta, p_ref[st], preferred_element_type=F32)

    @pl.when(n_off > 0)
    def _():
        for p_ref, al_ref in zip(p_refs, al_refs):
            p_ref[...] = jnp.zeros_like(p_ref)
            al_ref[...] = jnp.ones_like(al_ref)
        qk(0, s_refs[0])

        def stage(n, r):
            qk(n + 1, s_refs[(r + 1) % 2])
            pv(n - 2, p_refs[(r + 2) % 4], al_refs[(r + 2) % 4])
            softmax(n, s_refs[r % 2], p_refs[r], al_refs[r])

        def quad(g, carry):
            for r in range(4):
                stage(4 * g + r, r)
            return carry

        n_quads = n_off // 4
        rem = n_off % 4
        lax.fori_loop(0, n_quads, quad, 0)
        for r in range(3):
            @pl.when(r < rem)
            def _(r=r):
                stage(4 * n_quads + r, r)

        for rv in range(4):
            @pl.when(rem == rv)
            def _(rv=rv):
                for n_rel in (rv - 2, rv - 1):
                    pv(4 * n_quads + n_rel, p_refs[n_rel % 4], al_refs[n_rel % 4])

    lam = sc_ref[0]
    a1 = acc_ref[0]
    a2 = acc_ref[1]
    out_t = (a1[:HEAD_W] / a1[HEAD_W:HEAD_W + 1] - lam * (a2[:HEAD_W] / a2[HEAD_W:HEAD_W + 1]))
    ms = jnp.mean(out_t * out_t, axis=0, keepdims=True)
    out_t = out_t * lax.rsqrt(ms + EPS) * (g_ref[...] * out_scale)
    o_ref[0] = out_t.T.astype(o_ref.dtype)


def _attn_features(bparts, ts):
    pos = jnp.arange(ts, dtype=jnp.int32)
    hi = ((pos // POS_SPLIT) * POS_SPLIT).astype(F32)[None, :, None]
    lo = (pos % POS_SPLIT).astype(F32)[None, :, None]
    bp = bparts[:, None, :]
    one = jnp.ones_like(bp)
    qf = jnp.stack([bp + 0 * hi, bp + 0 * hi, hi * one, lo * one], axis=-1)
    kf = jnp.stack([hi * one, lo * one, -bp + 0 * hi, -bp + 0 * hi], axis=-1)
    pad = lambda f: jnp.pad(f.reshape(N_HEADS, ts, 4 * N_BIAS_PARTS),
                            ((0, 0), (0, 0), (0, HEAD_W - 4 * N_BIAS_PARTS)))
    qf, kf = pad(qf), pad(kf)
    kf = jnp.stack([kf, -kf, jnp.zeros_like(kf)], axis=1)
    return qf.astype(BF16), kf.astype(BF16)


def _score_bounds(nrm, b, nsub):
    nrm = jnp.max(nrm, axis=1)[:, :2 * N_HEADS * 2].reshape(b, nsub, 2, N_HEADS, 2)
    q2 = nrm[:, :, 0]
    k2 = jnp.max(nrm[:, :, 1], axis=1)
    ub = jnp.max(jnp.sqrt(q2 * k2[:, None]), axis=-1) * NORM_INFLATE
    return jnp.transpose(ub, (0, 2, 1)).reshape(-1)


def _attention(q, k, vt, scal, ub, qf, kf, g_subln, lambda_init, ts):
    b, l, _ = q.shape
    nsub = l // ts
    kernel = functools.partial(_attn_kernel, ts=ts, out_scale=1.0 - lambda_init)
    return pl.pallas_call(
        kernel,
        grid=(b, N_HEADS, nsub),
        in_specs=[pl.BlockSpec(memory_space=pltpu.SMEM),
                  pl.BlockSpec(memory_space=pltpu.SMEM),
                  pl.BlockSpec((1, ts, HEAD_W), lambda bi, h, qi: (bi, qi, h)),
                  pl.BlockSpec((1, l, HEAD_W), lambda bi, h, qi: (bi, 0, h)),
                  pl.BlockSpec((nsub, HEAD_W, ts), lambda bi, h, qi: (bi, h, 0)),
                  pl.BlockSpec((1, ts, HEAD_W), lambda bi, h, qi: (h, 0, 0)),
                  pl.BlockSpec((1, 3, ts, HEAD_W), lambda bi, h, qi: (h, 0, 0, 0)),
                  pl.BlockSpec((HEAD_W, 1), lambda bi, h, qi: (0, 0))],
        out_specs=pl.BlockSpec((1, ts, HEAD_W), lambda bi, h, qi: (bi, qi, h)),
        out_shape=jax.ShapeDtypeStruct((b, l, ATTN_WIDTH), BF16),
        scratch_shapes=[pltpu.VMEM((2, ts, 2 * HEAD_W), BF16),
                        pltpu.VMEM((2, 1, ts), F32),
                        pltpu.VMEM((2, ACC_ROWS, ts), F32),
                        ] + [pltpu.VMEM((2, ts, ts), F32)] * 2
        + [pltpu.VMEM((2, ts, ts), BF16)] * 4
        + [pltpu.VMEM((2, 1, ts), F32)] * 4,
        compiler_params=_cparams(("parallel", "parallel", "arbitrary")),
        name="attn",
    )(scal, ub, q, k, vt, qf, kf, g_subln.reshape(HEAD_W, 1))


SCAN_LANES = 512
S5_BLOCKS = 2


def _s5_kernel(uf_ref, ub_ref, pm_ref, pmt_ref,
               bbf_ref, ccf_ref, af_ref, atf_ref, apf_ref,
               bbb_ref, ccb_ref, ab_ref, atb_ref, apb_ref,
               yf_ref, yb_ref, xsf_ref, xsb_ref, xbf_ref, xbb_ref, carf_ref, carb_ref, *, tc):
    tc8 = tc // 8
    cw = SSM_WIDTH // S5_BLOCKS
    sw = N_STATE // S5_BLOCKS

    @pl.when(pl.program_id(1) == 0)
    def _():
        carf_ref[...] = jnp.zeros_like(carf_ref)
        carb_ref[...] = jnp.zeros_like(carb_ref)

    def direction(u_ref, bb_ref, cc_ref, a_ref, at_ref, apow_ref, y_ref, xs_ref, xb_ref, carry_ref,
                  reverse):
        def bproj():
            up = jnp.dot(pm_ref[...], u_ref[0].astype(BF16), preferred_element_type=F32).astype(BF16)
            for hb in range(S5_BLOCKS):
                bu = jnp.dot(up[:, hb * cw:(hb + 1) * cw], bb_ref[hb], preferred_element_type=F32)
                xs_ref[:, hb * sw:(hb + 1) * sw] = bu[:, :sw]
                xs_ref[:, N_STATE + hb * sw:N_STATE + (hb + 1) * sw] = bu[:, sw:]

        def scan():
            for cb in range(N_STATE // SCAN_LANES):
                lo = cb * SCAN_LANES
                ar = jnp.broadcast_to(a_ref[0:1, lo:lo + SCAN_LANES], (8, SCAN_LANES))
                ai = jnp.broadcast_to(a_ref[1:2, lo:lo + SCAN_LANES], (8, SCAN_LANES))
                xr = xi = None
                for i in range(tc8):
                    r = (tc8 - 1 - i) if reverse else i
                    br = xs_ref[r * 8:r * 8 + 8, lo:lo + SCAN_LANES]
                    bi = xs_ref[r * 8:r * 8 + 8, N_STATE + lo:N_STATE + lo + SCAN_LANES]
                    if i == 0:
                        xr, xi = br, bi
                    else:
                        xr, xi = ar * xr - ai * xi + br, ar * xi + ai * xr + bi
                        xs_ref[r * 8:r * 8 + 8, lo:lo + SCAN_LANES] = xr
                        xs_ref[r * 8:r * 8 + 8, N_STATE + lo:N_STATE + lo + SCAN_LANES] = xi

        def fix():
            last = 0 if reverse else (tc8 - 1) * 8
            e_re = xs_ref[last:last + 8, 0:N_STATE]
            e_im = xs_ref[last:last + 8, N_STATE:2 * N_STATE]
            atr, ati = at_ref[0:1, :], at_ref[1:2, :]
            cr, ci = carry_ref[0:1, :], carry_ref[1:2, :]
            cin_r, cin_i = [None] * 8, [None] * 8
            for j in (range(7, -1, -1) if reverse else range(8)):
                cin_r[j], cin_i[j] = cr, ci
                cr, ci = (atr * cr - ati * ci + e_re[j:j + 1], atr * ci + ati * cr + e_im[j:j + 1])
            carry_ref[0:1, :] = cr
            carry_ref[1:2, :] = ci
            cin_r = jnp.concatenate(cin_r + cin_r, axis=0)
            cin_i = jnp.concatenate(cin_i + cin_i, axis=0)
            for i in range(tc // 16):
                rows = slice(i * 16, (i + 1) * 16)
                pr = apow_ref[rows, 0:N_STATE]
                pi = apow_ref[rows, N_STATE:2 * N_STATE]
                xr = xs_ref[rows, 0:N_STATE] + pr * cin_r - pi * cin_i
                xi = xs_ref[rows, N_STATE:2 * N_STATE] + pr * cin_i + pi * cin_r
                xb_ref[rows, 0:N_STATE] = xr.astype(BF16)
                xb_ref[rows, N_STATE:2 * N_STATE] = xi.astype(BF16)

        def cproj():
            yp = jnp.concatenate(
                [jnp.dot(xb_ref[:, hb * sw:(hb + 1) * sw], cc_ref[hb, 0], preferred_element_type=F32)
                 + jnp.dot(xb_ref[:, N_STATE + hb * sw:N_STATE + (hb + 1) * sw], cc_ref[hb, 1],
                           preferred_element_type=F32) for hb in range(S5_BLOCKS)], axis=1)
            hi = yp.astype(BF16)
            lo_part = (yp - hi.astype(F32)).astype(BF16)
            y_ref[0] = (jnp.dot(pmt_ref[...], hi, preferred_element_type=F32)
                        + jnp.dot(pmt_ref[...], lo_part, preferred_element_type=F32))

        return bproj, scan, fix, cproj

    fwd = direction(uf_ref, bbf_ref, ccf_ref, af_ref, atf_ref, apf_ref, yf_ref, xsf_ref, xbf_ref,
                    carf_ref, False)
    bwd = direction(ub_ref, bbb_ref, ccb_ref, ab_ref, atb_ref, apb_ref, yb_ref, xsb_ref, xbb_ref,
                    carb_ref, True)
    for stage_f, stage_b in zip(fwd, bwd):
        stage_f()
        stage_b()


def _s5(u, pm, pmt, par_f, par_b, tc):
    b, l, _ = u.shape
    nc = l // tc
    fmap = lambda bi, c: (bi, c, 0)
    bmap = lambda bi, c: (bi, nc - 1 - c, 0)
    full = lambda arr: pl.BlockSpec(arr.shape, lambda bi, c: (0,) * arr.ndim)
    kernel = functools.partial(_s5_kernel, tc=tc)
    state = [pltpu.VMEM((tc, 2 * N_STATE), F32)] * 2 + [pltpu.VMEM((tc, 2 * N_STATE), BF16)] * 2
    return pl.pallas_call(
        kernel,
        grid=(b, nc),
        in_specs=[pl.BlockSpec((1, tc, SSM_WIDTH), fmap), pl.BlockSpec((1, tc, SSM_WIDTH), bmap),
                  full(pm), full(pmt)] + [full(x) for x in par_f] + [full(x) for x in par_b],
        out_specs=[pl.BlockSpec((1, tc, SSM_WIDTH), fmap), pl.BlockSpec((1, tc, SSM_WIDTH), bmap)],
        out_shape=[jax.ShapeDtypeStruct((b, l, SSM_WIDTH), F32)] * 2,
        scratch_shapes=state + [pltpu.VMEM((2, N_STATE), F32)] * 2,
        compiler_params=_cparams(("parallel", "arbitrary")),
        name="s5",
    )(u, u, pm, pmt, *par_f, *par_b)


def _s5_params(lam_re, lam_im, log_dt, b_re, b_im, c_re, c_im, tc, reverse):
    tc8 = tc // 8
    dt = jnp.exp(log_dt)[:, None]
    mag = jnp.exp(lam_re * dt)
    ab_re = mag * jnp.cos(lam_im * dt)
    ab_im = mag * jnp.sin(lam_im * dt)
    den = lam_re * lam_re + lam_im * lam_im
    nr, ni = ab_re - 1.0, ab_im
    co_re = (nr * lam_re + ni * lam_im) / den
    co_im = (ni * lam_re - nr * lam_im) / den
    bb_re = co_re[..., None] * b_re - co_im[..., None] * b_im
    bb_im = co_re[..., None] * b_im + co_im[..., None] * b_re
    gb = N_GROUPS // S5_BLOCKS
    cw, sw = SSM_WIDTH // S5_BLOCKS, N_STATE // S5_BLOCKS
    eye = jnp.eye(gb, dtype=F32)
    blk_in = lambda w: jnp.einsum('bgph,gk->bghkp', w.reshape(S5_BLOCKS, gb, SSM_STATE, SSM_GROUP),
                                  eye).reshape(S5_BLOCKS, cw, sw)
    bb = jnp.concatenate([blk_in(bb_re), blk_in(bb_im)], axis=2).astype(BF16)
    blk_out = lambda w: jnp.einsum('bghp,gk->bgpkh', w.reshape(S5_BLOCKS, gb, SSM_GROUP, SSM_STATE),
                                   eye).reshape(S5_BLOCKS, sw, cw)
    cc = jnp.stack([blk_out(c_re), -blk_out(c_im)], axis=1).astype(BF16)
    a = jnp.stack([ab_re.reshape(-1), ab_im.reshape(-1)])
    n = jnp.arange(1, tc8 + 1, dtype=F32)[:, None, None]
    pmag = jnp.exp(n * (lam_re * dt)[None])
    ang = n * (lam_im * dt)[None]
    pw_re = (pmag * jnp.cos(ang)).reshape(tc8, N_STATE)
    pw_im = (pmag * jnp.sin(ang)).reshape(tc8, N_STATE)
    at = jnp.stack([pw_re[-1], pw_im[-1]])
    if reverse:
        pw_re, pw_im = pw_re[::-1], pw_im[::-1]
    apow = jnp.concatenate([jnp.repeat(pw_re, 8, axis=0), jnp.repeat(pw_im, 8, axis=0)], axis=1)
    return bb, cc, a, at, apow


def _perm_matrices(tc):
    tc8 = tc // 8
    i = np.arange(tc)
    src = (i % 8) * tc8 + i // 8
    pm = np.zeros((tc, tc), np.float32)
    pm[i, src] = 1.0
    return jnp.asarray(pm, BF16), jnp.asarray(pm.T, BF16)


def _mix_kernel(h_ref, a_ref, u_ref, yf_ref, yb_ref, d_ref, wglu_ref, bglu_ref, wout_ref,
                gffn_ref, wr_ref, br_ref, tri_ref, h1_ref, hn_ref, ti_ref, tg_ref, counts_ref, cnt_ref):
    h1 = h_ref[...] + jnp.dot(a_ref[...], wout_ref[0:ATTN_WIDTH, :], preferred_element_type=F32)
    y = d_ref[...] * u_ref[...] + yf_ref[...] + yb_ref[...]
    y = jax.nn.gelu(y)
    z = jnp.dot(y.astype(BF16), wglu_ref[...], preferred_element_type=F32) + bglu_ref[...]
    s = y * jax.nn.sigmoid(z)
    h1 = h1 + jnp.dot(s.astype(BF16), wout_ref[ATTN_WIDTH:, :], preferred_element_type=F32)
    h1_ref[...] = h1
    hn = _rms(h1, gffn_ref[...])
    hn_ref[...] = hn
    hn_hi = hn.astype(BF16)
    hn_lo = (hn - hn_hi.astype(F32)).astype(BF16)
    logits = (jnp.dot(hn_hi, wr_ref[0], preferred_element_type=F32)
              + jnp.dot(hn_lo, wr_ref[0], preferred_element_type=F32)
              + jnp.dot(hn_hi, wr_ref[1], preferred_element_type=F32)) + br_ref[...]
    lane = lax.broadcasted_iota(jnp.int32, logits.shape, 1)
    neg = jnp.float32(-jnp.inf)
    work = logits
    ti = jnp.zeros(logits.shape, jnp.int32)
    tv = jnp.full(logits.shape, neg, F32)
    picked = jnp.zeros(logits.shape, F32)
    idxs = []
    for kk in range(TOP_K):
        mx = jnp.max(work, axis=-1, keepdims=True)
        idx = jnp.min(jnp.where(work == mx, lane, LANES), axis=-1, keepdims=True)
        idxs.append(idx)
        ti = jnp.where(lane == kk, idx, ti)
        tv = jnp.where(lane == kk, mx, tv)
        picked = jnp.where(lane == idx, 1.0, picked)
        work = jnp.where(lane == idx, neg, work)
    ex = jnp.exp(tv - jnp.max(tv, axis=-1, keepdims=True))
    tg_ref[...] = ex / jnp.sum(ex, axis=-1, keepdims=True)

    @pl.when(pl.program_id(0) == 0)
    def _():
        cnt_ref[...] = jnp.zeros_like(cnt_ref)

    before = jnp.dot(tri_ref[...], picked.astype(BF16), preferred_element_type=F32) + cnt_ref[0:1, :]
    for kk in range(TOP_K):
        rank = jnp.sum(jnp.where(lane == idxs[kk], before, 0.0), axis=-1, keepdims=True)
        ti = jnp.where(lane == TOP_K + kk, rank.astype(jnp.int32), ti)
    ti_ref[...] = ti
    cnt_ref[...] = cnt_ref[...] + jnp.sum(picked, axis=0, keepdims=True)
    counts_ref[...] = cnt_ref[...]


def _mix(h, a, u, yf, yb, d, wglu_bf, bglu, wout_bf, g_ffn, wr_pad, br_pad, tm):
    t = h.shape[0]
    row = lambda i: (i, 0)
    const = lambda i: (0, 0)
    full = lambda arr: pl.BlockSpec(arr.shape, lambda i: (0,) * arr.ndim)
    tri = jnp.asarray(np.tril(np.ones((tm, tm), np.float32), -1), BF16)
    return pl.pallas_call(
        _mix_kernel,
        grid=(t // tm,),
        in_specs=[pl.BlockSpec((tm, D_MODEL), row), pl.BlockSpec((tm, ATTN_WIDTH), row),
                  pl.BlockSpec((tm, SSM_WIDTH), row), pl.BlockSpec((tm, SSM_WIDTH), row),
                  pl.BlockSpec((tm, SSM_WIDTH), row),
                  full(d), full(wglu_bf), full(bglu), full(wout_bf), full(g_ffn), full(wr_pad),
                  full(br_pad), full(tri)],
        out_specs=[pl.BlockSpec((tm, D_MODEL), row), pl.BlockSpec((tm, D_MODEL), row),
                   pl.BlockSpec((tm, LANES), row), pl.BlockSpec((tm, LANES), row),
                   pl.BlockSpec((8, LANES), const)],
        out_shape=[jax.ShapeDtypeStruct((t, D_MODEL), F32), jax.ShapeDtypeStruct((t, D_MODEL), F32),
                   jax.ShapeDtypeStruct((t, LANES), jnp.int32), jax.ShapeDtypeStruct((t, LANES), F32),
                   jax.ShapeDtypeStruct((8, LANES), F32)],
        scratch_shapes=[pltpu.VMEM((8, LANES), F32)],
        compiler_params=_cparams(("arbitrary",)),
        name="mix",
    )(h, a, u, yf, yb, d, wglu_bf, bglu, wout_bf, g_ffn, wr_pad, br_pad, tri)


ROWS_PER_ISSUE = 4

def _dispatch_kernel(dest_ref, pad_start_ref, pad_len_ref, hn_ref, xs_ref, zrow_ref, sem, zsem, *, tm):
    @pl.when(pl.program_id(0) == 0)
    def _():
        zrow_ref[...] = jnp.zeros_like(zrow_ref)

        def zero_copy(row):
            return pltpu.make_async_copy(zrow_ref.at[pl.ds(0, 1), :], xs_ref.at[pl.ds(row, 1), :], zsem)

        def start_span(e, _):
            base = pad_start_ref[e]
            lax.fori_loop(0, pad_len_ref[e], lambda r, c: (zero_copy(base + r).start(), c)[1], 0)
            return 0

        def wait_span(e, _):
            lax.fori_loop(0, pad_len_ref[e], lambda r, c: (zero_copy(0).wait(), c)[1], 0)
            return 0

        lax.fori_loop(0, N_EXPERTS + 1, start_span, 0)
        lax.fori_loop(0, N_EXPERTS + 1, wait_span, 0)

    def issue(g, _):
        for rr in range(ROWS_PER_ISSUE):
            r = g * ROWS_PER_ISSUE + rr
            for kk in range(TOP_K):
                pltpu.make_async_copy(hn_ref.at[pl.ds(r, 1), :],
                                      xs_ref.at[pl.ds(dest_ref[r * TOP_K + kk], 1), :], sem).start()
        return 0

    lax.fori_loop(0, tm // ROWS_PER_ISSUE, issue, 0)

    def drain(g, _):
        for _unused in range(ROWS_PER_ISSUE * TOP_K):
            pltpu.make_async_copy(hn_ref.at[pl.ds(0, 1), :], xs_ref.at[pl.ds(0, 1), :], sem).wait()
        return 0

    lax.fori_loop(0, tm // ROWS_PER_ISSUE, drain, 0)


def _dispatch(dest_flat, pad_start, pad_len, hn, n_pad, tm):
    t = hn.shape[0]
    kernel = functools.partial(_dispatch_kernel, tm=tm)
    return pl.pallas_call(
        kernel,
        grid=(t // tm,),
        in_specs=[pl.BlockSpec((tm * TOP_K,), lambda i: (i,), memory_space=pltpu.SMEM),
                  pl.BlockSpec(memory_space=pltpu.SMEM), pl.BlockSpec(memory_space=pltpu.SMEM),
                  pl.BlockSpec((tm, D_MODEL), lambda i: (i, 0))],
        out_specs=pl.BlockSpec(memory_space=pl.ANY),
        out_shape=jax.ShapeDtypeStruct((n_pad, D_MODEL), F32),
        scratch_shapes=[pltpu.VMEM((8, D_MODEL), F32), pltpu.SemaphoreType.DMA(()),
                        pltpu.SemaphoreType.DMA(())],
        compiler_params=_cparams(("arbitrary",)),
        name="dispatch",
    )(dest_flat, pad_start, pad_len, hn)


def _experts_kernel(be_ref, x_ref, wg_ref, bg_ref, wu_ref, bu_ref, wd_ref, bd_ref, y_ref,
                    wgb_ref, wub_ref, wdb_ref):
    i = pl.program_id(0)

    @pl.when((i == 0) | (be_ref[i] != be_ref[jnp.maximum(i - 1, 0)]))
    def _():
        wgb_ref[...] = wg_ref[0].astype(BF16)
        wub_ref[...] = wu_ref[0].astype(BF16)
        wdb_ref[...] = wd_ref[0].astype(BF16)

    x = x_ref[...].astype(BF16)
    gt = jnp.minimum(jnp.dot(x, wgb_ref[...], preferred_element_type=F32) + bg_ref[0], SWIGLU_LIMIT)
    up = jnp.clip(jnp.dot(x, wub_ref[...], preferred_element_type=F32) + bu_ref[0],
                  -SWIGLU_LIMIT, SWIGLU_LIMIT)
    hh = (up + 1.0) * (gt * jax.nn.sigmoid(SWIGLU_ALPHA * gt))
    y_ref[...] = jnp.dot(hh.astype(BF16), wdb_ref[...], preferred_element_type=F32) + bd_ref[0]


def _experts(blk_expert, xs, wg, bg, wu, bu, wd, bd, eb):
    n_pad = xs.shape[0]
    row = lambda i, be: (i, 0)
    wmap = lambda i, be: (be[i], 0, 0)
    wspec = pl.BlockSpec((1, D_MODEL, D_MODEL), wmap)
    bspec = pl.BlockSpec((1, 1, D_MODEL), wmap)
    grid_spec = pltpu.PrefetchScalarGridSpec(
        num_scalar_prefetch=1,
        grid=(n_pad // eb,),
        in_specs=[pl.BlockSpec((eb, D_MODEL), row), wspec, bspec, wspec, bspec, wspec, bspec],
        out_specs=pl.BlockSpec((eb, D_MODEL), row),
        scratch_shapes=[pltpu.VMEM((D_MODEL, D_MODEL), BF16)] * 3,
    )
    return pl.pallas_call(
        _experts_kernel,
        grid_spec=grid_spec,
        out_shape=jax.ShapeDtypeStruct((n_pad, D_MODEL), F32),
        compiler_params=_cparams(("arbitrary",)),
        name="experts",
    )(blk_expert, xs, wg, bg, wu, bu, wd, bd)


def _tail_kernel(dcur_ref, dnxt_ref, h1_ref, tg_ref, p_ref, ys_ref, gple_ref, wpg_ref, wpp_ref, gfin_ref,
                 o_ref, bufa_ref, bufb_ref, sema, semb, *, th):
    i = pl.program_id(0)

    def row_copy(idx, buf_ref, kk, r, sem):
        return pltpu.make_async_copy(ys_ref.at[pl.ds(idx, 1), :], buf_ref.at[kk, pl.ds(r, 1), :], sem)

    def gather(d_ref, base, buf_ref, sem):
        for r in range(th):
            for kk in range(TOP_K):
                row_copy(d_ref[base + r * TOP_K + kk], buf_ref, kk, r, sem).start()

    def wait_all(buf_ref, sem):
        for _unused in range(th * TOP_K):
            row_copy(0, buf_ref, 0, 0, sem).wait()

    def compute(rows, buf_ref):
        proj = jnp.dot(p_ref[rows, :].astype(BF16), wpp_ref[...], preferred_element_type=F32)
        tg = tg_ref[rows, :]
        h2 = h1_ref[rows, :]
        for kk in range(TOP_K):
            h2 = h2 + tg[:, kk:kk + 1] * buf_ref[kk]
        gate = jax.nn.sigmoid(jnp.dot(_rms(h2, gple_ref[...]).astype(BF16), wpg_ref[...],
                                      preferred_element_type=F32))
        o_ref[rows, :] = _rms(h2 + gate * proj, gfin_ref[...])

    @pl.when(i == 0)
    def _():
        gather(dcur_ref, 0, bufa_ref, sema)

    wait_all(bufa_ref, sema)
    gather(dcur_ref, th * TOP_K, bufb_ref, semb)
    compute(slice(0, th), bufa_ref)
    wait_all(bufb_ref, semb)
    gather(dnxt_ref, 0, bufa_ref, sema)
    compute(slice(th, 2 * th), bufb_ref)

    @pl.when(i == pl.num_programs(0) - 1)
    def _():
        wait_all(bufa_ref, sema)


def _tail(dest_flat, h1, tg, p, ys, g_ple, wpg_bf, wpp_bf, g_final, th):
    t = h1.shape[0]
    tm = 2 * th
    n = t // tm
    row = lambda i: (i, 0)
    full = lambda arr: pl.BlockSpec(arr.shape, lambda i: (0,) * arr.ndim)
    kernel = functools.partial(_tail_kernel, th=th)
    return pl.pallas_call(
        kernel,
        grid=(n,),
        in_specs=[pl.BlockSpec((tm * TOP_K,), lambda i: (i,), memory_space=pltpu.SMEM),
                  pl.BlockSpec((tm * TOP_K,), lambda i: (jnp.minimum(i + 1, n - 1),),
                               memory_space=pltpu.SMEM),
                  pl.BlockSpec((tm, D_MODEL), row), pl.BlockSpec((tm, LANES), row),
                  pl.BlockSpec((tm, PLE_DIM), row),
                  pl.BlockSpec(memory_space=pl.ANY),
                  full(g_ple), full(wpg_bf), full(wpp_bf), full(g_final)],
        out_specs=pl.BlockSpec((tm, D_MODEL), row),
        out_shape=jax.ShapeDtypeStruct((t, D_MODEL), F32),
        scratch_shapes=[pltpu.VMEM((TOP_K, th, D_MODEL), F32), pltpu.VMEM((TOP_K, th, D_MODEL), F32),
                        pltpu.SemaphoreType.DMA(()), pltpu.SemaphoreType.DMA(())],
        compiler_params=_cparams(("arbitrary",)),
        name="tail",
    )(dest_flat, dest_flat, h1, tg, p, ys, g_ple, wpg_bf, wpp_bf, g_final)


def _routing(ti, counts, eb):
    t = ti.shape[0]
    top_i, rank = ti[:, :TOP_K], ti[:, TOP_K:2 * TOP_K]
    counts = counts[0, :N_EXPERTS].astype(jnp.int32)
    padded = ((counts + eb - 1) // eb) * eb
    pends = jnp.cumsum(padded)
    pstarts = pends - padded
    onehot = top_i[:, :, None] == jnp.arange(N_EXPERTS, dtype=jnp.int32)
    dest = rank + jnp.sum(jnp.where(onehot, pstarts, 0), axis=-1)
    n_blocks = -(-(t * TOP_K + N_EXPERTS * (eb - 1)) // eb)
    blk_start = jnp.arange(n_blocks, dtype=jnp.int32) * eb
    blk_expert = jnp.minimum(jnp.sum(blk_start[:, None] >= pends[None, :], axis=1), N_EXPERTS - 1)
    n_pad = n_blocks * eb
    pad_start = jnp.concatenate([pstarts + counts, pends[-1:]]).astype(jnp.int32)
    pad_len = jnp.concatenate([padded - counts, n_pad - pends[-1:]]).astype(jnp.int32)
    return (dest.reshape(-1).astype(jnp.int32), blk_expert.astype(jnp.int32), pad_start, pad_len, n_pad)


def _pick(n, prefs):
    for c in prefs:
        if n % c == 0:
            return c
    raise ValueError(f"no tile for {n}")


def _trunk(x, p, w, lambda_init):
    b, l, _ = x.shape
    t = b * l
    tm = TOKEN_TILE
    assert l % tm == 0
    tc = _pick(l, (256, 128))
    eb = 256

    q, k, vt, u, nrm = _inproj(x.reshape(t, D_MODEL), w['g_mix'], w['w_qku'], w['w_vt'], tm)
    a = _attention(q.reshape(b, l, -1), k.reshape(b, l, -1), vt, w['attn_scal'],
                   _score_bounds(nrm, b, l // tm), w['attn_qf'], w['attn_kf'], w['g_subln'],
                   lambda_init, tm)
    u3 = u.reshape(b, l, SSM_WIDTH)
    pm, pmt = _perm_matrices(tc)
    ys = _s5(u3, pm, pmt, w['s5'][0], w['s5'][1], tc)
    h1, hn, ti, tg, counts = _mix(x.reshape(t, D_MODEL), a.reshape(t, ATTN_WIDTH), u,
                                  ys[0].reshape(t, SSM_WIDTH), ys[1].reshape(t, SSM_WIDTH),
                                  w['ssm_d'], w['w_glu'], w['b_glu'], w['w_out'], w['g_ffn'],
                                  w['w_router'], w['b_router'], tm)
    dest, blk_expert, pad_start, pad_len, n_pad = _routing(ti, counts, eb)
    xs = _dispatch(dest, pad_start, pad_len, hn, n_pad, tm // 2)
    ye = _experts(blk_expert, xs, w['w_gate'], w['b_gate'], w['w_up'], w['b_up'],
                  w['w_down'], w['b_down'], eb)
    out = _tail(dest, h1, tg, p.reshape(t, PLE_DIM), ye, w['g_ple'], w['w_ple_gate'],
                w['w_ple_proj'], w['g_final'], tm // 2)
    return out.reshape(b, l, D_MODEL)


def _hi_lo(x):
    hi = x.astype(BF16)
    return jnp.stack([hi, (x - hi.astype(F32)).astype(BF16)])


def _prepare(i, g_mix, w_in, lambda_q1, lambda_k1, lambda_q2, lambda_k2, g_subln, ssm_lambda_re,
             ssm_lambda_im, ssm_log_dt, ssm_b_re, ssm_b_im, ssm_c_re, ssm_c_im, ssm_d, w_glu, b_glu,
             w_out, g_ffn, w_router, b_router, w_gate, b_gate, w_up, b_up, w_down, b_down, g_ple,
             w_ple_gate, w_ple_proj, g_final, tcs):
    lambda_init = 0.8 - 0.6 * math.exp(-0.3 * i)
    lam = (jnp.exp(jnp.sum(lambda_q1[i] * lambda_k1[i]))
           - jnp.exp(jnp.sum(lambda_q2[i] * lambda_k2[i])) + lambda_init)
    slopes = jnp.exp2(-8.0 * jnp.arange(1, N_HEADS + 1, dtype=F32) / N_HEADS)
    bslopes = slopes * LOG2E
    parts, rest = [], bslopes
    for _ in range(N_BIAS_PARTS):
        part = rest.astype(BF16).astype(F32)
        parts.append(part)
        rest = rest - part
    attn_qf, attn_kf = _attn_features(jnp.stack(parts, axis=1), TOKEN_TILE)
    w_qku = jnp.concatenate([w_in[i][:, :2 * ATTN_WIDTH], w_in[i][:, 3 * ATTN_WIDTH:]], axis=1)
    row = lambda vec: vec.reshape(1, -1).astype(F32)
    w = {
        'g_mix': g_mix[i], 'w_qku': w_qku.astype(BF16),
        'w_vt': w_in[i][:, 2 * ATTN_WIDTH:3 * ATTN_WIDTH].T.astype(BF16),
        'attn_scal': jnp.concatenate([lam.reshape(1), bslopes, 1.0 / bslopes]).astype(F32),
        'attn_qf': attn_qf, 'attn_kf': attn_kf,
        'g_subln': g_subln[i],
        'ssm_d': row(ssm_d[i]), 'w_glu': w_glu[i].astype(BF16), 'b_glu': row(b_glu[i]),
        'w_out': w_out[i].astype(BF16), 'g_ffn': row(g_ffn[i]),
        'w_router': _hi_lo(jnp.pad(w_router[i], ((0, 0), (0, LANES - N_EXPERTS)))),
        'b_router': jnp.pad(row(b_router[i]), ((0, 0), (0, LANES - N_EXPERTS)),
                            constant_values=-jnp.inf),
        'w_gate': w_gate[i], 'b_gate': b_gate[i].reshape(N_EXPERTS, 1, D_MODEL),
        'w_up': w_up[i], 'b_up': b_up[i].reshape(N_EXPERTS, 1, D_MODEL),
        'w_down': w_down[i], 'b_down': b_down[i].reshape(N_EXPERTS, 1, D_MODEL),
        'g_ple': row(g_ple[i]), 'w_ple_gate': w_ple_gate[i].astype(BF16),
        'w_ple_proj': w_ple_proj[i].astype(BF16), 'g_final': row(g_final),
    }
    w['s5'] = {tc: [_s5_params(ssm_lambda_re[i, dr], ssm_lambda_im[i, dr], ssm_log_dt[i, dr],
                               ssm_b_re[i, dr], ssm_b_im[i, dr], ssm_c_re[i, dr], ssm_c_im[i, dr],
                               tc, reverse=(dr == 1)) for dr in range(2)] for tc in tcs}
    return w, lambda_init


def kernel(x_prompt, x_sample, p_prompt, p_sample, g_mix, w_in, lambda_q1, lambda_k1, lambda_q2, lambda_k2, g_subln, ssm_lambda_re, ssm_lambda_im, ssm_log_dt, ssm_b_re, ssm_b_im, ssm_c_re, ssm_c_im, ssm_d, w_glu, b_glu, w_out, g_ffn, w_router, b_router, w_gate, b_gate, w_up, b_up, w_down, b_down, g_ple, w_ple_gate, w_ple_proj, g_final):
    assert w_in.shape[0] == 1, "single-layer trunk"
    tcs = {_pick(x.shape[1], (256, 128)) for x in (x_prompt, x_sample)}
    w, lambda_init = _prepare(0, g_mix, w_in, lambda_q1, lambda_k1, lambda_q2, lambda_k2, g_subln,
                              ssm_lambda_re, ssm_lambda_im, ssm_log_dt, ssm_b_re, ssm_b_im, ssm_c_re,
                              ssm_c_im, ssm_d, w_glu, b_glu, w_out, g_ffn, w_router, b_router, w_gate,
                              b_gate, w_up, b_up, w_down, b_down, g_ple, w_ple_gate, w_ple_proj,
                              g_final, tcs)
    outs = []
    for x, p in ((x_prompt, p_prompt), (x_sample, p_sample)):
        tc = _pick(x.shape[1], (256, 128))
        wt = dict(w, s5=w['s5'][tc])
        outs.append(_trunk(x, p[0], wt, lambda_init))
    return tuple(outs)
```

```python
import functools
import math

import jax
import jax.numpy as jnp
import numpy as np
from jax import lax
from jax.experimental import pallas as pl
from jax.experimental.pallas import tpu as pltpu

F32 = jnp.float32
BF16 = jnp.bfloat16

D_MODEL = 1024
PLE_DIM = 256
ATTN_WIDTH = 512
SSM_WIDTH = 512
HEAD_W = 128
DIFF_HEAD_DIM = 64
N_HEADS = 4
SSM_GROUP = 16
N_GROUPS = 32
SSM_STATE = 64
N_STATE = N_GROUPS * SSM_STATE
N_EXPERTS = 32
TOP_K = 4
SWIGLU_LIMIT = 7.0
SWIGLU_ALPHA = 1.702
EPS = 1e-6
LANES = 128
TOKEN_TILE = 512

VMEM_LIMIT = 56 * 1024 * 1024


def _cparams(sem):
    return pltpu.CompilerParams(dimension_semantics=sem, vmem_limit_bytes=VMEM_LIMIT)


def _rms(x, g):
    return x * lax.rsqrt(jnp.mean(x * x, axis=-1, keepdims=True) + EPS) * g


LOG2E = 1.4426950408889634
NT_DIMS = (((1,), (1,)), ((), ()))


def _inproj_kernel(x_ref, g_ref, wqku_ref, wvt_ref, seg_ref, q_ref, k_ref, vt_ref, u_ref, nrm_ref):
    xn = _rms(x_ref[...], g_ref[...]).astype(BF16)
    proj = jnp.dot(xn, wqku_ref[...], preferred_element_type=F32)
    scale = LOG2E / math.sqrt(DIFF_HEAD_DIM)
    qk = jnp.concatenate([proj[:, :ATTN_WIDTH] * scale, proj[:, ATTN_WIDTH:2 * ATTN_WIDTH]], axis=1)
    q_ref[...] = qk[:, :ATTN_WIDTH].astype(BF16)
    k_ref[...] = qk[:, ATTN_WIDTH:].astype(BF16)
    u_ref[...] = proj[:, 2 * ATTN_WIDTH:]
    nsq = jnp.dot((qk * qk).astype(BF16), seg_ref[...], preferred_element_type=F32)
    nrm_ref[0] = jnp.max(nsq.reshape(nsq.shape[0] // 8, 8, LANES), axis=0)
    vt_ref[0] = lax.dot_general(wvt_ref[...], xn, NT_DIMS, preferred_element_type=F32).astype(BF16)


def _inproj(x, g_mix, w_qku_bf, w_vt_bf, tm):
    t = x.shape[0]
    row = lambda i: (i, 0)
    const = lambda i: (0, 0)
    col = np.arange(2 * ATTN_WIDTH)
    seg = np.zeros((2 * ATTN_WIDTH, LANES), np.float32)
    seg[col, col // DIFF_HEAD_DIM] = 1.0
    seg = jnp.asarray(seg, BF16)
    return pl.pallas_call(
        _inproj_kernel,
        grid=(t // tm,),
        in_specs=[pl.BlockSpec((tm, D_MODEL), row),
                  pl.BlockSpec((1, D_MODEL), const),
                  pl.BlockSpec(w_qku_bf.shape, const),
                  pl.BlockSpec(w_vt_bf.shape, const),
                  pl.BlockSpec(seg.shape, const)],
        out_specs=[pl.BlockSpec((tm, ATTN_WIDTH), row), pl.BlockSpec((tm, ATTN_WIDTH), row),
                   pl.BlockSpec((1, ATTN_WIDTH, tm), lambda i: (i, 0, 0)),
                   pl.BlockSpec((tm, SSM_WIDTH), row),
                   pl.BlockSpec((1, 8, LANES), lambda i: (i, 0, 0))],
        out_shape=[jax.ShapeDtypeStruct((t, ATTN_WIDTH), BF16), jax.ShapeDtypeStruct((t, ATTN_WIDTH), BF16),
                   jax.ShapeDtypeStruct((t // tm, ATTN_WIDTH, tm), BF16),
                   jax.ShapeDtypeStruct((t, SSM_WIDTH), F32),
                   jax.ShapeDtypeStruct((t // tm, 8, LANES), F32)],
        compiler_params=_cparams(("parallel",)),
        name="inproj",
    )(x, g_mix.reshape(1, D_MODEL), w_qku_bf, w_vt_bf, seg)


ONES_ROWS = 16
ACC_ROWS = HEAD_W + ONES_ROWS
POS_SPLIT = 16
N_BIAS_PARTS = 3
SKIP_BITS = 130.0
NORM_INFLATE = 1.0 + 2.0 ** -5


def _attn_kernel(sc_ref, ub_ref, q_ref, k_ref, vt_ref, qf_ref, kf_ref, g_ref, o_ref,
                 qa_ref, m_ref, acc_ref, *stage_refs, ts, out_scale):
    s_refs, p_refs, al_refs = stage_refs[0:2], stage_refs[2:6], stage_refs[6:10]
    bi = pl.program_id(0)
    h = pl.program_id(1)
    qi = pl.program_id(2)
    nsub = k_ref.shape[1] // ts
    bslope = sc_ref[1 + h]

    q = q_ref[0]
    lane = lax.broadcasted_iota(jnp.int32, q.shape, 1)
    zero = jnp.zeros_like(q)
    qf = qf_ref[0]
    qa_ref[0] = jnp.concatenate([jnp.where(lane < DIFF_HEAD_DIM, q, zero), qf], axis=1)
    qa_ref[1] = jnp.concatenate([jnp.where(lane >= DIFF_HEAD_DIM, q, zero), qf], axis=1)
    ones = jnp.ones((ONES_ROWS, ts), BF16)

    def keys(j, variant):
        rows = pl.ds(pl.multiple_of(j * ts, ts), ts)
        return jnp.concatenate([k_ref[0, rows, :], kf_ref[0, variant]], axis=1)

    def values(j):
        return jnp.concatenate([vt_ref[j], ones], axis=0)

    kpos = lax.broadcasted_iota(jnp.int32, (ts, ts), 0)
    qpos = lax.broadcasted_iota(jnp.int32, (ts, ts), 1)
    bias = jnp.abs(qpos - kpos).astype(F32) * (-bslope)
    ka = keys(qi, 2)
    vta = values(qi)
    s_diag = [lax.dot_general(ka, qa_ref[st], NT_DIMS, preferred_element_type=F32) for st in range(2)]
    for st in range(2):
        s = s_diag[st] + bias
        m_new = jnp.max(s, axis=0, keepdims=True)
        p = jnp.exp2(s - m_new).astype(BF16)
        acc_ref[st] = jnp.dot(vta, p, preferred_element_type=F32)
        m_ref[st] = m_new

    ub = ub_ref[(bi * N_HEADS + h) * nsub + qi]
    slack = ub - jnp.min(m_ref[...]) + SKIP_BITS
    x = (slack * sc_ref[1 + N_HEADS + h] - 1.0) * (1.0 / ts)
    w = jnp.where(x < 0.0, 0, jnp.minimum(x, float(nsub)).astype(jnp.int32) + 1)
    lo = jnp.maximum(qi - w, 0)
    n_off = jnp.minimum(qi + w, nsub - 1) - lo

    def sub_index(n):
        j = lo + jnp.clip(n, 0, jnp.maximum(n_off - 1, 0))
        return j + (j >= qi).astype(jnp.int32)

    def qk(n, s_ref):
        j = sub_index(n)
        ka = keys(j, jnp.where(j < qi, 0, 1))
        for st in range(2):
            s_ref[st] = lax.dot_general(ka, qa_ref[st], NT_DIMS, preferred_element_type=F32)

    def softmax(n, s_ref, p_ref, al_ref):
        j = sub_index(n)
        c = -bslope * (jnp.abs(qi - j) * ts).astype(F32)
        for st in range(2):
            m_old = m_ref[st]
            m_new = jnp.maximum(m_old, jnp.max(s_ref[st], axis=0, keepdims=True) + c)
            al_ref[st] = jnp.exp2(m_old - m_new)
            p_ref[st] = jnp.exp2(s_ref[st] - (m_new - c)).astype(BF16)
            m_ref[st] = m_new

    def pv(n, p_ref, al_ref):
        vta = values(sub_index(n))
        for st in range(2):
            acc_ref[st] = al_ref[st] * acc_ref[st] + jnp.dot(vta, p_ref[st], preferred_element_type=F32)

    @pl.when(n_off > 0)
    def _():
        for p_ref, al_ref in zip(p_refs[2:], al_refs[2:]):
            p_ref[...] = jnp.zeros_like(p_ref)
            al_ref[...] = jnp.ones_like(al_ref)
        qk(0, s_refs[0])

        def stage(n, r):
            qk(n + 1, s_refs[(r + 1) % 2])
            pv(n - 2, p_refs[(r + 2) % 4], al_refs[(r + 2) % 4])
            softmax(n, s_refs[r % 2], p_refs[r], al_refs[r])

        def quad(g, carry):
            for r in range(4):
                stage(4 * g + r, r)
            return carry

        n_quads = n_off // 4
        rem = n_off % 4
        lax.fori_loop(0, n_quads, quad, 0)
        for rv in range(4):
            @pl.when(rem == rv)
            def _(rv=rv):
                for r in range(rv):
                    stage(4 * n_quads + r, r)
                for n_rel in (rv - 2, rv - 1):
                    pv(4 * n_quads + n_rel, p_refs[n_rel % 4], al_refs[n_rel % 4])

    lam = sc_ref[0]
    a1 = acc_ref[0]
    a2 = acc_ref[1]
    out_t = (a1[:HEAD_W] / a1[HEAD_W:HEAD_W + 1] - lam * (a2[:HEAD_W] / a2[HEAD_W:HEAD_W + 1]))
    ms = jnp.mean(out_t * out_t, axis=0, keepdims=True)
    out_t = out_t * lax.rsqrt(ms + EPS) * (g_ref[...] * out_scale)
    o_ref[0] = out_t.T.astype(o_ref.dtype)


def _attn_features(bparts, ts):
    pos = jnp.arange(ts, dtype=jnp.int32)
    hi = ((pos // POS_SPLIT) * POS_SPLIT).astype(F32)[None, :, None]
    lo = (pos % POS_SPLIT).astype(F32)[None, :, None]
    bp = bparts[:, None, :]
    one = jnp.ones_like(bp)
    qf = jnp.stack([bp + 0 * hi, bp + 0 * hi, hi * one, lo * one], axis=-1)
    kf = jnp.stack([hi * one, lo * one, -bp + 0 * hi, -bp + 0 * hi], axis=-1)
    pad = lambda f: jnp.pad(f.reshape(N_HEADS, ts, 4 * N_BIAS_PARTS),
                            ((0, 0), (0, 0), (0, HEAD_W - 4 * N_BIAS_PARTS)))
    qf, kf = pad(qf), pad(kf)
    kf = jnp.stack([kf, -kf, jnp.zeros_like(kf)], axis=1)
    return qf.astype(BF16), kf.astype(BF16)


def _score_bounds(nrm, b, nsub):
    nrm = jnp.max(nrm, axis=1)[:, :2 * N_HEADS * 2].reshape(b, nsub, 2, N_HEADS, 2)
    q2 = nrm[:, :, 0]
    k2 = jnp.max(nrm[:, :, 1], axis=1)
    ub = jnp.max(jnp.sqrt(q2 * k2[:, None]), axis=-1) * NORM_INFLATE
    return jnp.transpose(ub, (0, 2, 1)).reshape(-1)


def _attention(q, k, vt, scal, ub, qf, kf, g_subln, lambda_init, ts):
    b, l, _ = q.shape
    nsub = l // ts
    kernel = functools.partial(_attn_kernel, ts=ts, out_scale=1.0 - lambda_init)
    return pl.pallas_call(
        kernel,
        grid=(b, N_HEADS, nsub),
        in_specs=[pl.BlockSpec(memory_space=pltpu.SMEM),
                  pl.BlockSpec(memory_space=pltpu.SMEM),
                  pl.BlockSpec((1, ts, HEAD_W), lambda bi, h, qi: (bi, qi, h)),
                  pl.BlockSpec((1, l, HEAD_W), lambda bi, h, qi: (bi, 0, h)),
                  pl.BlockSpec((nsub, HEAD_W, ts), lambda bi, h, qi: (bi, h, 0)),
                  pl.BlockSpec((1, ts, HEAD_W), lambda bi, h, qi: (h, 0, 0)),
                  pl.BlockSpec((1, 3, ts, HEAD_W), lambda bi, h, qi: (h, 0, 0, 0)),
                  pl.BlockSpec((HEAD_W, 1), lambda bi, h, qi: (0, 0))],
        out_specs=pl.BlockSpec((1, ts, HEAD_W), lambda bi, h, qi: (bi, qi, h)),
        out_shape=jax.ShapeDtypeStruct((b, l, ATTN_WIDTH), BF16),
        scratch_shapes=[pltpu.VMEM((2, ts, 2 * HEAD_W), BF16),
                        pltpu.VMEM((2, 1, ts), F32),
                        pltpu.VMEM((2, ACC_ROWS, ts), F32),
                        ] + [pltpu.VMEM((2, ts, ts), F32)] * 2
        + [pltpu.VMEM((2, ts, ts), BF16)] * 4
        + [pltpu.VMEM((2, 1, ts), F32)] * 4,
        compiler_params=_cparams(("parallel", "parallel", "arbitrary")),
        name="attn",
    )(scal, ub, q, k, vt, qf, kf, g_subln.reshape(HEAD_W, 1))


SCAN_LANES = 512
S5_BLOCKS = 2


def _s5_kernel(uf_ref, ub_ref, pm_ref, pmt_ref,
               bbf_ref, ccf_ref, af_ref, atf_ref, apf_ref,
               bbb_ref, ccb_ref, ab_ref, atb_ref, apb_ref,
               yf_ref, yb_ref, xsf_ref, xsb_ref, xbf_ref, xbb_ref, carf_ref, carb_ref, *, tc):
    tc8 = tc // 8
    cw = SSM_WIDTH // S5_BLOCKS
    sw = N_STATE // S5_BLOCKS

    @pl.when(pl.program_id(1) == 0)
    def _():
        carf_ref[...] = jnp.zeros_like(carf_ref)
        carb_ref[...] = jnp.zeros_like(carb_ref)

    def direction(u_ref, bb_ref, cc_ref, a_ref, at_ref, apow_ref, y_ref, xs_ref, xb_ref, carry_ref,
                  reverse):
        def bproj():
            up = jnp.dot(pm_ref[...], u_ref[0].astype(BF16), preferred_element_type=F32).astype(BF16)
            for hb in range(S5_BLOCKS):
                bu = jnp.dot(up[:, hb * cw:(hb + 1) * cw], bb_ref[hb], preferred_element_type=F32)
                xs_ref[:, hb * sw:(hb + 1) * sw] = bu[:, :sw]
                xs_ref[:, N_STATE + hb * sw:N_STATE + (hb + 1) * sw] = bu[:, sw:]

        def scan():
            for cb in range(N_STATE // SCAN_LANES):
                lo = cb * SCAN_LANES
                ar = jnp.broadcast_to(a_ref[0:1, lo:lo + SCAN_LANES], (8, SCAN_LANES))
                ai = jnp.broadcast_to(a_ref[1:2, lo:lo + SCAN_LANES], (8, SCAN_LANES))
                xr = xi = None
                for i in range(tc8):
                    r = (tc8 - 1 - i) if reverse else i
                    br = xs_ref[r * 8:r * 8 + 8, lo:lo + SCAN_LANES]
                    bi = xs_ref[r * 8:r * 8 + 8, N_STATE + lo:N_STATE + lo + SCAN_LANES]
                    if i == 0:
                        xr, xi = br, bi
                    else:
                        xr, xi = ar * xr - ai * xi + br, ar * xi + ai * xr + bi
                        xs_ref[r * 8:r * 8 + 8, lo:lo + SCAN_LANES] = xr
                        xs_ref[r * 8:r * 8 + 8, N_STATE + lo:N_STATE + lo + SCAN_LANES] = xi

        def fix():
            last = 0 if reverse else (tc8 - 1) * 8
            e_re = xs_ref[last:last + 8, 0:N_STATE]
            e_im = xs_ref[last:last + 8, N_STATE:2 * N_STATE]
            atr, ati = at_ref[0:1, :], at_ref[1:2, :]
            cr, ci = carry_ref[0:1, :], carry_ref[1:2, :]
            cin_r, cin_i = [None] * 8, [None] * 8
            for j in (range(7, -1, -1) if reverse else range(8)):
                cin_r[j], cin_i[j] = cr, ci
                cr, ci = (atr * cr - ati * ci + e_re[j:j + 1], atr * ci + ati * cr + e_im[j:j + 1])
            carry_ref[0:1, :] = cr
            carry_ref[1:2, :] = ci
            cin_r = jnp.concatenate(cin_r + cin_r, axis=0)
            cin_i = jnp.concatenate(cin_i + cin_i, axis=0)
            for i in range(tc // 16):
                rows = slice(i * 16, (i + 1) * 16)
                pr = apow_ref[rows, 0:N_STATE]
                pi = apow_ref[rows, N_STATE:2 * N_STATE]
                xr = xs_ref[rows, 0:N_STATE] + pr * cin_r - pi * cin_i
                xi = xs_ref[rows, N_STATE:2 * N_STATE] + pr * cin_i + pi * cin_r
                xb_ref[rows, 0:N_STATE] = xr.astype(BF16)
                xb_ref[rows, N_STATE:2 * N_STATE] = xi.astype(BF16)

        def cproj():
            yp = jnp.concatenate(
                [jnp.dot(xb_ref[:, hb * sw:(hb + 1) * sw], cc_ref[hb, 0], preferred_element_type=F32)
                 + jnp.dot(xb_ref[:, N_STATE + hb * sw:N_STATE + (hb + 1) * sw], cc_ref[hb, 1],
                           preferred_element_type=F32) for hb in range(S5_BLOCKS)], axis=1)
            hi = yp.astype(BF16)
            lo_part = (yp - hi.astype(F32)).astype(BF16)
            y_ref[0] = (jnp.dot(pmt_ref[...], hi, preferred_element_type=F32)
                        + jnp.dot(pmt_ref[...], lo_part, preferred_element_type=F32))

        return bproj, scan, fix, cproj

    fwd = direction(uf_ref, bbf_ref, ccf_ref, af_ref, atf_ref, apf_ref, yf_ref, xsf_ref, xbf_ref,
                    carf_ref, False)
    bwd = direction(ub_ref, bbb_ref, ccb_ref, ab_ref, atb_ref, apb_ref, yb_ref, xsb_ref, xbb_ref,
                    carb_ref, True)
    for stage_f, stage_b in zip(fwd, bwd):
        stage_f()
        stage_b()


def _s5(u, pm, pmt, par_f, par_b, tc):
    b, l, _ = u.shape
    nc = l // tc
    fmap = lambda bi, c: (bi, c, 0)
    bmap = lambda bi, c: (bi, nc - 1 - c, 0)
    full = lambda arr: pl.BlockSpec(arr.shape, lambda bi, c: (0,) * arr.ndim)
    kernel = functools.partial(_s5_kernel, tc=tc)
    state = [pltpu.VMEM((tc, 2 * N_STATE), F32)] * 2 + [pltpu.VMEM((tc, 2 * N_STATE), BF16)] * 2
    return pl.pallas_call(
        kernel,
        grid=(b, nc),
        in_specs=[pl.BlockSpec((1, tc, SSM_WIDTH), fmap), pl.BlockSpec((1, tc, SSM_WIDTH), bmap),
                  full(pm), full(pmt)] + [full(x) for x in par_f] + [full(x) for x in par_b],
        out_specs=[pl.BlockSpec((1, tc, SSM_WIDTH), fmap), pl.BlockSpec((1, tc, SSM_WIDTH), bmap)],
        out_shape=[jax.ShapeDtypeStruct((b, l, SSM_WIDTH), F32)] * 2,
        scratch_shapes=state + [pltpu.VMEM((2, N_STATE), F32)] * 2,
        compiler_params=_cparams(("parallel", "arbitrary")),
        name="s5",
    )(u, u, pm, pmt, *par_f, *par_b)


def _s5_params(lam_re, lam_im, log_dt, b_re, b_im, c_re, c_im, tc, reverse):
    tc8 = tc // 8
    dt = jnp.exp(log_dt)[:, None]
    mag = jnp.exp(lam_re * dt)
    ab_re = mag * jnp.cos(lam_im * dt)
    ab_im = mag * jnp.sin(lam_im * dt)
    den = lam_re * lam_re + lam_im * lam_im
    nr, ni = ab_re - 1.0, ab_im
    co_re = (nr * lam_re + ni * lam_im) / den
    co_im = (ni * lam_re - nr * lam_im) / den
    bb_re = co_re[..., None] * b_re - co_im[..., None] * b_im
    bb_im = co_re[..., None] * b_im + co_im[..., None] * b_re
    gb = N_GROUPS // S5_BLOCKS
    cw, sw = SSM_WIDTH // S5_BLOCKS, N_STATE // S5_BLOCKS
    eye = jnp.eye(gb, dtype=F32)
    blk_in = lambda w: jnp.einsum('bgph,gk->bghkp', w.reshape(S5_BLOCKS, gb, SSM_STATE, SSM_GROUP),
                                  eye).reshape(S5_BLOCKS, cw, sw)
    bb = jnp.concatenate([blk_in(bb_re), blk_in(bb_im)], axis=2).astype(BF16)
    blk_out = lambda w: jnp.einsum('bghp,gk->bgpkh', w.reshape(S5_BLOCKS, gb, SSM_GROUP, SSM_STATE),
                                   eye).reshape(S5_BLOCKS, sw, cw)
    cc = jnp.stack([blk_out(c_re), -blk_out(c_im)], axis=1).astype(BF16)
    a = jnp.stack([ab_re.reshape(-1), ab_im.reshape(-1)])
    n = jnp.arange(1, tc8 + 1, dtype=F32)[:, None, None]
    pmag = jnp.exp(n * (lam_re * dt)[None])
    ang = n * (lam_im * dt)[None]
    pw_re = (pmag * jnp.cos(ang)).reshape(tc8, N_STATE)
    pw_im = (pmag * jnp.sin(ang)).reshape(tc8, N_STATE)
    at = jnp.stack([pw_re[-1], pw_im[-1]])
    if reverse:
        pw_re, pw_im = pw_re[::-1], pw_im[::-1]
    apow = jnp.concatenate([jnp.repeat(pw_re, 8, axis=0), jnp.repeat(pw_im, 8, axis=0)], axis=1)
    return bb, cc, a, at, apow


def _perm_matrices(tc):
    tc8 = tc // 8
    i = np.arange(tc)
    src = (i % 8) * tc8 + i // 8
    pm = np.zeros((tc, tc), np.float32)
    pm[i, src] = 1.0
    return jnp.asarray(pm, BF16), jnp.asarray(pm.T, BF16)


def _mix_kernel(h_ref, a_ref, u_ref, yf_ref, yb_ref, d_ref, wglu_ref, bglu_ref, wout_ref,
                gffn_ref, wr_ref, br_ref, tri_ref, h1_ref, hn_ref, ti_ref, tg_ref, counts_ref, cnt_ref):
    h1 = h_ref[...] + jnp.dot(a_ref[...], wout_ref[0:ATTN_WIDTH, :], preferred_element_type=F32)
    y = d_ref[...] * u_ref[...] + yf_ref[...] + yb_ref[...]
    y = jax.nn.gelu(y)
    z = jnp.dot(y.astype(BF16), wglu_ref[...], preferred_element_type=F32) + bglu_ref[...]
    s = y * jax.nn.sigmoid(z)
    h1 = h1 + jnp.dot(s.astype(BF16), wout_ref[ATTN_WIDTH:, :], preferred_element_type=F32)
    h1_ref[...] = h1
    hn = _rms(h1, gffn_ref[...])
    hn_ref[...] = hn
    hn_hi = hn.astype(BF16)
    hn_lo = (hn - hn_hi.astype(F32)).astype(BF16)
    logits = (jnp.dot(hn_hi, wr_ref[0], preferred_element_type=F32)
              + jnp.dot(hn_lo, wr_ref[0], preferred_element_type=F32)
              + jnp.dot(hn_hi, wr_ref[1], preferred_element_type=F32)) + br_ref[...]
    lane = lax.broadcasted_iota(jnp.int32, logits.shape, 1)
    neg = jnp.float32(-jnp.inf)
    work = logits
    ti = jnp.zeros(logits.shape, jnp.int32)
    tv = jnp.full(logits.shape, neg, F32)
    picked = jnp.zeros(logits.shape, F32)
    idxs = []
    for kk in range(TOP_K):
        mx = jnp.max(work, axis=-1, keepdims=True)
        idx = jnp.min(jnp.where(work == mx, lane, LANES), axis=-1, keepdims=True)
        idxs.append(idx)
        ti = jnp.where(lane == kk, idx, ti)
        tv = jnp.where(lane == kk, mx, tv)
        picked = jnp.where(lane == idx, 1.0, picked)
        work = jnp.where(lane == idx, neg, work)
    ex = jnp.exp(tv - jnp.max(tv, axis=-1, keepdims=True))
    tg_ref[...] = ex / jnp.sum(ex, axis=-1, keepdims=True)

    @pl.when(pl.program_id(0) == 0)
    def _():
        cnt_ref[...] = jnp.zeros_like(cnt_ref)

    before = jnp.dot(tri_ref[...], picked.astype(BF16), preferred_element_type=F32) + cnt_ref[0:1, :]
    for kk in range(TOP_K):
        rank = jnp.sum(jnp.where(lane == idxs[kk], before, 0.0), axis=-1, keepdims=True)
        ti = jnp.where(lane == TOP_K + kk, rank.astype(jnp.int32), ti)
    ti_ref[...] = ti
    cnt_ref[...] = cnt_ref[...] + jnp.sum(picked, axis=0, keepdims=True)
    counts_ref[...] = cnt_ref[...]


def _mix(h, a, u, yf, yb, d, wglu_bf, bglu, wout_bf, g_ffn, wr_pad, br_pad, tm):
    t = h.shape[0]
    row = lambda i: (i, 0)
    const = lambda i: (0, 0)
    full = lambda arr: pl.BlockSpec(arr.shape, lambda i: (0,) * arr.ndim)
    tri = jnp.asarray(np.tril(np.ones((tm, tm), np.float32), -1), BF16)
    return pl.pallas_call(
        _mix_kernel,
        grid=(t // tm,),
        in_specs=[pl.BlockSpec((tm, D_MODEL), row), pl.BlockSpec((tm, ATTN_WIDTH), row),
                  pl.BlockSpec((tm, SSM_WIDTH), row), pl.BlockSpec((tm, SSM_WIDTH), row),
                  pl.BlockSpec((tm, SSM_WIDTH), row),
                  full(d), full(wglu_bf), full(bglu), full(wout_bf), full(g_ffn), full(wr_pad),
                  full(br_pad), full(tri)],
        out_specs=[pl.BlockSpec((tm, D_MODEL), row), pl.BlockSpec((tm, D_MODEL), row),
                   pl.BlockSpec((tm, LANES), row), pl.BlockSpec((tm, LANES), row),
                   pl.BlockSpec((8, LANES), const)],
        out_shape=[jax.ShapeDtypeStruct((t, D_MODEL), F32), jax.ShapeDtypeStruct((t, D_MODEL), F32),
                   jax.ShapeDtypeStruct((t, LANES), jnp.int32), jax.ShapeDtypeStruct((t, LANES), F32),
                   jax.ShapeDtypeStruct((8, LANES), F32)],
        scratch_shapes=[pltpu.VMEM((8, LANES), F32)],
        compiler_params=_cparams(("arbitrary",)),
        name="mix",
    )(h, a, u, yf, yb, d, wglu_bf, bglu, wout_bf, g_ffn, wr_pad, br_pad, tri)


ROWS_PER_ISSUE = 4

def _dispatch_kernel(dest_ref, pad_start_ref, pad_len_ref, hn_ref, xs_ref, zrow_ref, sem, zsem, *, tm):
    @pl.when(pl.program_id(0) == 0)
    def _():
        zrow_ref[...] = jnp.zeros_like(zrow_ref)

        def zero_copy(row):
            return pltpu.make_async_copy(zrow_ref.at[pl.ds(0, 1), :], xs_ref.at[pl.ds(row, 1), :], zsem)

        def start_span(e, _):
            base = pad_start_ref[e]
            lax.fori_loop(0, pad_len_ref[e], lambda r, c: (zero_copy(base + r).start(), c)[1], 0)
            return 0

        def wait_span(e, _):
            lax.fori_loop(0, pad_len_ref[e], lambda r, c: (zero_copy(0).wait(), c)[1], 0)
            return 0

        lax.fori_loop(0, N_EXPERTS + 1, start_span, 0)
        lax.fori_loop(0, N_EXPERTS + 1, wait_span, 0)

    def issue(g, _):
        for rr in range(ROWS_PER_ISSUE):
            r = g * ROWS_PER_ISSUE + rr
            for kk in range(TOP_K):
                pltpu.make_async_copy(hn_ref.at[pl.ds(r, 1), :],
                                      xs_ref.at[pl.ds(dest_ref[r * TOP_K + kk], 1), :], sem).start()
        return 0

    lax.fori_loop(0, tm // ROWS_PER_ISSUE, issue, 0)

    def drain(g, _):
        for _unused in range(ROWS_PER_ISSUE * TOP_K):
            pltpu.make_async_copy(hn_ref.at[pl.ds(0, 1), :], xs_ref.at[pl.ds(0, 1), :], sem).wait()
        return 0

    lax.fori_loop(0, tm // ROWS_PER_ISSUE, drain, 0)


def _dispatch(dest_flat, pad_start, pad_len, hn, n_pad, tm):
    t = hn.shape[0]
    kernel = functools.partial(_dispatch_kernel, tm=tm)
    return pl.pallas_call(
        kernel,
        grid=(t // tm,),
        in_specs=[pl.BlockSpec((tm * TOP_K,), lambda i: (i,), memory_space=pltpu.SMEM),
                  pl.BlockSpec(memory_space=pltpu.SMEM), pl.BlockSpec(memory_space=pltpu.SMEM),
                  pl.BlockSpec((tm, D_MODEL), lambda i: (i, 0))],
        out_specs=pl.BlockSpec(memory_space=pl.ANY),
        out_shape=jax.ShapeDtypeStruct((n_pad, D_MODEL), F32),
        scratch_shapes=[pltpu.VMEM((8, D_MODEL), F32), pltpu.SemaphoreType.DMA(()),
                        pltpu.SemaphoreType.DMA(())],
        compiler_params=_cparams(("arbitrary",)),
        name="dispatch",
    )(dest_flat, pad_start, pad_len, hn)


def _experts_kernel(be_ref, x_ref, wg_ref, bg_ref, wu_ref, bu_ref, wd_ref, bd_ref, y_ref,
                    wgb_ref, wub_ref, wdb_ref):
    i = pl.program_id(0)

    @pl.when((i == 0) | (be_ref[i] != be_ref[jnp.maximum(i - 1, 0)]))
    def _():
        wgb_ref[...] = wg_ref[0].astype(BF16)
        wub_ref[...] = wu_ref[0].astype(BF16)
        wdb_ref[...] = wd_ref[0].astype(BF16)

    x = x_ref[...].astype(BF16)
    gt = jnp.minimum(jnp.dot(x, wgb_ref[...], preferred_element_type=F32) + bg_ref[0], SWIGLU_LIMIT)
    up = jnp.clip(jnp.dot(x, wub_ref[...], preferred_element_type=F32) + bu_ref[0],
                  -SWIGLU_LIMIT, SWIGLU_LIMIT)
    hh = (up + 1.0) * (gt * jax.nn.sigmoid(SWIGLU_ALPHA * gt))
    y_ref[...] = jnp.dot(hh.astype(BF16), wdb_ref[...], preferred_element_type=F32) + bd_ref[0]


def _experts(blk_expert, xs, wg, bg, wu, bu, wd, bd, eb):
    n_pad = xs.shape[0]
    row = lambda i, be: (i, 0)
    wmap = lambda i, be: (be[i], 0, 0)
    wspec = pl.BlockSpec((1, D_MODEL, D_MODEL), wmap)
    bspec = pl.BlockSpec((1, 1, D_MODEL), wmap)
    grid_spec = pltpu.PrefetchScalarGridSpec(
        num_scalar_prefetch=1,
        grid=(n_pad // eb,),
        in_specs=[pl.BlockSpec((eb, D_MODEL), row), wspec, bspec, wspec, bspec, wspec, bspec],
        out_specs=pl.BlockSpec((eb, D_MODEL), row),
        scratch_shapes=[pltpu.VMEM((D_MODEL, D_MODEL), BF16)] * 3,
    )
    return pl.pallas_call(
        _experts_kernel,
        grid_spec=grid_spec,
        out_shape=jax.ShapeDtypeStruct((n_pad, D_MODEL), F32),
        compiler_params=_cparams(("arbitrary",)),
        name="experts",
    )(blk_expert, xs, wg, bg, wu, bu, wd, bd)


def _tail_kernel(dcur_ref, dnxt_ref, h1_ref, tg_ref, p_ref, ys_ref, gple_ref, wpg_ref, wpp_ref, gfin_ref,
                 o_ref, bufa_ref, bufb_ref, sema, semb, *, th):
    i = pl.program_id(0)

    def row_copy(idx, buf_ref, kk, r, sem):
        return pltpu.make_async_copy(ys_ref.at[pl.ds(idx, 1), :], buf_ref.at[kk, pl.ds(r, 1), :], sem)

    def gather(d_ref, base, buf_ref, sem):
        for r in range(th):
            for kk in range(TOP_K):
                row_copy(d_ref[base + r * TOP_K + kk], buf_ref, kk, r, sem).start()

    def wait_all(buf_ref, sem):
        for _unused in range(th * TOP_K):
            row_copy(0, buf_ref, 0, 0, sem).wait()

    def compute(rows, buf_ref):
        proj = jnp.dot(p_ref[rows, :].astype(BF16), wpp_ref[...], preferred_element_type=F32)
        tg = tg_ref[rows, :]
        h2 = h1_ref[rows, :]
        for kk in range(TOP_K):
            h2 = h2 + tg[:, kk:kk + 1] * buf_ref[kk]
        gate = jax.nn.sigmoid(jnp.dot(_rms(h2, gple_ref[...]).astype(BF16), wpg_ref[...],
                                      preferred_element_type=F32))
        o_ref[rows, :] = _rms(h2 + gate * proj, gfin_ref[...])

    @pl.when(i == 0)
    def _():
        gather(dcur_ref, 0, bufa_ref, sema)

    wait_all(bufa_ref, sema)
    gather(dcur_ref, th * TOP_K, bufb_ref, semb)
    compute(slice(0, th), bufa_ref)
    wait_all(bufb_ref, semb)
    gather(dnxt_ref, 0, bufa_ref, sema)
    compute(slice(th, 2 * th), bufb_ref)

    @pl.when(i == pl.num_programs(0) - 1)
    def _():
        wait_all(bufa_ref, sema)


def _tail(dest_flat, h1, tg, p, ys, g_ple, wpg_bf, wpp_bf, g_final, th):
    t = h1.shape[0]
    tm = 2 * th
    n = t // tm
    row = lambda i: (i, 0)
    full = lambda arr: pl.BlockSpec(arr.shape, lambda i: (0,) * arr.ndim)
    kernel = functools.partial(_tail_kernel, th=th)
    return pl.pallas_call(
        kernel,
        grid=(n,),
        in_specs=[pl.BlockSpec((tm * TOP_K,), lambda i: (i,), memory_space=pltpu.SMEM),
                  pl.BlockSpec((tm * TOP_K,), lambda i: (jnp.minimum(i + 1, n - 1),),
                               memory_space=pltpu.SMEM),
                  pl.BlockSpec((tm, D_MODEL), row), pl.BlockSpec((tm, LANES), row),
                  pl.BlockSpec((tm, PLE_DIM), row),
                  pl.BlockSpec(memory_space=pl.ANY),
                  full(g_ple), full(wpg_bf), full(wpp_bf), full(g_final)],
        out_specs=pl.BlockSpec((tm, D_MODEL), row),
        out_shape=jax.ShapeDtypeStruct((t, D_MODEL), F32),
        scratch_shapes=[pltpu.VMEM((TOP_K, th, D_MODEL), F32), pltpu.VMEM((TOP_K, th, D_MODEL), F32),
                        pltpu.SemaphoreType.DMA(()), pltpu.SemaphoreType.DMA(())],
        compiler_params=_cparams(("arbitrary",)),
        name="tail",
    )(dest_flat, dest_flat, h1, tg, p, ys, g_ple, wpg_bf, wpp_bf, g_final)


def _routing(ti, counts, eb):
    t = ti.shape[0]
    top_i, rank = ti[:, :TOP_K], ti[:, TOP_K:2 * TOP_K]
    counts = counts[0, :N_EXPERTS].astype(jnp.int32)
    padded = ((counts + eb - 1) // eb) * eb
    pends = jnp.cumsum(padded)
    pstarts = pends - padded
    onehot = top_i[:, :, None] == jnp.arange(N_EXPERTS, dtype=jnp.int32)
    dest = rank + jnp.sum(jnp.where(onehot, pstarts, 0), axis=-1)
    n_blocks = -(-(t * TOP_K + N_EXPERTS * (eb - 1)) // eb)
    blk_start = jnp.arange(n_blocks, dtype=jnp.int32) * eb
    blk_expert = jnp.minimum(jnp.sum(blk_start[:, None] >= pends[None, :], axis=1), N_EXPERTS - 1)
    n_pad = n_blocks * eb
    pad_start = jnp.concatenate([pstarts + counts, pends[-1:]]).astype(jnp.int32)
    pad_len = jnp.concatenate([padded - counts, n_pad - pends[-1:]]).astype(jnp.int32)
    return (dest.reshape(-1).astype(jnp.int32), blk_expert.astype(jnp.int32), pad_start, pad_len, n_pad)


def _pick(n, prefs):
    for c in prefs:
        if n % c == 0:
            return c
    raise ValueError(f"no tile for {n}")


def _trunk(x, p, w, lambda_init):
    b, l, _ = x.shape
    t = b * l
    tm = TOKEN_TILE
    assert l % tm == 0
    tc = _pick(l, (256, 128))
    eb = 256

    q, k, vt, u, nrm = _inproj(x.reshape(t, D_MODEL), w['g_mix'], w['w_qku'], w['w_vt'], tm)
    a = _attention(q.reshape(b, l, -1), k.reshape(b, l, -1), vt, w['attn_scal'],
                   _score_bounds(nrm, b, l // tm), w['attn_qf'], w['attn_kf'], w['g_subln'],
                   lambda_init, tm)
    u3 = u.reshape(b, l, SSM_WIDTH)
    pm, pmt = _perm_matrices(tc)
    ys = _s5(u3, pm, pmt, w['s5'][0], w['s5'][1], tc)
    h1, hn, ti, tg, counts = _mix(x.reshape(t, D_MODEL), a.reshape(t, ATTN_WIDTH), u,
                                  ys[0].reshape(t, SSM_WIDTH), ys[1].reshape(t, SSM_WIDTH),
                                  w['ssm_d'], w['w_glu'], w['b_glu'], w['w_out'], w['g_ffn'],
                                  w['w_router'], w['b_router'], tm)
    dest, blk_expert, pad_start, pad_len, n_pad = _routing(ti, counts, eb)
    xs = _dispatch(dest, pad_start, pad_len, hn, n_pad, tm // 2)
    ye = _experts(blk_expert, xs, w['w_gate'], w['b_gate'], w['w_up'], w['b_up'],
                  w['w_down'], w['b_down'], eb)
    out = _tail(dest, h1, tg, p.reshape(t, PLE_DIM), ye, w['g_ple'], w['w_ple_gate'],
                w['w_ple_proj'], w['g_final'], tm // 2)
    return out.reshape(b, l, D_MODEL)


def _hi_lo(x):
    hi = x.astype(BF16)
    return jnp.stack([hi, (x - hi.astype(F32)).astype(BF16)])


def _prepare(i, g_mix, w_in, lambda_q1, lambda_k1, lambda_q2, lambda_k2, g_subln, ssm_lambda_re,
             ssm_lambda_im, ssm_log_dt, ssm_b_re, ssm_b_im, ssm_c_re, ssm_c_im, ssm_d, w_glu, b_glu,
             w_out, g_ffn, w_router, b_router, w_gate, b_gate, w_up, b_up, w_down, b_down, g_ple,
             w_ple_gate, w_ple_proj, g_final, tcs):
    lambda_init = 0.8 - 0.6 * math.exp(-0.3 * i)
    lam = (jnp.exp(jnp.sum(lambda_q1[i] * lambda_k1[i]))
           - jnp.exp(jnp.sum(lambda_q2[i] * lambda_k2[i])) + lambda_init)
    slopes = jnp.exp2(-8.0 * jnp.arange(1, N_HEADS + 1, dtype=F32) / N_HEADS)
    bslopes = slopes * LOG2E
    parts, rest = [], bslopes
    for _ in range(N_BIAS_PARTS):
        part = rest.astype(BF16).astype(F32)
        parts.append(part)
        rest = rest - part
    attn_qf, attn_kf = _attn_features(jnp.stack(parts, axis=1), TOKEN_TILE)
    w_qku = jnp.concatenate([w_in[i][:, :2 * ATTN_WIDTH], w_in[i][:, 3 * ATTN_WIDTH:]], axis=1)
    row = lambda vec: vec.reshape(1, -1).astype(F32)
    w = {
        'g_mix': g_mix[i], 'w_qku': w_qku.astype(BF16),
        'w_vt': w_in[i][:, 2 * ATTN_WIDTH:3 * ATTN_WIDTH].T.astype(BF16),
        'attn_scal': jnp.concatenate([lam.reshape(1), bslopes, 1.0 / bslopes]).astype(F32),
        'attn_qf': attn_qf, 'attn_kf': attn_kf,
        'g_subln': g_subln[i],
        'ssm_d': row(ssm_d[i]), 'w_glu': w_glu[i].astype(BF16), 'b_glu': row(b_glu[i]),
        'w_out': w_out[i].astype(BF16), 'g_ffn': row(g_ffn[i]),
        'w_router': _hi_lo(jnp.pad(w_router[i], ((0, 0), (0, LANES - N_EXPERTS)))),
        'b_router': jnp.pad(row(b_router[i]), ((0, 0), (0, LANES - N_EXPERTS)),
                            constant_values=-jnp.inf),
        'w_gate': w_gate[i], 'b_gate': b_gate[i].reshape(N_EXPERTS, 1, D_MODEL),
        'w_up': w_up[i], 'b_up': b_up[i].reshape(N_EXPERTS, 1, D_MODEL),
        'w_down': w_down[i], 'b_down': b_down[i].reshape(N_EXPERTS, 1, D_MODEL),
        'g_ple': row(g_ple[i]), 'w_ple_gate': w_ple_gate[i].astype(BF16),
        'w_ple_proj': w_ple_proj[i].astype(BF16), 'g_final': row(g_final),
    }
    w['s5'] = {tc: [_s5_params(ssm_lambda_re[i, dr], ssm_lambda_im[i, dr], ssm_log_dt[i, dr],
                               ssm_b_re[i, dr], ssm_b_im[i, dr], ssm_c_re[i, dr], ssm_c_im[i, dr],
                               tc, reverse=(dr == 1)) for dr in range(2)] for tc in tcs}
    return w, lambda_init


def kernel(x_prompt, x_sample, p_prompt, p_sample, g_mix, w_in, lambda_q1, lambda_k1, lambda_q2, lambda_k2, g_subln, ssm_lambda_re, ssm_lambda_im, ssm_log_dt, ssm_b_re, ssm_b_im, ssm_c_re, ssm_c_im, ssm_d, w_glu, b_glu, w_out, g_ffn, w_router, b_router, w_gate, b_gate, w_up, b_up, w_down, b_down, g_ple, w_ple_gate, w_ple_proj, g_final):
    assert w_in.shape[0] == 1, "single-layer trunk"
    tcs = {_pick(x.shape[1], (256, 128)) for x in (x_prompt, x_sample)}
    w, lambda_init = _prepare(0, g_mix, w_in, lambda_q1, lambda_k1, lambda_q2, lambda_k2, g_subln,
                              ssm_lambda_re, ssm_lambda_im, ssm_log_dt, ssm_b_re, ssm_b_im, ssm_c_re,
                              ssm_c_im, ssm_d, w_glu, b_glu, w_out, g_ffn, w_router, b_router, w_gate,
                              b_gate, w_up, b_up, w_down, b_down, g_ple, w_ple_gate, w_ple_proj,
                              g_final, tcs)
    outs = []
    for x, p in ((x_prompt, p_prompt), (x_sample, p_sample)):
        tc = _pick(x.shape[1], (256, 128))
        wt = dict(w, s5=w['s5'][tc])
        outs.append(_trunk(x, p[0], wt, lambda_init))
    return tuple(outs)
```
